```python
import jax, jax.numpy as jnp
from jax import lax
import numpy as np

D_MODEL = 4096
BATCH = 16
SEQ = 2048
DEPTH = 1

MIX_WIDTH = D_MODEL
HEAD_DIM = 128
CONV_WIDTH = MIX_WIDTH // 2
GMLP_WIDTH = MIX_WIDTH - CONV_WIDTH
CONV_GROUPS = CONV_WIDTH // HEAD_DIM
GMLP_HEADS = GMLP_WIDTH // HEAD_DIM
CONV_K = 3
CHUNK = 128
D_FF = 4 * D_MODEL
IN_PROJ_WIDTH = 3 * CONV_WIDTH + 2 * GMLP_WIDTH
EPS = 1e-5

kernel_name = "hybrid_shortconv_gmlp_block"


def rmsnorm(x, g):
    xf = x.astype(jnp.float32)
    inv = lax.rsqrt(jnp.mean(xf * xf, axis=-1, keepdims=True) + EPS)
    return (xf * inv * g.astype(jnp.float32)).astype(x.dtype)


def short_conv_mixer(b_gate, c_gate, h_in, conv_w):
    h = c_gate * h_in
    S = h.shape[1]
    hp = jnp.pad(h, ((0, 0), (CONV_K - 1, 0), (0, 0)))
    y = conv_w[0] * hp[:, 0:S]
    for k in range(1, CONV_K):
        y = y + conv_w[k] * hp[:, k:k + S]
    return b_gate * y


def chunked_spatial_gating(u, v, spatial_w, spatial_b):
    bsz, S, _ = v.shape
    n_chunks = S // CHUNK
    causal = jnp.tril(jnp.ones((CHUNK, CHUNK), dtype=bool))
    w = jnp.where(causal[None], spatial_w, jnp.zeros((), spatial_w.dtype))
    vc = v.reshape(bsz, n_chunks, CHUNK, GMLP_HEADS, HEAD_DIM)
    s = jnp.einsum('hts,bcshd->bcthd', w, vc) + spatial_b.T[None, None, :, :, None]
    return u * s.reshape(bsz, S, GMLP_WIDTH)


def _fwd_setup_inputs(seed: int = 0) -> dict:
    key = jax.random.key(seed)
    ks = jax.random.split(key, 13)
    f32 = jnp.float32
    x = jax.random.normal(ks[0], (BATCH, SEQ, D_MODEL), f32)
    mix_norm_g = 1.0 + 0.02 * jax.random.normal(ks[1], (DEPTH, D_MODEL), f32)
    w_in = jax.random.normal(ks[2], (DEPTH, D_MODEL, IN_PROJ_WIDTH), f32) * D_MODEL ** -0.5
    conv_w = jax.random.normal(ks[3], (DEPTH, CONV_K, CONV_WIDTH), f32) * CONV_K ** -0.5
    spatial_w = jax.random.normal(ks[4], (DEPTH, GMLP_HEADS, CHUNK, CHUNK), f32) * (0.5 * CHUNK ** -0.5)
    spatial_b = 1.0 + 0.02 * jax.random.normal(ks[5], (DEPTH, GMLP_HEADS, CHUNK), f32)
    conv_out_norm_g = 1.0 + 0.02 * jax.random.normal(ks[6], (DEPTH, CONV_WIDTH), f32)
    gmlp_out_norm_g = 1.0 + 0.02 * jax.random.normal(ks[7], (DEPTH, GMLP_WIDTH), f32)
    w_out = jax.random.normal(ks[8], (DEPTH, MIX_WIDTH, D_MODEL), f32) * MIX_WIDTH ** -0.5
    mlp_norm_g = 1.0 + 0.02 * jax.random.normal(ks[9], (DEPTH, D_MODEL), f32)
    w_up = jax.random.normal(ks[10], (DEPTH, D_MODEL, D_FF), f32) * D_MODEL ** -0.5
    w_down = jax.random.normal(ks[11], (DEPTH, D_FF, D_MODEL), f32) * D_FF ** -0.5
    final_norm_g = 1.0 + 0.02 * jax.random.normal(ks[12], (D_MODEL,), f32)
    return {"x": x, "mix_norm_g": mix_norm_g, "w_in": w_in, "conv_w": conv_w,
            "spatial_w": spatial_w, "spatial_b": spatial_b,
            "conv_out_norm_g": conv_out_norm_g, "gmlp_out_norm_g": gmlp_out_norm_g,
            "w_out": w_out, "mlp_norm_g": mlp_norm_g, "w_up": w_up, "w_down": w_down,
            "final_norm_g": final_norm_g}


def _fwd_reference(x, mix_norm_g, w_in, conv_w, spatial_w, spatial_b, conv_out_norm_g,
              gmlp_out_norm_g, w_out, mlp_norm_g, w_up, w_down, final_norm_g):
    split_points = [CONV_WIDTH, 2 * CONV_WIDTH, 3 * CONV_WIDTH, 3 * CONV_WIDTH + GMLP_WIDTH]
    h = x
    for l in range(DEPTH):
        xn = rmsnorm(h, mix_norm_g[l])
        proj = jnp.einsum('bsd,de->bse', xn, w_in[l])
        b_gate, c_gate, h_in, u, v = jnp.split(proj, split_points, axis=-1)
        y_a = short_conv_mixer(b_gate, c_gate, h_in, conv_w[l])
        y_b = chunked_spatial_gating(jax.nn.gelu(u), jax.nn.gelu(v),
                                     spatial_w[l], spatial_b[l])
        y = jnp.concatenate([rmsnorm(y_a, conv_out_norm_g[l]),
                             rmsnorm(y_b, gmlp_out_norm_g[l])], axis=-1)
        h = h + jnp.einsum('bse,ed->bsd', y, w_out[l])
        xn = rmsnorm(h, mlp_norm_g[l])
        a = jnp.square(jax.nn.relu(jnp.einsum('bsd,df->bsf', xn, w_up[l])))
        h = h + jnp.einsum('bsf,fd->bsd', a, w_down[l])
    return rmsnorm(h, final_norm_g)


import jax as _jax
import jax.numpy as _jnp

TWIN_FORMAT = 'train_step'
FWD_PARAMS = ['x', 'mix_norm_g', 'w_in', 'conv_w', 'spatial_w', 'spatial_b', 'conv_out_norm_g', 'gmlp_out_norm_g', 'w_out', 'mlp_norm_g', 'w_up', 'w_down', 'final_norm_g']
TWIN_WEIGHTS = ['mix_norm_g', 'w_in', 'conv_w', 'spatial_w', 'spatial_b', 'conv_out_norm_g', 'gmlp_out_norm_g', 'w_out', 'mlp_norm_g', 'w_up', 'w_down', 'final_norm_g']
TWIN_DIFF_INPUT = 'x'
TWIN_INPUTS = ['x', 'mix_norm_g', 'w_in', 'conv_w', 'spatial_w', 'spatial_b', 'conv_out_norm_g', 'gmlp_out_norm_g', 'w_out', 'mlp_norm_g', 'w_up', 'w_down', 'final_norm_g', 'loss_target', 'm_mix_norm_g', 'm_w_in', 'm_conv_w', 'm_spatial_w', 'm_spatial_b', 'm_conv_out_norm_g', 'm_gmlp_out_norm_g', 'm_w_out', 'm_mlp_norm_g', 'm_w_up', 'm_w_down', 'm_final_norm_g', 'v_mix_norm_g', 'v_w_in', 'v_conv_w', 'v_spatial_w', 'v_spatial_b', 'v_conv_out_norm_g', 'v_gmlp_out_norm_g', 'v_w_out', 'v_mlp_norm_g', 'v_w_up', 'v_w_down', 'v_final_norm_g']
TWIN_OUTPUTS = ['loss', 'grad_x', 'grad_mix_norm_g', 'grad_w_in', 'grad_conv_w', 'grad_spatial_w', 'grad_spatial_b', 'grad_conv_out_norm_g', 'grad_gmlp_out_norm_g', 'grad_w_out', 'grad_mlp_norm_g', 'grad_w_up', 'grad_w_down', 'grad_final_norm_g', 'delta_mix_norm_g', 'delta_w_in', 'delta_conv_w', 'delta_spatial_w', 'delta_spatial_b', 'delta_conv_out_norm_g', 'delta_gmlp_out_norm_g', 'delta_w_out', 'delta_mlp_norm_g', 'delta_w_up', 'delta_w_down', 'delta_final_norm_g', 'new_m_mix_norm_g', 'new_m_w_in', 'new_m_conv_w', 'new_m_spatial_w', 'new_m_spatial_b', 'new_m_conv_out_norm_g', 'new_m_gmlp_out_norm_g', 'new_m_w_out', 'new_m_mlp_norm_g', 'new_m_w_up', 'new_m_w_down', 'new_m_final_norm_g', 'new_v_mix_norm_g', 'new_v_w_in', 'new_v_conv_w', 'new_v_spatial_w', 'new_v_spatial_b', 'new_v_conv_out_norm_g', 'new_v_gmlp_out_norm_g', 'new_v_w_out', 'new_v_mlp_norm_g', 'new_v_w_up', 'new_v_w_down', 'new_v_final_norm_g']
TWIN_LEAF_KINDS = {'loss': 'loss', 'grad_x': 'grad_x', 'grad_mix_norm_g': 'grad_w', 'grad_w_in': 'grad_w', 'grad_conv_w': 'grad_w', 'grad_spatial_w': 'grad_w', 'grad_spatial_b': 'grad_w', 'grad_conv_out_norm_g': 'grad_w', 'grad_gmlp_out_norm_g': 'grad_w', 'grad_w_out': 'grad_w', 'grad_mlp_norm_g': 'grad_w', 'grad_w_up': 'grad_w', 'grad_w_down': 'grad_w', 'grad_final_norm_g': 'grad_w', 'delta_mix_norm_g': 'delta_w', 'delta_w_in': 'delta_w', 'delta_conv_w': 'delta_w', 'delta_spatial_w': 'delta_w', 'delta_spatial_b': 'delta_w', 'delta_conv_out_norm_g': 'delta_w', 'delta_gmlp_out_norm_g': 'delta_w', 'delta_w_out': 'delta_w', 'delta_mlp_norm_g': 'delta_w', 'delta_w_up': 'delta_w', 'delta_w_down': 'delta_w', 'delta_final_norm_g': 'delta_w', 'new_m_mix_norm_g': 'new_m', 'new_m_w_in': 'new_m', 'new_m_conv_w': 'new_m', 'new_m_spatial_w': 'new_m', 'new_m_spatial_b': 'new_m', 'new_m_conv_out_norm_g': 'new_m', 'new_m_gmlp_out_norm_g': 'new_m', 'new_m_w_out': 'new_m', 'new_m_mlp_norm_g': 'new_m', 'new_m_w_up': 'new_m', 'new_m_w_down': 'new_m', 'new_m_final_norm_g': 'new_m', 'new_v_mix_norm_g': 'new_v', 'new_v_w_in': 'new_v', 'new_v_conv_w': 'new_v', 'new_v_spatial_w': 'new_v', 'new_v_spatial_b': 'new_v', 'new_v_conv_out_norm_g': 'new_v', 'new_v_gmlp_out_norm_g': 'new_v', 'new_v_w_out': 'new_v', 'new_v_mlp_norm_g': 'new_v', 'new_v_w_up': 'new_v', 'new_v_w_down': 'new_v', 'new_v_final_norm_g': 'new_v'}


def _forward(args):
    return _fwd_reference(*[args[k] for k in FWD_PARAMS])


def _output_shape():
    def fwd():
        inp = _fwd_setup_inputs(0)
        return _fwd_reference(*[inp[k] for k in FWD_PARAMS])
    out = _jax.eval_shape(fwd)
    return out.shape, out.dtype

N_MICROBATCH = 1
ADAM_LR = 0.001
ADAM_B1 = 0.9
ADAM_B2 = 0.999
ADAM_EPS = 1e-08
ADAM_WD = 0.01
ADAM_STEP = 10
PER_EXAMPLE_BATCH_AXIS = {'x': 0, 'loss_target': 0}
SHARED_INPUTS = []
_WEIGHT_DTYPES = {'mix_norm_g': _jnp.float32, 'w_in': _jnp.float32, 'conv_w': _jnp.float32, 'spatial_w': _jnp.float32, 'spatial_b': _jnp.float32, 'conv_out_norm_g': _jnp.float32, 'gmlp_out_norm_g': _jnp.float32, 'w_out': _jnp.float32, 'mlp_norm_g': _jnp.float32, 'w_up': _jnp.float32, 'w_down': _jnp.float32, 'final_norm_g': _jnp.float32}
MOMENT_SCALE = {'mix_norm_g': 4.893857e-02, 'w_in': 3.058730e-02, 'conv_w': 3.384040e-02, 'spatial_w': 1.491064e-02, 'spatial_b': 3.216626e-02, 'conv_out_norm_g': 3.483226e-02, 'gmlp_out_norm_g': 3.826783e-02, 'w_out': 3.595456e-02, 'mlp_norm_g': 3.461095e-02, 'w_up': 1.712280e-02, 'w_down': 3.511485e-02, 'final_norm_g': 8.085360e+00}


def _to_microbatches(a, axis):
    t = _jnp.moveaxis(a, axis, 0)
    t = t.reshape((N_MICROBATCH, t.shape[0] // N_MICROBATCH) + t.shape[1:])
    return _jnp.moveaxis(t, 1, axis + 1)


def setup_inputs(seed: int = 0) -> dict:
    inp = _fwd_setup_inputs(seed)
    key = _jax.random.fold_in(_jax.random.key(seed), 7919)
    shape, _ = _output_shape()
    out = dict(inp)
    out["loss_target"] = _jax.random.normal(_jax.random.fold_in(key, 0), shape, _jnp.float32)
    for i, name in enumerate(TWIN_WEIGHTS):
        w = inp[name].astype(_jnp.float32)
        if MOMENT_SCALE is None:
            s = _jnp.sqrt(_jnp.mean(_jnp.square(w)) + 1e-30)
        else:
            s = MOMENT_SCALE[name]
        km, kv = _jax.random.split(_jax.random.fold_in(key, i + 1))
        out[name] = w
        out["m_" + name] = s * _jax.random.normal(km, w.shape, _jnp.float32)
        out["v_" + name] = (s * s) * _jax.random.uniform(kv, w.shape, _jnp.float32, 0.5, 1.5)
    if N_MICROBATCH > 1:
        for name, axis in PER_EXAMPLE_BATCH_AXIS.items():
            out[name] = _to_microbatches(out[name], axis)
    return {'x': out['x'], 'mix_norm_g': out['mix_norm_g'], 'w_in': out['w_in'], 'conv_w': out['conv_w'], 'spatial_w': out['spatial_w'], 'spatial_b': out['spatial_b'], 'conv_out_norm_g': out['conv_out_norm_g'], 'gmlp_out_norm_g': out['gmlp_out_norm_g'], 'w_out': out['w_out'], 'mlp_norm_g': out['mlp_norm_g'], 'w_up': out['w_up'], 'w_down': out['w_down'], 'final_norm_g': out['final_norm_g'], 'loss_target': out['loss_target'], 'm_mix_norm_g': out['m_mix_norm_g'], 'm_w_in': out['m_w_in'], 'm_conv_w': out['m_conv_w'], 'm_spatial_w': out['m_spatial_w'], 'm_spatial_b': out['m_spatial_b'], 'm_conv_out_norm_g': out['m_conv_out_norm_g'], 'm_gmlp_out_norm_g': out['m_gmlp_out_norm_g'], 'm_w_out': out['m_w_out'], 'm_mlp_norm_g': out['m_mlp_norm_g'], 'm_w_up': out['m_w_up'], 'm_w_down': out['m_w_down'], 'm_final_norm_g': out['m_final_norm_g'], 'v_mix_norm_g': out['v_mix_norm_g'], 'v_w_in': out['v_w_in'], 'v_conv_w': out['v_conv_w'], 'v_spatial_w': out['v_spatial_w'], 'v_spatial_b': out['v_spatial_b'], 'v_conv_out_norm_g': out['v_conv_out_norm_g'], 'v_gmlp_out_norm_g': out['v_gmlp_out_norm_g'], 'v_w_out': out['v_w_out'], 'v_mlp_norm_g': out['v_mlp_norm_g'], 'v_w_up': out['v_w_up'], 'v_w_down': out['v_w_down'], 'v_final_norm_g': out['v_final_norm_g']}


def _loss(weights, diff, rest, loss_target):
    with _jax.named_scope("forward"):
        args = {**rest, TWIN_DIFF_INPUT: diff, **{k: w.astype(_WEIGHT_DTYPES[k]) for k, w in weights.items()}}
        y = _forward(args)
    with _jax.named_scope("loss_head"):
        err = _jnp.square(y.astype(_jnp.float32) - loss_target)
        return 0.5 * _jnp.sum(_jnp.mean(err, axis=-1)) if err.ndim else 0.5 * err


def _adamw(w, g, m, v):
    m = ADAM_B1 * m + (1.0 - ADAM_B1) * g
    v = ADAM_B2 * v + (1.0 - ADAM_B2) * _jnp.square(g)
    m_hat = m / (1.0 - ADAM_B1 ** ADAM_STEP)
    v_hat = v / (1.0 - ADAM_B2 ** ADAM_STEP)
    delta = -ADAM_LR * (m_hat / (_jnp.sqrt(v_hat) + ADAM_EPS) + ADAM_WD * w)
    return delta, m, v


def reference(x, mix_norm_g, w_in, conv_w, spatial_w, spatial_b, conv_out_norm_g, gmlp_out_norm_g, w_out, mlp_norm_g, w_up, w_down, final_norm_g, loss_target, m_mix_norm_g, m_w_in, m_conv_w, m_spatial_w, m_spatial_b, m_conv_out_norm_g, m_gmlp_out_norm_g, m_w_out, m_mlp_norm_g, m_w_up, m_w_down, m_final_norm_g, v_mix_norm_g, v_w_in, v_conv_w, v_spatial_w, v_spatial_b, v_conv_out_norm_g, v_gmlp_out_norm_g, v_w_out, v_mlp_norm_g, v_w_up, v_w_down, v_final_norm_g):
    given = dict(x=x, mix_norm_g=mix_norm_g, w_in=w_in, conv_w=conv_w, spatial_w=spatial_w, spatial_b=spatial_b, conv_out_norm_g=conv_out_norm_g, gmlp_out_norm_g=gmlp_out_norm_g, w_out=w_out, mlp_norm_g=mlp_norm_g, w_up=w_up, w_down=w_down, final_norm_g=final_norm_g, loss_target=loss_target, m_mix_norm_g=m_mix_norm_g, m_w_in=m_w_in, m_conv_w=m_conv_w, m_spatial_w=m_spatial_w, m_spatial_b=m_spatial_b, m_conv_out_norm_g=m_conv_out_norm_g, m_gmlp_out_norm_g=m_gmlp_out_norm_g, m_w_out=m_w_out, m_mlp_norm_g=m_mlp_norm_g, m_w_up=m_w_up, m_w_down=m_w_down, m_final_norm_g=m_final_norm_g, v_mix_norm_g=v_mix_norm_g, v_w_in=v_w_in, v_conv_w=v_conv_w, v_spatial_w=v_spatial_w, v_spatial_b=v_spatial_b, v_conv_out_norm_g=v_conv_out_norm_g, v_gmlp_out_norm_g=v_gmlp_out_norm_g, v_w_out=v_w_out, v_mlp_norm_g=v_mlp_norm_g, v_w_up=v_w_up, v_w_down=v_w_down, v_final_norm_g=v_final_norm_g)
    weights = {n: given[n] for n in TWIN_WEIGHTS}
    shared = {n: given[n] for n in SHARED_INPUTS}
    per_example = {n: given[n] for n in ['x']}
    grad_fn = _jax.value_and_grad(_loss, argnums=(0, 1))

    def one_microbatch(ex, loss_target):
        ex = dict(ex)
        diff = ex.pop(TWIN_DIFF_INPUT)
        return grad_fn(weights, diff, {**shared, **ex}, loss_target)

    if N_MICROBATCH == 1:
        loss, (grad_w, grad_x) = one_microbatch(per_example, given["loss_target"])
    else:
        def body(carry, xs):
            loss_sum, grad_sum = carry
            l_k, (gw_k, gx_k) = one_microbatch(xs[0], xs[1])
            with _jax.named_scope("update"):
                return (loss_sum + l_k, _jax.tree.map(_jnp.add, grad_sum, gw_k)), gx_k

        init = (_jnp.zeros((), _jnp.float32), _jax.tree.map(_jnp.zeros_like, weights))
        (loss, grad_w), grad_x = _jax.lax.scan(body, init, (per_example, given["loss_target"]))
    with _jax.named_scope("update"):
        delta_w, new_m, new_v = {}, {}, {}
        for n in TWIN_WEIGHTS:
            delta_w[n], new_m[n], new_v[n] = _adamw(weights[n], grad_w[n], given["m_" + n], given["v_" + n])
    return (loss, grad_x, *[grad_w[n] for n in TWIN_WEIGHTS], *[delta_w[n] for n in TWIN_WEIGHTS],
            *[new_m[n] for n in TWIN_WEIGHTS], *[new_v[n] for n in TWIN_WEIGHTS])
```

```python
import functools
import math

import jax
import jax.numpy as jnp
from jax import lax
from jax.experimental import pallas as pl
from jax.experimental.pallas import tpu as pltpu

F32 = jnp.float32
BF16 = jnp.bfloat16
MESH = pl.DeviceIdType.MESH
HBM = pltpu.HBM

EPS = 1e-5
HEAD = 128
CONV_K = 3
N_DEV = 8
N_CHIP = 4
VMEM_LIMIT_BYTES = 56 * 1024 * 1024

ADAM_LR = 0.001
ADAM_B1 = 0.9
ADAM_B2 = 0.999
ADAM_EPS = 1e-08
ADAM_WD = 0.01
ADAM_STEP = 10

GELU_K0 = math.sqrt(2.0 / math.pi)
GELU_K1 = 0.044715

NN = (((1,), (0,)), ((), ()))
NT = (((1,), (1,)), ((), ()))
TN = (((0,), (0,)), ((), ()))


def _params(semantics):
    return pltpu.CompilerParams(dimension_semantics=semantics, vmem_limit_bytes=VMEM_LIMIT_BYTES)


def _tile(dim, want):
    if dim <= want:
        return dim
    for t in range(want - want % 8, 0, -8):
        if dim % t == 0:
            return t
    raise ValueError((dim, want))


def _matmul(name, grid, dims, operands, in_specs, out_shapes, out_specs, acc_shape, epilogue):
    n_in = len(operands)
    n_out = len(out_shapes)
    nk = grid[2]

    def body(*refs):
        a_ref, b_ref = refs[0], refs[1]
        extra = refs[2:n_in]
        outs = refs[n_in:n_in + n_out]
        part = lax.dot_general(a_ref[...], b_ref[...], dims, preferred_element_type=F32)

        def finish(acc):
            res = epilogue(acc, *[e[...] for e in extra])
            for o, r in zip(outs, res):
                o[...] = r.astype(o.dtype)

        if nk == 1:
            finish(part)
        else:
            acc_ref = refs[n_in + n_out]
            k = pl.program_id(2)

            @pl.when(k == 0)
            def _():
                acc_ref[...] = part

            @pl.when(k > 0)
            def _():
                acc_ref[...] += part

            @pl.when(k == nk - 1)
            def _():
                finish(acc_ref[...])

    scratch = [pltpu.VMEM(acc_shape, F32)] if nk > 1 else []
    return pl.pallas_call(
        body, name=name, grid=grid, in_specs=in_specs, out_specs=out_specs, out_shape=out_shapes,
        scratch_shapes=scratch, compiler_params=_params(("parallel", "parallel", "arbitrary")),
    )(*operands)


def _ident(acc):
    return (acc,)


ROW_TILE = 256


def _rms_fwd(x, g, name):
    T, D = x.shape
    tr = _tile(T, ROW_TILE)

    def body(x_ref, g_ref, o_ref):
        xv = x_ref[...]
        inv = lax.rsqrt(jnp.mean(xv * xv, axis=-1, keepdims=True) + EPS)
        o_ref[...] = (xv * inv * g_ref[...]).astype(o_ref.dtype)

    return pl.pallas_call(
        body, name=name, grid=(T // tr,),
        in_specs=[pl.BlockSpec((tr, D), lambda i: (i, 0)), pl.BlockSpec((1, D), lambda i: (0, 0))],
        out_specs=pl.BlockSpec((tr, D), lambda i: (i, 0)),
        out_shape=jax.ShapeDtypeStruct((T, D), BF16),
        compiler_params=_params(("parallel",)),
    )(x, g)


def _rms_bwd(dy, x, g, dres, name, want_bf16):
    T, D = x.shape
    tr = _tile(T, ROW_TILE)

    def body(dy_ref, x_ref, g_ref, dres_ref, *outs):
        dx_ref, gg_ref = outs[0], outs[-1]
        i = pl.program_id(0)
        xv = x_ref[...]
        dyv = dy_ref[...]
        inv = lax.rsqrt(jnp.mean(xv * xv, axis=-1, keepdims=True) + EPS)
        gd = dyv * g_ref[...]
        dot = jnp.mean(gd * xv, axis=-1, keepdims=True)
        dx = dres_ref[...] + (inv * gd - xv * (inv * inv * inv * dot))
        dx_ref[...] = dx
        if want_bf16:
            outs[1][...] = dx.astype(BF16)
        part = jnp.sum(dyv * xv * inv, axis=0, keepdims=True)

        @pl.when(i == 0)
        def _():
            gg_ref[...] = part

        @pl.when(i > 0)
        def _():
            gg_ref[...] += part

    row = pl.BlockSpec((tr, D), lambda i: (i, 0))
    vec = pl.BlockSpec((1, D), lambda i: (0, 0))
    out_shape = [jax.ShapeDtypeStruct((T, D), F32)]
    out_specs = [row]
    if want_bf16:
        out_shape.append(jax.ShapeDtypeStruct((T, D), BF16))
        out_specs.append(row)
    out_shape.append(jax.ShapeDtypeStruct((1, D), F32))
    out_specs.append(vec)
    return pl.pallas_call(
        body, name=name, grid=(T // tr,), in_specs=[row, row, vec, row],
        out_specs=out_specs, out_shape=out_shape, compiler_params=_params(("arbitrary",)),
    )(dy, x, g, dres)


def _loss_head(h, target, g):
    T, D = h.shape
    tr = _tile(T, ROW_TILE)

    def body(h_ref, t_ref, g_ref, dh_ref, dhb_ref, gg_ref, loss_ref):
        i = pl.program_id(0)
        hv = h_ref[...]
        gv = g_ref[...]
        inv = lax.rsqrt(jnp.mean(hv * hv, axis=-1, keepdims=True) + EPS)
        diff = hv * inv * gv - t_ref[...]
        lpart = 0.5 * jnp.sum(jnp.mean(diff * diff, axis=-1, keepdims=True), axis=0, keepdims=True)
        dout = diff * (1.0 / D)
        gd = dout * gv
        dot = jnp.mean(gd * hv, axis=-1, keepdims=True)
        dh = inv * gd - hv * (inv * inv * inv * dot)
        dh_ref[...] = dh
        dhb_ref[...] = dh.astype(BF16)
        part = jnp.sum(dout * hv * inv, axis=0, keepdims=True)
        lrow = jnp.broadcast_to(lpart, (1, 128))

        @pl.when(i == 0)
        def _():
            gg_ref[...] = part
            loss_ref[...] = lrow

        @pl.when(i > 0)
        def _():
            gg_ref[...] += part
            loss_ref[...] += lrow

    row = pl.BlockSpec((tr, D), lambda i: (i, 0))
    vec = pl.BlockSpec((1, D), lambda i: (0, 0))
    return pl.pallas_call(
        body, name="loss_head", grid=(T // tr,), in_specs=[row, row, vec],
        out_specs=[row, row, vec, pl.BlockSpec((1, 128), lambda i: (0, 0))],
        out_shape=[jax.ShapeDtypeStruct((T, D), F32), jax.ShapeDtypeStruct((T, D), BF16),
                   jax.ShapeDtypeStruct((1, D), F32), jax.ShapeDtypeStruct((1, 128), F32)],
        compiler_params=_params(("arbitrary",)),
    )(h, target, g)


HALO = 8


def _gelu_parts(x):
    th = jnp.tanh(GELU_K0 * (x + GELU_K1 * (x * x * x)))
    return x * (0.5 * (1.0 + th)), th


def _gelu_grad(x, th):
    return 0.5 * (1.0 + th) + (0.5 * GELU_K0) * x * (1.0 - th * th) * (1.0 + (3.0 * GELU_K1) * (x * x))


def _conv_fwd(b_ref, c_ref, h_ref, ch_ref, hh_ref, w_ref, first):
    tt, wc = c_ref.shape
    c = c_ref[...]
    h = h_ref[...]
    hc = c * h
    prev1 = jnp.where(first, 0.0, ch_ref[HALO - 1:HALO, :] * hh_ref[HALO - 1:HALO, :])
    prev2 = jnp.where(first, 0.0, ch_ref[HALO - 2:HALO - 1, :] * hh_ref[HALO - 2:HALO - 1, :])
    row = lax.broadcasted_iota(jnp.int32, (tt, wc), 0)
    m1 = jnp.where(row == 0, prev1, pltpu.roll(hc, 1, 0))
    m2 = jnp.where(row == 0, prev2, jnp.where(row == 1, prev1, pltpu.roll(hc, 2, 0)))
    conv = w_ref[0:1, :] * m2 + w_ref[1:2, :] * m1 + w_ref[2:3, :] * hc
    return c, h, hc, m1, m2, conv, b_ref[...] * conv


def _tril():
    r = lax.broadcasted_iota(jnp.int32, (HEAD, HEAD), 0)
    s = lax.broadcasted_iota(jnp.int32, (HEAD, HEAD), 1)
    return r >= s


def _spatial_fwd(gvb, sw_ref, sbt_ref, s_scr):
    n_head = sw_ref.shape[0]
    tri = _tril()
    for hd in range(n_head):
        sl = slice(hd * HEAD, (hd + 1) * HEAD)
        wm = jnp.where(tri, sw_ref[hd], 0.0).astype(BF16)
        s_scr[:, sl] = jnp.dot(wm, gvb[:, sl], preferred_element_type=F32) + sbt_ref[:, hd:hd + 1]


def _mixer_specs(n_tiles, wc, row_of, n_grid):
    def grp(g):
        return pl.BlockSpec((HEAD, wc), lambda *ids: (row_of(*ids), g))

    def halo(g):
        return pl.BlockSpec((HALO, wc), lambda *ids: (jnp.maximum(row_of(*ids) * (HEAD // HALO) - 1, 0), g))

    return grp, halo


def _mixer_fwd(proj, conv_w, sw, sbt, g_a, g_b, n_seq, seq):
    T, w5 = proj.shape
    wc = w5 // 5
    n_head = wc // HEAD
    nt = seq // HEAD

    def row_of(s, i, g):
        return s * nt + i

    grp, halo = _mixer_specs(nt, wc, row_of, 3)

    def body(b_ref, c_ref, h_ref, u_ref, v_ref, ch_ref, hh_ref, w_ref, sw_ref, sbt_ref, ga_ref, gb_ref,
             y_ref, s_scr):
        i = pl.program_id(1)
        g = pl.program_id(2)

        @pl.when(g == 0)
        def _():
            ya = _conv_fwd(b_ref, c_ref, h_ref, ch_ref, hh_ref, w_ref, i == 0)[-1]
            inv = lax.rsqrt(jnp.mean(ya * ya, axis=-1, keepdims=True) + EPS)
            y_ref[...] = (ya * inv * ga_ref[...]).astype(BF16)

        @pl.when(g == 1)
        def _():
            gu, _ = _gelu_parts(u_ref[...])
            gv, _ = _gelu_parts(v_ref[...])
            _spatial_fwd(gv.astype(BF16), sw_ref, sbt_ref, s_scr)
            yb = gu * s_scr[...]
            inv = lax.rsqrt(jnp.mean(yb * yb, axis=-1, keepdims=True) + EPS)
            y_ref[...] = (yb * inv * gb_ref[...]).astype(BF16)

    const2 = lambda shape: pl.BlockSpec(shape, lambda s, i, g: (0, 0))
    return pl.pallas_call(
        body, name="mixer_fwd", grid=(n_seq, nt, 2),
        in_specs=[grp(0), grp(1), grp(2), grp(3), grp(4), halo(1), halo(2),
                  const2((CONV_K, wc)), pl.BlockSpec((n_head, HEAD, HEAD), lambda s, i, g: (0, 0, 0)),
                  const2((HEAD, n_head)), const2((1, wc)), const2((1, wc))],
        out_specs=pl.BlockSpec((HEAD, wc), lambda s, i, g: (s * nt + i, g)),
        out_shape=jax.ShapeDtypeStruct((T, 2 * wc), BF16),
        scratch_shapes=[pltpu.VMEM((HEAD, wc), F32)],
        compiler_params=_params(("parallel", "parallel", "arbitrary")),
    )(proj, proj, proj, proj, proj, proj, proj, conv_w, sw, sbt, g_a, g_b)


def _mixer_bwd(proj, dy, conv_w, sw, sbt, g_a, g_b, n_seq, seq):
    T, w5 = proj.shape
    wc = w5 // 5
    n_head = wc // HEAD
    nt = seq // HEAD
    tt = HEAD

    def row_of(s, ir, g):
        return s * nt + (nt - 1 - ir)

    grp, halo = _mixer_specs(nt, wc, row_of, 3)

    def body(b_ref, c_ref, h_ref, u_ref, v_ref, ch_ref, hh_ref, dya_ref, dyb_ref, w_ref, sw_ref, sbt_ref,
             ga_ref, gb_ref, dp_ref, gw_ref, gga_ref, ggb_ref, gsw_ref, gsb_ref,
             carry_scr, stash_scr, s_scr, t_scr, dsum_scr):
        s_id = pl.program_id(0)
        ir = pl.program_id(1)
        g = pl.program_id(2)
        first_tile = jnp.logical_and(s_id == 0, ir == 0)
        last_tile = jnp.logical_and(s_id == n_seq - 1, ir == nt - 1)

        @pl.when(g == 0)
        def _():
            @pl.when(ir == 0)
            def _():
                carry_scr[...] = jnp.zeros_like(carry_scr)

            c, h, hc, m1, m2, conv, ya = _conv_fwd(b_ref, c_ref, h_ref, ch_ref, hh_ref, w_ref, ir == nt - 1)
            inv = lax.rsqrt(jnp.mean(ya * ya, axis=-1, keepdims=True) + EPS)
            dyn = dya_ref[...]
            gd = dyn * ga_ref[...]
            dot = jnp.mean(gd * ya, axis=-1, keepdims=True)
            dya = inv * gd - ya * (inv * inv * inv * dot)
            gg = jnp.sum(dyn * ya * inv, axis=0, keepdims=True)
            dconv = dya * b_ref[...]
            nxt0 = carry_scr[0:1, :]
            nxt1 = carry_scr[1:2, :]
            row = lax.broadcasted_iota(jnp.int32, (tt, wc), 0)
            p1 = jnp.where(row == tt - 1, nxt0, pltpu.roll(dconv, tt - 1, 0))
            p2 = jnp.where(row == tt - 2, nxt0, jnp.where(row == tt - 1, nxt1, pltpu.roll(dconv, tt - 2, 0)))
            dhc = w_ref[2:3, :] * dconv + w_ref[1:2, :] * p1 + w_ref[0:1, :] * p2
            carry_scr[...] = dconv[0:HALO, :]
            dp_ref[...] = (dya * conv).astype(BF16)
            stash_scr[0] = (dhc * h).astype(BF16)
            stash_scr[1] = (dhc * c).astype(BF16)
            gw0 = jnp.sum(dconv * m2, axis=0, keepdims=True)
            gw1 = jnp.sum(dconv * m1, axis=0, keepdims=True)
            gw2 = jnp.sum(dconv * hc, axis=0, keepdims=True)

            @pl.when(first_tile)
            def _():
                gw_ref[0:1, :] = gw0
                gw_ref[1:2, :] = gw1
                gw_ref[2:3, :] = gw2
                gga_ref[...] = gg

            @pl.when(jnp.logical_not(first_tile))
            def _():
                gw_ref[0:1, :] += gw0
                gw_ref[1:2, :] += gw1
                gw_ref[2:3, :] += gw2
                gga_ref[...] += gg

        @pl.when(g == 1)
        def _():
            dp_ref[...] = stash_scr[0]

        @pl.when(g == 2)
        def _():
            dp_ref[...] = stash_scr[1]

        @pl.when(g == 3)
        def _():
            u = u_ref[...]
            v = v_ref[...]
            gu, thu = _gelu_parts(u)
            gv, thv = _gelu_parts(v)
            gvb = gv.astype(BF16)
            _spatial_fwd(gvb, sw_ref, sbt_ref, s_scr)
            sv = s_scr[...]
            yb = gu * sv
            inv = lax.rsqrt(jnp.mean(yb * yb, axis=-1, keepdims=True) + EPS)
            dyn = dyb_ref[...]
            gd = dyn * gb_ref[...]
            dot = jnp.mean(gd * yb, axis=-1, keepdims=True)
            dyb = inv * gd - yb * (inv * inv * inv * dot)
            gg = jnp.sum(dyn * yb * inv, axis=0, keepdims=True)
            ds = dyb * gu
            dsb = ds.astype(BF16)
            tri = _tril()

            @pl.when(first_tile)
            def _():
                ggb_ref[...] = gg
                dsum_scr[...] = ds
                gsw_ref[...] = jnp.zeros_like(gsw_ref)

            @pl.when(jnp.logical_not(first_tile))
            def _():
                ggb_ref[...] += gg
                dsum_scr[...] += ds

            for hd in range(n_head):
                sl = slice(hd * HEAD, (hd + 1) * HEAD)
                wm = jnp.where(tri, sw_ref[hd], 0.0).astype(BF16)
                t_scr[:, sl] = lax.dot_general(wm, dsb[:, sl], TN, preferred_element_type=F32)
                gsw_ref[hd] += lax.dot_general(dsb[:, sl], gvb[:, sl], NT, preferred_element_type=F32)
            dp_ref[...] = (dyb * sv * _gelu_grad(u, thu)).astype(BF16)
            stash_scr[0] = (t_scr[...] * _gelu_grad(v, thv)).astype(BF16)

            @pl.when(last_tile)
            def _():
                for hd in range(n_head):
                    sl = slice(hd * HEAD, (hd + 1) * HEAD)
                    gsw_ref[hd] = jnp.where(tri, gsw_ref[hd], 0.0)
                    gsb_ref[:, hd:hd + 1] = jnp.sum(dsum_scr[:, sl], axis=1, keepdims=True)

        @pl.when(g == 4)
        def _():
            dp_ref[...] = stash_scr[0]

    const2 = lambda shape: pl.BlockSpec(shape, lambda s, i, g: (0, 0))
    const3 = pl.BlockSpec((n_head, HEAD, HEAD), lambda s, i, g: (0, 0, 0))
    dy_spec = lambda col: pl.BlockSpec((tt, wc), lambda s, ir, g: (row_of(s, ir, g), col))
    return pl.pallas_call(
        body, name="mixer_bwd", grid=(n_seq, nt, 5),
        in_specs=[grp(0), grp(1), grp(2), grp(3), grp(4), halo(1), halo(2), dy_spec(0), dy_spec(1),
                  const2((CONV_K, wc)), const3, const2((HEAD, n_head)), const2((1, wc)), const2((1, wc))],
        out_specs=[pl.BlockSpec((tt, wc), lambda s, ir, g: (row_of(s, ir, g), g)),
                   const2((CONV_K, wc)), const2((1, wc)), const2((1, wc)), const3, const2((HEAD, n_head))],
        out_shape=[jax.ShapeDtypeStruct((T, 5 * wc), BF16), jax.ShapeDtypeStruct((CONV_K, wc), F32),
                   jax.ShapeDtypeStruct((1, wc), F32), jax.ShapeDtypeStruct((1, wc), F32),
                   jax.ShapeDtypeStruct((n_head, HEAD, HEAD), F32), jax.ShapeDtypeStruct((HEAD, n_head), F32)],
        scratch_shapes=[pltpu.VMEM((HALO, wc), F32), pltpu.VMEM((2, tt, wc), BF16), pltpu.VMEM((tt, wc), F32),
                        pltpu.VMEM((tt, wc), F32), pltpu.VMEM((tt, wc), F32)],
        compiler_params=_params(("arbitrary", "arbitrary", "arbitrary")),
    )(proj, proj, proj, proj, proj, proj, proj, dy, dy, conv_w, sw, sbt, g_a, g_b)


def _place():
    return lax.axis_index("x"), lax.axis_index("y"), lax.axis_index("c")


def _other_chips(x, y):
    return [(1 - x, y), (x, 1 - y), (1 - x, 1 - y)]


def _all_gather(blk, name):
    def body(x_ref, out_ref, send_sems, recv_sems, local_sem):
        x, y, c = _place()
        me, sibling = (x, y, c), (x, y, 1 - c)
        chips = _other_chips(x, y)

        def slot(px, py, pc):
            return out_ref.at[4 * px + 2 * py + pc]

        def copy(k, block, to, src=None):
            return pltpu.make_async_remote_copy(
                src_ref=slot(*block) if src is None else src, dst_ref=slot(*block),
                send_sem=send_sems.at[k], recv_sem=recv_sems.at[k], device_id=to, device_id_type=MESH)

        mine = pltpu.make_async_copy(x_ref, slot(*me), local_sem)
        mine.start()
        first = [copy(0, me, sibling, src=x_ref)]
        first += [copy(1 + j, me, (*chip, c), src=x_ref) for j, chip in enumerate(chips)]
        for cp in first:
            cp.start()
        passed = [copy(4 + j, (*chip, c), sibling) for j, chip in enumerate(chips)]
        for j, chip in enumerate(chips):
            copy(1 + j, (*chip, c), me).wait_recv()
            passed[j].start()
        copy(0, sibling, me).wait_recv()
        for j, chip in enumerate(chips):
            copy(4 + j, (*chip, 1 - c), me).wait_recv()
        for cp in first + passed:
            cp.wait_send()
        mine.wait()

    return pl.pallas_call(
        body, name=name, in_specs=[pl.BlockSpec(memory_space=HBM)], out_specs=pl.BlockSpec(memory_space=HBM),
        out_shape=jax.ShapeDtypeStruct((N_DEV,) + blk.shape, blk.dtype),
        scratch_shapes=[pltpu.SemaphoreType.DMA((7,)), pltpu.SemaphoreType.DMA((7,)), pltpu.SemaphoreType.DMA],
    )(blk)


def _swap_with_sibling(parts, name):
    _, R, C = parts.shape

    def body(p_ref, got_ref, send_sems, recv_sems):
        x, y, c = _place()
        copies = [pltpu.make_async_remote_copy(
            src_ref=p_ref.at[2 * k + (1 - c)], dst_ref=got_ref.at[k], send_sem=send_sems.at[k],
            recv_sem=recv_sems.at[k], device_id=(x, y, 1 - c), device_id_type=MESH) for k in range(N_CHIP)]
        for cp in copies:
            cp.start()
        for cp in copies:
            cp.wait()

    return pl.pallas_call(
        body, name=name, in_specs=[pl.BlockSpec(memory_space=HBM)], out_specs=pl.BlockSpec(memory_space=HBM),
        out_shape=jax.ShapeDtypeStruct((N_CHIP, R, C), parts.dtype),
        scratch_shapes=[pltpu.SemaphoreType.DMA((N_CHIP,)), pltpu.SemaphoreType.DMA((N_CHIP,))],
    )(parts)


def _add_sibling(parts, got, c_idx, name):
    _, R, C = parts.shape
    tr, tc = _tile(R, 512), _tile(C, 1024)

    def body(c_ref, p_ref, g_ref, o_ref):
        o_ref[...] = (p_ref[...].astype(F32) + g_ref[...].astype(F32)).astype(o_ref.dtype)

    grid_spec = pltpu.PrefetchScalarGridSpec(
        num_scalar_prefetch=1, grid=(N_CHIP, R // tr, C // tc),
        in_specs=[pl.BlockSpec((None, tr, tc), lambda k, i, j, c_ref: (2 * k + c_ref[0], i, j)),
                  pl.BlockSpec((None, tr, tc), lambda k, i, j, c_ref: (k, i, j))],
        out_specs=pl.BlockSpec((None, tr, tc), lambda k, i, j, c_ref: (k, i, j)))
    return pl.pallas_call(
        body, name=name, grid_spec=grid_spec, out_shape=jax.ShapeDtypeStruct((N_CHIP, R, C), parts.dtype),
        compiler_params=_params(("parallel", "parallel", "parallel")),
    )(c_idx, parts, got)


def _scatter_to_chips(sums, name):
    _, R, C = sums.shape

    def body(q_ref, got_ref, send_sems, recv_sems, local_sem):
        x, y, c = _place()
        my_chip = 2 * x + y
        mine = pltpu.make_async_copy(q_ref.at[my_chip], got_ref.at[my_chip], local_sem)
        mine.start()
        copies = [pltpu.make_async_remote_copy(
            src_ref=q_ref.at[2 * px + py], dst_ref=got_ref.at[my_chip], send_sem=send_sems.at[j],
            recv_sem=recv_sems.at[j], device_id=(px, py, c), device_id_type=MESH)
            for j, (px, py) in enumerate(_other_chips(x, y))]
        for cp in copies:
            cp.start()
        for cp in copies:
            cp.wait()
        mine.wait()

    return pl.pallas_call(
        body, name=name, in_specs=[pl.BlockSpec(memory_space=HBM)], out_specs=pl.BlockSpec(memory_space=HBM),
        out_shape=jax.ShapeDtypeStruct((N_CHIP, R, C), sums.dtype),
        scratch_shapes=[pltpu.SemaphoreType.DMA((3,)), pltpu.SemaphoreType.DMA((3,)), pltpu.SemaphoreType.DMA],
    )(sums)


def _adamw_math(w, g, m, v):
    m = ADAM_B1 * m + (1.0 - ADAM_B1) * g
    v = ADAM_B2 * v + (1.0 - ADAM_B2) * (g * g)
    m_hat = m / (1.0 - ADAM_B1 ** ADAM_STEP)
    v_hat = v / (1.0 - ADAM_B2 ** ADAM_STEP)
    delta = -ADAM_LR * (m_hat / (jnp.sqrt(v_hat) + ADAM_EPS) + ADAM_WD * w)
    return delta, m, v


def _sum_adamw(parts, w, m, v, name):
    n_parts, R, C = parts.shape
    tr, tc = _tile(R, 256), _tile(C, 1024)

    def body(p_ref, w_ref, m_ref, v_ref, g_out, d_out, m_out, v_out):
        g = p_ref[0].astype(F32)
        for k in range(1, n_parts):
            g = g + p_ref[k].astype(F32)
        delta, mn, vn = _adamw_math(w_ref[...], g, m_ref[...], v_ref[...])
        g_out[...] = g
        d_out[...] = delta
        m_out[...] = mn
        v_out[...] = vn

    blk = pl.BlockSpec((tr, tc), lambda i, j: (i, j))
    shp = jax.ShapeDtypeStruct((R, C), F32)
    return pl.pallas_call(
        body, name=name, grid=(R // tr, C // tc),
        in_specs=[pl.BlockSpec((n_parts, tr, tc), lambda i, j: (0, i, j)), blk, blk, blk],
        out_specs=[blk, blk, blk, blk], out_shape=[shp, shp, shp, shp],
        compiler_params=_params(("parallel", "parallel")),
    )(parts, w, m, v)


def _reduce_scatter(parts, c_idx, tag):
    got = _swap_with_sibling(parts, "rs_sibling_" + tag)
    sums = _add_sibling(parts, got, c_idx, "rs_add_" + tag)
    return _scatter_to_chips(sums, "rs_chips_" + tag)


def _rows128(a):
    return a.reshape(-1, 128)


def kernel(x, mix_norm_g, w_in, conv_w, spatial_w, spatial_b, conv_out_norm_g, gmlp_out_norm_g, w_out, mlp_norm_g, w_up, w_down, final_norm_g, loss_target, m_mix_norm_g, m_w_in, m_conv_w, m_spatial_w, m_spatial_b, m_conv_out_norm_g, m_gmlp_out_norm_g, m_w_out, m_mlp_norm_g, m_w_up, m_w_down, m_final_norm_g, v_mix_norm_g, v_w_in, v_conv_w, v_spatial_w, v_spatial_b, v_conv_out_norm_g, v_gmlp_out_norm_g, v_w_out, v_mlp_norm_g, v_w_up, v_w_down, v_final_norm_g):
    n_seq, seq, D = x.shape
    T = n_seq * seq
    n_in = w_in.shape[2]
    n_out = w_out.shape[1]
    n_up = w_up.shape[2]
    wc = conv_w.shape[2] * N_DEV
    n_head = wc // HEAD
    FF = n_up * N_DEV
    assert N_DEV * n_in == 5 * wc and seq % HEAD == 0 and D == 2 * wc

    c_idx = lax.axis_index("c").astype(jnp.int32).reshape(1)
    my_dev = 4 * lax.axis_index("x") + 2 * lax.axis_index("y") + lax.axis_index("c")

    xf = x.reshape(T, D)
    tgt = loss_target.reshape(T, D)

    win_g = _all_gather(w_in[0].astype(BF16), "ag_w_in")
    wout_g = _all_gather(w_out[0].astype(BF16), "ag_w_out").reshape(D, D)
    wup_g = _all_gather(w_up[0].astype(BF16), "ag_w_up")
    wdown_g = _all_gather(w_down[0].astype(BF16), "ag_w_down").reshape(FF, D)
    cw_pad = jnp.pad(conv_w[0], ((0, HALO - CONV_K), (0, 0)))
    cw_g = _all_gather(cw_pad, "ag_conv_w")
    conv_full = jnp.transpose(cw_g[:, :CONV_K, :], (1, 0, 2)).reshape(CONV_K, wc)

    sw = spatial_w[0]
    sbt = spatial_b[0].T
    g_mix, g_a, g_b, g_mlp = mix_norm_g, conv_out_norm_g, gmlp_out_norm_g, mlp_norm_g
    g_fin = final_norm_g.reshape(1, D)

    bm = _tile(T, 512)
    bn = _tile(D, 1024)

    xn = _rms_fwd(xf, g_mix, "norm_mix")
    proj = _matmul(
        "proj", (N_DEV, T // bm, 1), NN, [xn, win_g],
        [pl.BlockSpec((bm, D), lambda n, m, k: (m, 0)), pl.BlockSpec((None, D, n_in), lambda n, m, k: (n, 0, 0))],
        [jax.ShapeDtypeStruct((T, N_DEV * n_in), F32)], [pl.BlockSpec((bm, n_in), lambda n, m, k: (m, n))],
        None, _ident)[0]
    y = _mixer_fwd(proj, conv_full, sw, sbt, g_a, g_b, n_seq, seq)
    h1 = _matmul(
        "out_proj", (D // bn, T // bm, 1), NN, [y, wout_g, xf],
        [pl.BlockSpec((bm, D), lambda n, m, k: (m, 0)), pl.BlockSpec((D, bn), lambda n, m, k: (0, n)),
         pl.BlockSpec((bm, bn), lambda n, m, k: (m, n))],
        [jax.ShapeDtypeStruct((T, D), F32)], [pl.BlockSpec((bm, bn), lambda n, m, k: (m, n))],
        None, lambda acc, res: (res + acc,))[0]
    xn2 = _rms_fwd(h1, g_mlp, "norm_mlp")
    bu = _tile(n_up, 1024)
    per = n_up // bu

    def up_epilogue(acc):
        r = jnp.maximum(acc, 0.0)
        return acc, r * r

    up, act = _matmul(
        "up_proj", (FF // bu, T // bm, 1), NN, [xn2, wup_g],
        [pl.BlockSpec((bm, D), lambda n, m, k: (m, 0)),
         pl.BlockSpec((None, D, bu), lambda n, m, k: (n // per, 0, n % per))],
        [jax.ShapeDtypeStruct((T, FF), BF16)] * 2, [pl.BlockSpec((bm, bu), lambda n, m, k: (m, n))] * 2,
        None, up_epilogue)
    bk = _tile(FF, 2048)
    h2 = _matmul(
        "down_proj", (D // bn, T // bm, FF // bk), NN, [act, wdown_g, h1],
        [pl.BlockSpec((bm, bk), lambda n, m, k: (m, k)), pl.BlockSpec((bk, bn), lambda n, m, k: (k, n)),
         pl.BlockSpec((bm, bn), lambda n, m, k: (m, n))],
        [jax.ShapeDtypeStruct((T, D), F32)], [pl.BlockSpec((bm, bn), lambda n, m, k: (m, n))],
        (bm, bn), lambda acc, res: (res + acc,))[0]

    dh2, dh2b, gg_fin, loss_row = _loss_head(h2, tgt, g_fin)
    loss = lax.psum(loss_row[0, 0], ("x", "y", "c"))

    bt = _tile(T, 2048)
    bw = _tile(D, 1024)
    dup = _matmul(
        "d_act", (FF // bu, T // bm, 1), NT, [dh2b, wdown_g, up],
        [pl.BlockSpec((bm, D), lambda n, m, k: (m, 0)), pl.BlockSpec((bu, D), lambda n, m, k: (n, 0)),
         pl.BlockSpec((bm, bu), lambda n, m, k: (m, n))],
        [jax.ShapeDtypeStruct((T, FF), BF16)], [pl.BlockSpec((bm, bu), lambda n, m, k: (m, n))],
        None, lambda acc, u: (acc * (2.0 * jnp.maximum(u.astype(F32), 0.0)),))[0]
    bf = _tile(FF, 1024)
    gp_down = _matmul(
        "gw_down", (FF // bf, D // bw, T // bt), TN, [act, dh2b],
        [pl.BlockSpec((bt, bf), lambda m, n, k: (k, m)), pl.BlockSpec((bt, bw), lambda m, n, k: (k, n))],
        [jax.ShapeDtypeStruct((FF, D), BF16)], [pl.BlockSpec((bf, bw), lambda m, n, k: (m, n))],
        (bf, bw), _ident)[0]
    dxn2 = _matmul(
        "d_xn2", (D // bn, T // bm, N_DEV), NT, [dup, wup_g],
        [pl.BlockSpec((bm, n_up), lambda n, m, k: (m, k)), pl.BlockSpec((None, bn, n_up), lambda n, m, k: (k, n, 0))],
        [jax.ShapeDtypeStruct((T, D), F32)], [pl.BlockSpec((bm, bn), lambda n, m, k: (m, n))],
        (bm, bn), _ident)[0]
    gp_up = _matmul(
        "gw_up", (D // bw, FF // bu, T // bt), TN, [xn2, dup],
        [pl.BlockSpec((bt, bw), lambda m, n, k: (k, m)), pl.BlockSpec((bt, bu), lambda m, n, k: (k, n))],
        [jax.ShapeDtypeStruct((N_DEV, D, n_up), BF16)],
        [pl.BlockSpec((None, bw, bu), lambda m, n, k: (n // per, m, n % per))],
        (bw, bu), _ident)[0]
    dh1, dh1b, gg_mlp = _rms_bwd(dxn2, h1, g_mlp, dh2, "norm_mlp_bwd", True)

    dy = _matmul(
        "d_y", (D // bn, T // bm, 1), NT, [dh1b, wout_g],
        [pl.BlockSpec((bm, D), lambda n, m, k: (m, 0)), pl.BlockSpec((bn, D), lambda n, m, k: (n, 0))],
        [jax.ShapeDtypeStruct((T, D), F32)], [pl.BlockSpec((bm, bn), lambda n, m, k: (m, n))],
        None, _ident)[0]
    gp_out = _matmul(
        "gw_out", (D // bw, D // bn, T // bt), TN, [y, dh1b],
        [pl.BlockSpec((bt, bw), lambda m, n, k: (k, m)), pl.BlockSpec((bt, bn), lambda m, n, k: (k, n))],
        [jax.ShapeDtypeStruct((D, D), BF16)], [pl.BlockSpec((bw, bn), lambda m, n, k: (m, n))],
        (bw, bn), _ident)[0]
    dproj, gl_conv, gl_a, gl_b, gl_sw, gl_sbt = _mixer_bwd(proj, dy, conv_full, sw, sbt, g_a, g_b, n_seq, seq)
    dxn = _matmul(
        "d_xn", (D // bn, T // bm, N_DEV), NT, [dproj, win_g],
        [pl.BlockSpec((bm, n_in), lambda n, m, k: (m, k)), pl.BlockSpec((None, bn, n_in), lambda n, m, k: (k, n, 0))],
        [jax.ShapeDtypeStruct((T, D), F32)], [pl.BlockSpec((bm, bn), lambda n, m, k: (m, n))],
        (bm, bn), _ident)[0]
    gp_in = _matmul(
        "gw_in", (D // bw, N_DEV, T // bt), TN, [xn, dproj],
        [pl.BlockSpec((bt, bw), lambda m, n, k: (k, m)), pl.BlockSpec((bt, n_in), lambda m, n, k: (k, n))],
        [jax.ShapeDtypeStruct((N_DEV, D, n_in), BF16)], [pl.BlockSpec((None, bw, n_in), lambda m, n, k: (n, m, 0))],
        (bw, n_in), _ident)[0]
    grad_x, gg_mix = _rms_bwd(dxn, xf, g_mix, dh1, "norm_mix_bwd", False)

    outs = {}
    for tag, gp, w, m, v in (("w_in", gp_in, w_in, m_w_in, v_w_in),
                             ("w_out", gp_out.reshape(N_DEV, n_out, D), w_out, m_w_out, v_w_out),
                             ("w_up", gp_up, w_up, m_w_up, v_w_up),
                             ("w_down", gp_down.reshape(N_DEV, n_up, D), w_down, m_w_down, v_w_down)):
        four = _reduce_scatter(gp, c_idx, tag)
        outs[tag] = [a[None] for a in _sum_adamw(four, w[0], m[0], v[0], "adamw_" + tag)]

    small = [("mix_norm_g", gg_mix, mix_norm_g, m_mix_norm_g, v_mix_norm_g),
             ("conv_w", gl_conv, None, None, None),
             ("spatial_w", gl_sw, spatial_w, m_spatial_w, v_spatial_w),
             ("spatial_b", gl_sbt.T, spatial_b, m_spatial_b, v_spatial_b),
             ("conv_out_norm_g", gl_a, conv_out_norm_g, m_conv_out_norm_g, v_conv_out_norm_g),
             ("gmlp_out_norm_g", gl_b, gmlp_out_norm_g, m_gmlp_out_norm_g, v_gmlp_out_norm_g),
             ("mlp_norm_g", gg_mlp, mlp_norm_g, m_mlp_norm_g, v_mlp_norm_g),
             ("final_norm_g", gg_fin, final_norm_g, m_final_norm_g, v_final_norm_g)]
    packed_g = jnp.concatenate([_rows128(g) for _, g, _, _, _ in small], axis=0)
    zeros_cw = jnp.zeros((CONV_K * wc // 128, 128), F32)
    pack = lambda idx: jnp.concatenate(
        [zeros_cw if item[2] is None else _rows128(item[idx]) for item in small], axis=0)
    all_g = _all_gather(packed_g, "ag_small_grads")
    sg, sd, sm, sv = _sum_adamw(all_g, pack(2), pack(3), pack(4), "adamw_small")
    row = 0
    for name, g, w, _, _ in small:
        n_rows = g.size // 128
        if w is not None:
            outs[name] = [a[row:row + n_rows].reshape(w.shape) for a in (sg, sd, sm, sv)]
        else:
            conv_grad_full = sg[row:row + n_rows].reshape(CONV_K, wc)
        row += n_rows
    cpd = wc // N_DEV
    conv_grad = lax.dynamic_slice(conv_grad_full, (0, my_dev * cpd), (CONV_K, cpd))
    pad8 = lambda a: jnp.pad(a, ((0, HALO - CONV_K), (0, 0)))
    outs["conv_w"] = [a[:CONV_K][None] for a in _sum_adamw(
        pad8(conv_grad)[None], pad8(conv_w[0]), pad8(m_conv_w[0]), pad8(v_conv_w[0]), "adamw_conv_w")]

    order = ["mix_norm_g", "w_in", "conv_w", "spatial_w", "spatial_b", "conv_out_norm_g", "gmlp_out_norm_g",
             "w_out", "mlp_norm_g", "w_up", "w_down", "final_norm_g"]
    result = [loss, grad_x.reshape(n_seq, seq, D)]
    for k in range(4):
        result += [outs[n][k] for n in order]
    return tuple(result)
```

```python
import functools
import math

import jax
import jax.numpy as jnp
from jax import lax
from jax.experimental import pallas as pl
from jax.experimental.pallas import tpu as pltpu
from jax.experimental.pallas import tpu_sc as plsc

F32 = jnp.float32
BF16 = jnp.bfloat16
MESH = pl.DeviceIdType.MESH
HBM = pltpu.HBM

EPS = 1e-5
HEAD = 128
CONV_K = 3
N_DEV = 8
N_CHIP = 4
VMEM_LIMIT_BYTES = 56 * 1024 * 1024

ADAM_LR = 0.001
ADAM_B1 = 0.9
ADAM_B2 = 0.999
ADAM_EPS = 1e-08
ADAM_WD = 0.01
ADAM_STEP = 10

GELU_K0 = math.sqrt(2.0 / math.pi)
GELU_K1 = 0.044715

NN = (((1,), (0,)), ((), ()))
NT = (((1,), (1,)), ((), ()))
TN = (((0,), (0,)), ((), ()))


def _params(semantics):
    return pltpu.CompilerParams(dimension_semantics=semantics, vmem_limit_bytes=VMEM_LIMIT_BYTES)


def _tile(dim, want):
    if dim <= want:
        return dim
    for t in range(want - want % 8, 0, -8):
        if dim % t == 0:
            return t
    raise ValueError((dim, want))


def _matmul(name, grid, dims, operands, in_specs, out_shapes, out_specs, acc_shape, epilogue):
    n_in = len(operands)
    n_out = len(out_shapes)
    nk = grid[2]

    def body(*refs):
        a_ref, b_ref = refs[0], refs[1]
        extra = refs[2:n_in]
        outs = refs[n_in:n_in + n_out]
        part = lax.dot_general(a_ref[...], b_ref[...], dims, preferred_element_type=F32)

        def finish(acc):
            res = epilogue(acc, *[e[...] for e in extra])
            for o, r in zip(outs, res):
                o[...] = r.astype(o.dtype)

        if nk == 1:
            finish(part)
        else:
            acc_ref = refs[n_in + n_out]
            k = pl.program_id(2)

            @pl.when(k == 0)
            def _():
                acc_ref[...] = part

            @pl.when(k > 0)
            def _():
                acc_ref[...] += part

            @pl.when(k == nk - 1)
            def _():
                finish(acc_ref[...])

    scratch = [pltpu.VMEM(acc_shape, F32)] if nk > 1 else []
    return pl.pallas_call(
        body, name=name, grid=grid, in_specs=in_specs, out_specs=out_specs, out_shape=out_shapes,
        scratch_shapes=scratch, compiler_params=_params(("parallel", "parallel", "arbitrary")),
    )(*operands)


def _ident(acc):
    return (acc,)


ROW_TILE = 256


def _rms_fwd(x, g, name):
    T, D = x.shape
    tr = _tile(T, ROW_TILE)

    def body(x_ref, g_ref, o_ref):
        xv = x_ref[...]
        inv = lax.rsqrt(jnp.mean(xv * xv, axis=-1, keepdims=True) + EPS)
        o_ref[...] = (xv * inv * g_ref[...]).astype(o_ref.dtype)

    return pl.pallas_call(
        body, name=name, grid=(T // tr,),
        in_specs=[pl.BlockSpec((tr, D), lambda i: (i, 0)), pl.BlockSpec((1, D), lambda i: (0, 0))],
        out_specs=pl.BlockSpec((tr, D), lambda i: (i, 0)),
        out_shape=jax.ShapeDtypeStruct((T, D), BF16),
        compiler_params=_params(("parallel",)),
    )(x, g)


def _rms_bwd(dy, x, g, dres, name, want_bf16):
    T, D = x.shape
    tr = _tile(T, ROW_TILE)

    def body(dy_ref, x_ref, g_ref, dres_ref, *outs):
        dx_ref, gg_ref = outs[0], outs[-1]
        i = pl.program_id(0)
        xv = x_ref[...]
        dyv = dy_ref[...]
        inv = lax.rsqrt(jnp.mean(xv * xv, axis=-1, keepdims=True) + EPS)
        gd = dyv * g_ref[...]
        dot = jnp.mean(gd * xv, axis=-1, keepdims=True)
        dx = dres_ref[...] + (inv * gd - xv * (inv * inv * inv * dot))
        dx_ref[...] = dx
        if want_bf16:
            outs[1][...] = dx.astype(BF16)
        part = jnp.sum(dyv * xv * inv, axis=0, keepdims=True)

        @pl.when(i == 0)
        def _():
            gg_ref[...] = part

        @pl.when(i > 0)
        def _():
            gg_ref[...] += part

    row = pl.BlockSpec((tr, D), lambda i: (i, 0))
    vec = pl.BlockSpec((1, D), lambda i: (0, 0))
    out_shape = [jax.ShapeDtypeStruct((T, D), F32)]
    out_specs = [row]
    if want_bf16:
        out_shape.append(jax.ShapeDtypeStruct((T, D), BF16))
        out_specs.append(row)
    out_shape.append(jax.ShapeDtypeStruct((1, D), F32))
    out_specs.append(vec)
    return pl.pallas_call(
        body, name=name, grid=(T // tr,), in_specs=[row, row, vec, row],
        out_specs=out_specs, out_shape=out_shape, compiler_params=_params(("arbitrary",)),
    )(dy, x, g, dres)


def _loss_head(h, target, g):
    T, D = h.shape
    tr = _tile(T, ROW_TILE)

    def body(h_ref, t_ref, g_ref, dh_ref, dhb_ref, gg_ref, loss_ref):
        i = pl.program_id(0)
        hv = h_ref[...]
        gv = g_ref[...]
        inv = lax.rsqrt(jnp.mean(hv * hv, axis=-1, keepdims=True) + EPS)
        diff = hv * inv * gv - t_ref[...]
        lpart = 0.5 * jnp.sum(jnp.mean(diff * diff, axis=-1, keepdims=True), axis=0, keepdims=True)
        dout = diff * (1.0 / D)
        gd = dout * gv
        dot = jnp.mean(gd * hv, axis=-1, keepdims=True)
        dh = inv * gd - hv * (inv * inv * inv * dot)
        dh_ref[...] = dh
        dhb_ref[...] = dh.astype(BF16)
        part = jnp.sum(dout * hv * inv, axis=0, keepdims=True)
        lrow = jnp.broadcast_to(lpart, (1, 128))

        @pl.when(i == 0)
        def _():
            gg_ref[...] = part
            loss_ref[...] = lrow

        @pl.when(i > 0)
        def _():
            gg_ref[...] += part
            loss_ref[...] += lrow

    row = pl.BlockSpec((tr, D), lambda i: (i, 0))
    vec = pl.BlockSpec((1, D), lambda i: (0, 0))
    return pl.pallas_call(
        body, name="loss_head", grid=(T // tr,), in_specs=[row, row, vec],
        out_specs=[row, row, vec, pl.BlockSpec((1, 128), lambda i: (0, 0))],
        out_shape=[jax.ShapeDtypeStruct((T, D), F32), jax.ShapeDtypeStruct((T, D), BF16),
                   jax.ShapeDtypeStruct((1, D), F32), jax.ShapeDtypeStruct((1, 128), F32)],
        compiler_params=_params(("arbitrary",)),
    )(h, target, g)


HALO = 8


def _gelu_parts(x):
    th = jnp.tanh(GELU_K0 * (x + GELU_K1 * (x * x * x)))
    return x * (0.5 * (1.0 + th)), th


def _gelu_grad(x, th):
    return 0.5 * (1.0 + th) + (0.5 * GELU_K0) * x * (1.0 - th * th) * (1.0 + (3.0 * GELU_K1) * (x * x))


def _conv_fwd(b_ref, c_ref, h_ref, ch_ref, hh_ref, w_ref, first):
    tt, wc = c_ref.shape
    c = c_ref[...]
    h = h_ref[...]
    hc = c * h
    prev1 = jnp.where(first, 0.0, ch_ref[HALO - 1:HALO, :] * hh_ref[HALO - 1:HALO, :])
    prev2 = jnp.where(first, 0.0, ch_ref[HALO - 2:HALO - 1, :] * hh_ref[HALO - 2:HALO - 1, :])
    row = lax.broadcasted_iota(jnp.int32, (tt, wc), 0)
    m1 = jnp.where(row == 0, prev1, pltpu.roll(hc, 1, 0))
    m2 = jnp.where(row == 0, prev2, jnp.where(row == 1, prev1, pltpu.roll(hc, 2, 0)))
    conv = w_ref[0:1, :] * m2 + w_ref[1:2, :] * m1 + w_ref[2:3, :] * hc
    return c, h, hc, m1, m2, conv, b_ref[...] * conv


def _tril():
    r = lax.broadcasted_iota(jnp.int32, (HEAD, HEAD), 0)
    s = lax.broadcasted_iota(jnp.int32, (HEAD, HEAD), 1)
    return r >= s


def _spatial_fwd(gvb, sw_ref, sbt_ref, s_scr):
    n_head = sw_ref.shape[0]
    tri = _tril()
    for hd in range(n_head):
        sl = slice(hd * HEAD, (hd + 1) * HEAD)
        wm = jnp.where(tri, sw_ref[hd], 0.0).astype(BF16)
        s_scr[:, sl] = jnp.dot(wm, gvb[:, sl], preferred_element_type=F32) + sbt_ref[:, hd:hd + 1]


def _mixer_specs(n_tiles, wc, row_of, n_grid):
    def grp(g):
        return pl.BlockSpec((HEAD, wc), lambda *ids: (row_of(*ids), g))

    def halo(g):
        return pl.BlockSpec((HALO, wc), lambda *ids: (jnp.maximum(row_of(*ids) * (HEAD // HALO) - 1, 0), g))

    return grp, halo


def _mixer_fwd(proj, conv_w, sw, sbt, g_a, g_b, n_seq, seq):
    T, w5 = proj.shape
    wc = w5 // 5
    n_head = wc // HEAD
    nt = seq // HEAD

    def row_of(s, i, g):
        return s * nt + i

    grp, halo = _mixer_specs(nt, wc, row_of, 3)

    def body(b_ref, c_ref, h_ref, u_ref, v_ref, ch_ref, hh_ref, w_ref, sw_ref, sbt_ref, ga_ref, gb_ref,
             y_ref, s_scr):
        i = pl.program_id(1)
        g = pl.program_id(2)

        @pl.when(g == 0)
        def _():
            ya = _conv_fwd(b_ref, c_ref, h_ref, ch_ref, hh_ref, w_ref, i == 0)[-1]
            inv = lax.rsqrt(jnp.mean(ya * ya, axis=-1, keepdims=True) + EPS)
            y_ref[...] = (ya * inv * ga_ref[...]).astype(BF16)

        @pl.when(g == 1)
        def _():
            gu, _ = _gelu_parts(u_ref[...])
            gv, _ = _gelu_parts(v_ref[...])
            _spatial_fwd(gv.astype(BF16), sw_ref, sbt_ref, s_scr)
            yb = gu * s_scr[...]
            inv = lax.rsqrt(jnp.mean(yb * yb, axis=-1, keepdims=True) + EPS)
            y_ref[...] = (yb * inv * gb_ref[...]).astype(BF16)

    const2 = lambda shape: pl.BlockSpec(shape, lambda s, i, g: (0, 0))
    return pl.pallas_call(
        body, name="mixer_fwd", grid=(n_seq, nt, 2),
        in_specs=[grp(0), grp(1), grp(2), grp(3), grp(4), halo(1), halo(2),
                  const2((CONV_K, wc)), pl.BlockSpec((n_head, HEAD, HEAD), lambda s, i, g: (0, 0, 0)),
                  const2((HEAD, n_head)), const2((1, wc)), const2((1, wc))],
        out_specs=pl.BlockSpec((HEAD, wc), lambda s, i, g: (s * nt + i, g)),
        out_shape=jax.ShapeDtypeStruct((T, 2 * wc), BF16),
        scratch_shapes=[pltpu.VMEM((HEAD, wc), F32)],
        compiler_params=_params(("parallel", "parallel", "arbitrary")),
    )(proj, proj, proj, proj, proj, proj, proj, conv_w, sw, sbt, g_a, g_b)


def _mixer_bwd(proj, dy, conv_w, sw, sbt, g_a, g_b, n_seq, seq):
    T, w5 = proj.shape
    wc = w5 // 5
    n_head = wc // HEAD
    nt = seq // HEAD
    tt = HEAD

    def row_of(s, ir, g):
        return s * nt + (nt - 1 - ir)

    grp, halo = _mixer_specs(nt, wc, row_of, 3)

    def body(b_ref, c_ref, h_ref, u_ref, v_ref, ch_ref, hh_ref, dya_ref, dyb_ref, w_ref, sw_ref, sbt_ref,
             ga_ref, gb_ref, dp_ref, gw_ref, gga_ref, ggb_ref, gsw_ref, gsb_ref,
             carry_scr, stash_scr, s_scr, t_scr, dsum_scr):
        s_id = pl.program_id(0)
        ir = pl.program_id(1)
        g = pl.program_id(2)
        first_tile = jnp.logical_and(s_id == 0, ir == 0)
        last_tile = jnp.logical_and(s_id == n_seq - 1, ir == nt - 1)

        @pl.when(g == 0)
        def _():
            @pl.when(ir == 0)
            def _():
                carry_scr[...] = jnp.zeros_like(carry_scr)

            c, h, hc, m1, m2, conv, ya = _conv_fwd(b_ref, c_ref, h_ref, ch_ref, hh_ref, w_ref, ir == nt - 1)
            inv = lax.rsqrt(jnp.mean(ya * ya, axis=-1, keepdims=True) + EPS)
            dyn = dya_ref[...]
            gd = dyn * ga_ref[...]
            dot = jnp.mean(gd * ya, axis=-1, keepdims=True)
            dya = inv * gd - ya * (inv * inv * inv * dot)
            gg = jnp.sum(dyn * ya * inv, axis=0, keepdims=True)
            dconv = dya * b_ref[...]
            nxt0 = carry_scr[0:1, :]
            nxt1 = carry_scr[1:2, :]
            row = lax.broadcasted_iota(jnp.int32, (tt, wc), 0)
            p1 = jnp.where(row == tt - 1, nxt0, pltpu.roll(dconv, tt - 1, 0))
            p2 = jnp.where(row == tt - 2, nxt0, jnp.where(row == tt - 1, nxt1, pltpu.roll(dconv, tt - 2, 0)))
            dhc = w_ref[2:3, :] * dconv + w_ref[1:2, :] * p1 + w_ref[0:1, :] * p2
            carry_scr[...] = dconv[0:HALO, :]
            dp_ref[...] = (dya * conv).astype(BF16)
            stash_scr[0] = (dhc * h).astype(BF16)
            stash_scr[1] = (dhc * c).astype(BF16)
            gw0 = jnp.sum(dconv * m2, axis=0, keepdims=True)
            gw1 = jnp.sum(dconv * m1, axis=0, keepdims=True)
            gw2 = jnp.sum(dconv * hc, axis=0, keepdims=True)

            @pl.when(first_tile)
            def _():
                gw_ref[0:1, :] = gw0
                gw_ref[1:2, :] = gw1
                gw_ref[2:3, :] = gw2
                gga_ref[...] = gg

            @pl.when(jnp.logical_not(first_tile))
            def _():
                gw_ref[0:1, :] += gw0
                gw_ref[1:2, :] += gw1
                gw_ref[2:3, :] += gw2
                gga_ref[...] += gg

        @pl.when(g == 1)
        def _():
            dp_ref[...] = stash_scr[0]

        @pl.when(g == 2)
        def _():
            dp_ref[...] = stash_scr[1]

        @pl.when(g == 3)
        def _():
            u = u_ref[...]
            v = v_ref[...]
            gu, thu = _gelu_parts(u)
            gv, thv = _gelu_parts(v)
            gvb = gv.astype(BF16)
            _spatial_fwd(gvb, sw_ref, sbt_ref, s_scr)
            sv = s_scr[...]
            yb = gu * sv
            inv = lax.rsqrt(jnp.mean(yb * yb, axis=-1, keepdims=True) + EPS)
            dyn = dyb_ref[...]
            gd = dyn * gb_ref[...]
            dot = jnp.mean(gd * yb, axis=-1, keepdims=True)
            dyb = inv * gd - yb * (inv * inv * inv * dot)
            gg = jnp.sum(dyn * yb * inv, axis=0, keepdims=True)
            ds = dyb * gu
            dsb = ds.astype(BF16)
            tri = _tril()

            @pl.when(first_tile)
            def _():
                ggb_ref[...] = gg
                dsum_scr[...] = ds
                gsw_ref[...] = jnp.zeros_like(gsw_ref)

            @pl.when(jnp.logical_not(first_tile))
            def _():
                ggb_ref[...] += gg
                dsum_scr[...] += ds

            for hd in range(n_head):
                sl = slice(hd * HEAD, (hd + 1) * HEAD)
                wm = jnp.where(tri, sw_ref[hd], 0.0).astype(BF16)
                t_scr[:, sl] = lax.dot_general(wm, dsb[:, sl], TN, preferred_element_type=F32)
                gsw_ref[hd] += lax.dot_general(dsb[:, sl], gvb[:, sl], NT, preferred_element_type=F32)
            dp_ref[...] = (dyb * sv * _gelu_grad(u, thu)).astype(BF16)
            stash_scr[0] = (t_scr[...] * _gelu_grad(v, thv)).astype(BF16)

            @pl.when(last_tile)
            def _():
                for hd in range(n_head):
                    sl = slice(hd * HEAD, (hd + 1) * HEAD)
                    gsw_ref[hd] = jnp.where(tri, gsw_ref[hd], 0.0)
                    gsb_ref[:, hd:hd + 1] = jnp.sum(dsum_scr[:, sl], axis=1, keepdims=True)

        @pl.when(g == 4)
        def _():
            dp_ref[...] = stash_scr[0]

    const2 = lambda shape: pl.BlockSpec(shape, lambda s, i, g: (0, 0))
    const3 = pl.BlockSpec((n_head, HEAD, HEAD), lambda s, i, g: (0, 0, 0))
    dy_spec = lambda col: pl.BlockSpec((tt, wc), lambda s, ir, g: (row_of(s, ir, g), col))
    return pl.pallas_call(
        body, name="mixer_bwd", grid=(n_seq, nt, 5),
        in_specs=[grp(0), grp(1), grp(2), grp(3), grp(4), halo(1), halo(2), dy_spec(0), dy_spec(1),
                  const2((CONV_K, wc)), const3, const2((HEAD, n_head)), const2((1, wc)), const2((1, wc))],
        out_specs=[pl.BlockSpec((tt, wc), lambda s, ir, g: (row_of(s, ir, g), g)),
                   const2((CONV_K, wc)), const2((1, wc)), const2((1, wc)), const3, const2((HEAD, n_head))],
        out_shape=[jax.ShapeDtypeStruct((T, 5 * wc), BF16), jax.ShapeDtypeStruct((CONV_K, wc), F32),
                   jax.ShapeDtypeStruct((1, wc), F32), jax.ShapeDtypeStruct((1, wc), F32),
                   jax.ShapeDtypeStruct((n_head, HEAD, HEAD), F32), jax.ShapeDtypeStruct((HEAD, n_head), F32)],
        scratch_shapes=[pltpu.VMEM((HALO, wc), F32), pltpu.VMEM((2, tt, wc), BF16), pltpu.VMEM((tt, wc), F32),
                        pltpu.VMEM((tt, wc), F32), pltpu.VMEM((tt, wc), F32)],
        compiler_params=_params(("arbitrary", "arbitrary", "arbitrary")),
    )(proj, proj, proj, proj, proj, proj, proj, dy, dy, conv_w, sw, sbt, g_a, g_b)


def _place():
    return lax.axis_index("x"), lax.axis_index("y"), lax.axis_index("c")


def _other_chips(x, y):
    return [(1 - x, y), (x, 1 - y), (1 - x, 1 - y)]


def _all_gather(blk, name):
    def body(x_ref, out_ref, send_sems, recv_sems, local_sem):
        x, y, c = _place()
        me, sibling = (x, y, c), (x, y, 1 - c)
        chips = _other_chips(x, y)

        def slot(px, py, pc):
            return out_ref.at[4 * px + 2 * py + pc]

        def copy(k, block, to, src=None):
            return pltpu.make_async_remote_copy(
                src_ref=slot(*block) if src is None else src, dst_ref=slot(*block),
                send_sem=send_sems.at[k], recv_sem=recv_sems.at[k], device_id=to, device_id_type=MESH)

        mine = pltpu.make_async_copy(x_ref, slot(*me), local_sem)
        mine.start()
        first = [copy(0, me, sibling, src=x_ref)]
        first += [copy(1 + j, me, (*chip, c), src=x_ref) for j, chip in enumerate(chips)]
        for cp in first:
            cp.start()
        passed = [copy(4 + j, (*chip, c), sibling) for j, chip in enumerate(chips)]
        for j, chip in enumerate(chips):
            copy(1 + j, (*chip, c), me).wait_recv()
            passed[j].start()
        copy(0, sibling, me).wait_recv()
        for j, chip in enumerate(chips):
            copy(4 + j, (*chip, 1 - c), me).wait_recv()
        for cp in first + passed:
            cp.wait_send()
        mine.wait()

    return pl.pallas_call(
        body, name=name, in_specs=[pl.BlockSpec(memory_space=HBM)], out_specs=pl.BlockSpec(memory_space=HBM),
        out_shape=jax.ShapeDtypeStruct((N_DEV,) + blk.shape, blk.dtype),
        scratch_shapes=[pltpu.SemaphoreType.DMA((7,)), pltpu.SemaphoreType.DMA((7,)), pltpu.SemaphoreType.DMA],
    )(blk)


def _all_gather_async(blk, name, collective_id):
    def body(x_ref, out_ref, send_sems, recv_sems, local_sem):
        x, y, c = _place()
        me, sibling = (x, y, c), (x, y, 1 - c)
        chips = _other_chips(x, y)
        barrier = pltpu.get_barrier_semaphore()
        for peer in [sibling] + [(*chip, c) for chip in chips]:
            pl.semaphore_signal(barrier, inc=1, device_id=peer, device_id_type=MESH)
        pl.semaphore_wait(barrier, 4)

        def slot(px, py, pc):
            return out_ref.at[4 * px + 2 * py + pc]

        def copy(k, block, to, src=None):
            return pltpu.make_async_remote_copy(
                src_ref=slot(*block) if src is None else src, dst_ref=slot(*block),
                send_sem=send_sems.at[k], recv_sem=recv_sems.at[k], device_id=to, device_id_type=MESH)

        mine = pltpu.make_async_copy(x_ref, slot(*me), local_sem)
        mine.start()
        first = [copy(0, me, sibling, src=x_ref)]
        first += [copy(1 + j, me, (*chip, c), src=x_ref) for j, chip in enumerate(chips)]
        for cp in first:
            cp.start()
        passed = [copy(4 + j, (*chip, c), sibling) for j, chip in enumerate(chips)]
        for j, chip in enumerate(chips):
            copy(1 + j, (*chip, c), me).wait_recv()
            passed[j].start()
        copy(0, sibling, me).wait_recv()
        for j, chip in enumerate(chips):
            copy(4 + j, (*chip, 1 - c), me).wait_recv()
        for cp in first + passed:
            cp.wait_send()
        mine.wait()

    return pl.kernel(
        body, name=name, out_type=jax.ShapeDtypeStruct((N_DEV,) + blk.shape, blk.dtype),
        mesh=plsc.ScalarSubcoreMesh(axis_name="seq_core", num_cores=1),
        scratch_types=[pltpu.SemaphoreType.DMA((7,)), pltpu.SemaphoreType.DMA((7,)), pltpu.SemaphoreType.DMA],
        compiler_params=pltpu.CompilerParams(collective_id=collective_id),
    )(blk)


def _swap_with_sibling(parts, name):
    _, R, C = parts.shape

    def body(p_ref, got_ref, send_sems, recv_sems):
        x, y, c = _place()
        copies = [pltpu.make_async_remote_copy(
            src_ref=p_ref.at[2 * k + (1 - c)], dst_ref=got_ref.at[k], send_sem=send_sems.at[k],
            recv_sem=recv_sems.at[k], device_id=(x, y, 1 - c), device_id_type=MESH) for k in range(N_CHIP)]
        for cp in copies:
            cp.start()
        for cp in copies:
            cp.wait()

    return pl.pallas_call(
        body, name=name, in_specs=[pl.BlockSpec(memory_space=HBM)], out_specs=pl.BlockSpec(memory_space=HBM),
        out_shape=jax.ShapeDtypeStruct((N_CHIP, R, C), parts.dtype),
        scratch_shapes=[pltpu.SemaphoreType.DMA((N_CHIP,)), pltpu.SemaphoreType.DMA((N_CHIP,))],
    )(parts)


def _add_sibling(parts, got, c_idx, name):
    _, R, C = parts.shape
    tr, tc = _tile(R, 512), _tile(C, 1024)

    def body(c_ref, p_ref, g_ref, o_ref):
        o_ref[...] = (p_ref[...].astype(F32) + g_ref[...].astype(F32)).astype(o_ref.dtype)

    grid_spec = pltpu.PrefetchScalarGridSpec(
        num_scalar_prefetch=1, grid=(N_CHIP, R // tr, C // tc),
        in_specs=[pl.BlockSpec((None, tr, tc), lambda k, i, j, c_ref: (2 * k + c_ref[0], i, j)),
                  pl.BlockSpec((None, tr, tc), lambda k, i, j, c_ref: (k, i, j))],
        out_specs=pl.BlockSpec((None, tr, tc), lambda k, i, j, c_ref: (k, i, j)))
    return pl.pallas_call(
        body, name=name, grid_spec=grid_spec, out_shape=jax.ShapeDtypeStruct((N_CHIP, R, C), parts.dtype),
        compiler_params=_params(("parallel", "parallel", "parallel")),
    )(c_idx, parts, got)


def _scatter_to_chips(sums, name):
    _, R, C = sums.shape

    def body(q_ref, got_ref, send_sems, recv_sems, local_sem):
        x, y, c = _place()
        my_chip = 2 * x + y
        mine = pltpu.make_async_copy(q_ref.at[my_chip], got_ref.at[my_chip], local_sem)
        mine.start()
        copies = [pltpu.make_async_remote_copy(
            src_ref=q_ref.at[2 * px + py], dst_ref=got_ref.at[my_chip], send_sem=send_sems.at[j],
            recv_sem=recv_sems.at[j], device_id=(px, py, c), device_id_type=MESH)
            for j, (px, py) in enumerate(_other_chips(x, y))]
        for cp in copies:
            cp.start()
        for cp in copies:
            cp.wait()
        mine.wait()

    return pl.pallas_call(
        body, name=name, in_specs=[pl.BlockSpec(memory_space=HBM)], out_specs=pl.BlockSpec(memory_space=HBM),
        out_shape=jax.ShapeDtypeStruct((N_CHIP, R, C), sums.dtype),
        scratch_shapes=[pltpu.SemaphoreType.DMA((3,)), pltpu.SemaphoreType.DMA((3,)), pltpu.SemaphoreType.DMA],
    )(sums)


def _adamw_math(w, g, m, v):
    m = ADAM_B1 * m + (1.0 - ADAM_B1) * g
    v = ADAM_B2 * v + (1.0 - ADAM_B2) * (g * g)
    m_hat = m / (1.0 - ADAM_B1 ** ADAM_STEP)
    v_hat = v / (1.0 - ADAM_B2 ** ADAM_STEP)
    delta = -ADAM_LR * (m_hat / (jnp.sqrt(v_hat) + ADAM_EPS) + ADAM_WD * w)
    return delta, m, v


def _sum_adamw(parts, w, m, v, name):
    n_parts, R, C = parts.shape
    tr, tc = _tile(R, 256), _tile(C, 1024)

    def body(p_ref, w_ref, m_ref, v_ref, g_out, d_out, m_out, v_out):
        g = p_ref[0].astype(F32)
        for k in range(1, n_parts):
            g = g + p_ref[k].astype(F32)
        delta, mn, vn = _adamw_math(w_ref[...], g, m_ref[...], v_ref[...])
        g_out[...] = g
        d_out[...] = delta
        m_out[...] = mn
        v_out[...] = vn

    blk = pl.BlockSpec((tr, tc), lambda i, j: (i, j))
    shp = jax.ShapeDtypeStruct((R, C), F32)
    return pl.pallas_call(
        body, name=name, grid=(R // tr, C // tc),
        in_specs=[pl.BlockSpec((n_parts, tr, tc), lambda i, j: (0, i, j)), blk, blk, blk],
        out_specs=[blk, blk, blk, blk], out_shape=[shp, shp, shp, shp],
        compiler_params=_params(("parallel", "parallel")),
    )(parts, w, m, v)


def _reduce_scatter(parts, c_idx, tag):
    got = _swap_with_sibling(parts, "rs_sibling_" + tag)
    sums = _add_sibling(parts, got, c_idx, "rs_add_" + tag)
    return _scatter_to_chips(sums, "rs_chips_" + tag)


def _rows128(a):
    return a.reshape(-1, 128)


def kernel(x, mix_norm_g, w_in, conv_w, spatial_w, spatial_b, conv_out_norm_g, gmlp_out_norm_g, w_out, mlp_norm_g, w_up, w_down, final_norm_g, loss_target, m_mix_norm_g, m_w_in, m_conv_w, m_spatial_w, m_spatial_b, m_conv_out_norm_g, m_gmlp_out_norm_g, m_w_out, m_mlp_norm_g, m_w_up, m_w_down, m_final_norm_g, v_mix_norm_g, v_w_in, v_conv_w, v_spatial_w, v_spatial_b, v_conv_out_norm_g, v_gmlp_out_norm_g, v_w_out, v_mlp_norm_g, v_w_up, v_w_down, v_final_norm_g):
    n_seq, seq, D = x.shape
    T = n_seq * seq
    n_in = w_in.shape[2]
    n_out = w_out.shape[1]
    n_up = w_up.shape[2]
    wc = conv_w.shape[2] * N_DEV
    n_head = wc // HEAD
    FF = n_up * N_DEV
    assert N_DEV * n_in == 5 * wc and seq % HEAD == 0 and D == 2 * wc

    c_idx = lax.axis_index("c").astype(jnp.int32).reshape(1)
    my_dev = 4 * lax.axis_index("x") + 2 * lax.axis_index("y") + lax.axis_index("c")

    xf = x.reshape(T, D)
    tgt = loss_target.reshape(T, D)

    win_g = _all_gather_async(w_in[0].astype(BF16), "ag_w_in", 0)
    wout_g = _all_gather_async(w_out[0].astype(BF16), "ag_w_out", 1).reshape(D, D)
    wup_g = _all_gather_async(w_up[0].astype(BF16), "ag_w_up", 2)
    wdown_g = _all_gather_async(w_down[0].astype(BF16), "ag_w_down", 3).reshape(FF, D)
    cw_pad = jnp.pad(conv_w[0], ((0, HALO - CONV_K), (0, 0)))
    cw_g = _all_gather(cw_pad, "ag_conv_w")
    conv_full = jnp.transpose(cw_g[:, :CONV_K, :], (1, 0, 2)).reshape(CONV_K, wc)

    sw = spatial_w[0]
    sbt = spatial_b[0].T
    g_mix, g_a, g_b, g_mlp = mix_norm_g, conv_out_norm_g, gmlp_out_norm_g, mlp_norm_g
    g_fin = final_norm_g.reshape(1, D)

    bm = _tile(T, 512)
    bn = _tile(D, 1024)

    xn = _rms_fwd(xf, g_mix, "norm_mix")
    proj = _matmul(
        "proj", (N_DEV, T // bm, 1), NN, [xn, win_g],
        [pl.BlockSpec((bm, D), lambda n, m, k: (m, 0)), pl.BlockSpec((None, D, n_in), lambda n, m, k: (n, 0, 0))],
        [jax.ShapeDtypeStruct((T, N_DEV * n_in), F32)], [pl.BlockSpec((bm, n_in), lambda n, m, k: (m, n))],
        None, _ident)[0]
    y = _mixer_fwd(proj, conv_full, sw, sbt, g_a, g_b, n_seq, seq)
    h1 = _matmul(
        "out_proj", (D // bn, T // bm, 1), NN, [y, wout_g, xf],
        [pl.BlockSpec((bm, D), lambda n, m, k: (m, 0)), pl.BlockSpec((D, bn), lambda n, m, k: (0, n)),
         pl.BlockSpec((bm, bn), lambda n, m, k: (m, n))],
        [jax.ShapeDtypeStruct((T, D), F32)], [pl.BlockSpec((bm, bn), lambda n, m, k: (m, n))],
        None, lambda acc, res: (res + acc,))[0]
    xn2 = _rms_fwd(h1, g_mlp, "norm_mlp")
    bu = _tile(n_up, 1024)
    per = n_up // bu

    def up_epilogue(acc):
        r = jnp.maximum(acc, 0.0)
        return acc, r * r

    up, act = _matmul(
        "up_proj", (FF // bu, T // bm, 1), NN, [xn2, wup_g],
        [pl.BlockSpec((bm, D), lambda n, m, k: (m, 0)),
         pl.BlockSpec((None, D, bu), lambda n, m, k: (n // per, 0, n % per))],
        [jax.ShapeDtypeStruct((T, FF), BF16)] * 2, [pl.BlockSpec((bm, bu), lambda n, m, k: (m, n))] * 2,
        None, up_epilogue)
    bk = _tile(FF, 2048)
    h2 = _matmul(
        "down_proj", (D // bn, T // bm, FF // bk), NN, [act, wdown_g, h1],
        [pl.BlockSpec((bm, bk), lambda n, m, k: (m, k)), pl.BlockSpec((bk, bn), lambda n, m, k: (k, n)),
         pl.BlockSpec((bm, bn), lambda n, m, k: (m, n))],
        [jax.ShapeDtypeStruct((T, D), F32)], [pl.BlockSpec((bm, bn), lambda n, m, k: (m, n))],
        (bm, bn), lambda acc, res: (res + acc,))[0]

    dh2, dh2b, gg_fin, loss_row = _loss_head(h2, tgt, g_fin)
    loss = lax.psum(loss_row[0, 0], ("x", "y", "c"))

    bt = _tile(T, 2048)
    bw = _tile(D, 1024)
    dup = _matmul(
        "d_act", (FF // bu, T // bm, 1), NT, [dh2b, wdown_g, up],
        [pl.BlockSpec((bm, D), lambda n, m, k: (m, 0)), pl.BlockSpec((bu, D), lambda n, m, k: (n, 0)),
         pl.BlockSpec((bm, bu), lambda n, m, k: (m, n))],
        [jax.ShapeDtypeStruct((T, FF), BF16)], [pl.BlockSpec((bm, bu), lambda n, m, k: (m, n))],
        None, lambda acc, u: (acc * (2.0 * jnp.maximum(u.astype(F32), 0.0)),))[0]
    bf = _tile(FF, 1024)
    gp_down = _matmul(
        "gw_down", (FF // bf, D // bw, T // bt), TN, [act, dh2b],
        [pl.BlockSpec((bt, bf), lambda m, n, k: (k, m)), pl.BlockSpec((bt, bw), lambda m, n, k: (k, n))],
        [jax.ShapeDtypeStruct((FF, D), BF16)], [pl.BlockSpec((bf, bw), lambda m, n, k: (m, n))],
        (bf, bw), _ident)[0]
    dxn2 = _matmul(
        "d_xn2", (D // bn, T // bm, N_DEV), NT, [dup, wup_g],
        [pl.BlockSpec((bm, n_up), lambda n, m, k: (m, k)), pl.BlockSpec((None, bn, n_up), lambda n, m, k: (k, n, 0))],
        [jax.ShapeDtypeStruct((T, D), F32)], [pl.BlockSpec((bm, bn), lambda n, m, k: (m, n))],
        (bm, bn), _ident)[0]
    gp_up = _matmul(
        "gw_up", (D // bw, FF // bu, T // bt), TN, [xn2, dup],
        [pl.BlockSpec((bt, bw), lambda m, n, k: (k, m)), pl.BlockSpec((bt, bu), lambda m, n, k: (k, n))],
        [jax.ShapeDtypeStruct((N_DEV, D, n_up), BF16)],
        [pl.BlockSpec((None, bw, bu), lambda m, n, k: (n // per, m, n % per))],
        (bw, bu), _ident)[0]
    dh1, dh1b, gg_mlp = _rms_bwd(dxn2, h1, g_mlp, dh2, "norm_mlp_bwd", True)

    dy = _matmul(
        "d_y", (D // bn, T // bm, 1), NT, [dh1b, wout_g],
        [pl.BlockSpec((bm, D), lambda n, m, k: (m, 0)), pl.BlockSpec((bn, D), lambda n, m, k: (n, 0))],
        [jax.ShapeDtypeStruct((T, D), F32)], [pl.BlockSpec((bm, bn), lambda n, m, k: (m, n))],
        None, _ident)[0]
    gp_out = _matmul(
        "gw_out", (D // bw, D // bn, T // bt), TN, [y, dh1b],
        [pl.BlockSpec((bt, bw), lambda m, n, k: (k, m)), pl.BlockSpec((bt, bn), lambda m, n, k: (k, n))],
        [jax.ShapeDtypeStruct((D, D), BF16)], [pl.BlockSpec((bw, bn), lambda m, n, k: (m, n))],
        (bw, bn), _ident)[0]
    dproj, gl_conv, gl_a, gl_b, gl_sw, gl_sbt = _mixer_bwd(proj, dy, conv_full, sw, sbt, g_a, g_b, n_seq, seq)
    dxn = _matmul(
        "d_xn", (D // bn, T // bm, N_DEV), NT, [dproj, win_g],
        [pl.BlockSpec((bm, n_in), lambda n, m, k: (m, k)), pl.BlockSpec((None, bn, n_in), lambda n, m, k: (k, n, 0))],
        [jax.ShapeDtypeStruct((T, D), F32)], [pl.BlockSpec((bm, bn), lambda n, m, k: (m, n))],
        (bm, bn), _ident)[0]
    gp_in = _matmul(
        "gw_in", (D // bw, N_DEV, T // bt), TN, [xn, dproj],
        [pl.BlockSpec((bt, bw), lambda m, n, k: (k, m)), pl.BlockSpec((bt, n_in), lambda m, n, k: (k, n))],
        [jax.ShapeDtypeStruct((N_DEV, D, n_in), BF16)], [pl.BlockSpec((None, bw, n_in), lambda m, n, k: (n, m, 0))],
        (bw, n_in), _ident)[0]
    grad_x, gg_mix = _rms_bwd(dxn, xf, g_mix, dh1, "norm_mix_bwd", False)

    outs = {}
    for tag, gp, w, m, v in (("w_in", gp_in, w_in, m_w_in, v_w_in),
                             ("w_out", gp_out.reshape(N_DEV, n_out, D), w_out, m_w_out, v_w_out),
                             ("w_up", gp_up, w_up, m_w_up, v_w_up),
                             ("w_down", gp_down.reshape(N_DEV, n_up, D), w_down, m_w_down, v_w_down)):
        four = _reduce_scatter(gp, c_idx, tag)
        outs[tag] = [a[None] for a in _sum_adamw(four, w[0], m[0], v[0], "adamw_" + tag)]

    small = [("mix_norm_g", gg_mix, mix_norm_g, m_mix_norm_g, v_mix_norm_g),
             ("conv_w", gl_conv, None, None, None),
             ("spatial_w", gl_sw, spatial_w, m_spatial_w, v_spatial_w),
             ("spatial_b", gl_sbt.T, spatial_b, m_spatial_b, v_spatial_b),
             ("conv_out_norm_g", gl_a, conv_out_norm_g, m_conv_out_norm_g, v_conv_out_norm_g),
             ("gmlp_out_norm_g", gl_b, gmlp_out_norm_g, m_gmlp_out_norm_g, v_gmlp_out_norm_g),
             ("mlp_norm_g", gg_mlp, mlp_norm_g, m_mlp_norm_g, v_mlp_norm_g),
             ("final_norm_g", gg_fin, final_norm_g, m_final_norm_g, v_final_norm_g)]
    packed_g = jnp.concatenate([_rows128(g) for _, g, _, _, _ in small], axis=0)
    zeros_cw = jnp.zeros((CONV_K * wc // 128, 128), F32)
    pack = lambda idx: jnp.concatenate(
        [zeros_cw if item[2] is None else _rows128(item[idx]) for item in small], axis=0)
    all_g = _all_gather(packed_g, "ag_small_grads")
    sg, sd, sm, sv = _sum_adamw(all_g, pack(2), pack(3), pack(4), "adamw_small")
    row = 0
    for name, g, w, _, _ in small:
        n_rows = g.size // 128
        if w is not None:
            outs[name] = [a[row:row + n_rows].reshape(w.shape) for a in (sg, sd, sm, sv)]
        else:
            conv_grad_full = sg[row:row + n_rows].reshape(CONV_K, wc)
        row += n_rows
    cpd = wc // N_DEV
    conv_grad = lax.dynamic_slice(conv_grad_full, (0, my_dev * cpd), (CONV_K, cpd))
    pad8 = lambda a: jnp.pad(a, ((0, HALO - CONV_K), (0, 0)))
    outs["conv_w"] = [a[:CONV_K][None] for a in _sum_adamw(
        pad8(conv_grad)[None], pad8(conv_w[0]), pad8(m_conv_w[0]), pad8(v_conv_w[0]), "adamw_conv_w")]

    order = ["mix_norm_g", "w_in", "conv_w", "spatial_w", "spatial_b", "conv_out_norm_g", "gmlp_out_norm_g",
             "w_out", "mlp_norm_g", "w_up", "w_down", "final_norm_g"]
    result = [loss, grad_x.reshape(n_seq, seq, D)]
    for k in range(4):
        result += [outs[n][k] for n in order]
    return tuple(result)
```

```python
import functools
import math

import jax
import jax.numpy as jnp
from jax import lax
from jax.experimental import pallas as pl
from jax.experimental.pallas import tpu as pltpu
from jax.experimental.pallas import tpu_sc as plsc

F32 = jnp.float32
BF16 = jnp.bfloat16
MESH = pl.DeviceIdType.MESH
HBM = pltpu.HBM

EPS = 1e-5
HEAD = 128
CONV_K = 3
N_DEV = 8
N_CHIP = 4
VMEM_LIMIT_BYTES = 56 * 1024 * 1024

ADAM_LR = 0.001
ADAM_B1 = 0.9
ADAM_B2 = 0.999
ADAM_EPS = 1e-08
ADAM_WD = 0.01
ADAM_STEP = 10

GELU_K0 = math.sqrt(2.0 / math.pi)
GELU_K1 = 0.044715

NN = (((1,), (0,)), ((), ()))
NT = (((1,), (1,)), ((), ()))
TN = (((0,), (0,)), ((), ()))


def _params(semantics):
    return pltpu.CompilerParams(dimension_semantics=semantics, vmem_limit_bytes=VMEM_LIMIT_BYTES)


def _tile(dim, want):
    if dim <= want:
        return dim
    for t in range(want - want % 8, 0, -8):
        if dim % t == 0:
            return t
    raise ValueError((dim, want))


def _sibling_copies(p_ref, got_ref, send_sems, recv_sems):
    x, y, c = _place()
    return [pltpu.make_async_remote_copy(
        src_ref=p_ref.at[2 * k + (1 - c)], dst_ref=got_ref.at[k], send_sem=send_sems.at[k],
        recv_sem=recv_sems.at[k], device_id=(x, y, 1 - c), device_id_type=MESH) for k in range(N_CHIP)]


def _matmul(name, grid, dims, operands, in_specs, out_shapes, out_specs, acc_shape, epilogue, swap=None):
    n_in = len(operands)
    n_out = len(out_shapes)
    nk = grid[2]
    n_host = 0 if swap is None else 1

    def body(*refs):
        a_ref, b_ref = refs[0], refs[1]
        extra = refs[2:n_in]
        outs = refs[n_in + n_host:n_in + n_host + n_out]
        if swap is not None:
            ids = [pl.program_id(d) for d in range(3)]
            copies = _sibling_copies(refs[n_in], refs[n_in + 1 + n_out], refs[-2], refs[-1])

            @pl.when(functools.reduce(jnp.logical_and, [i == 0 for i in ids]))
            def _():
                for cp in copies:
                    cp.start()

        part = lax.dot_general(a_ref[...], b_ref[...], dims, preferred_element_type=F32)

        def finish(acc):
            res = epilogue(acc, *[e[...] for e in extra])
            for o, r in zip(outs, res):
                o[...] = r.astype(o.dtype)

        if nk == 1:
            finish(part)
        else:
            acc_ref = refs[n_in + n_out + 2 * n_host]
            k = pl.program_id(2)

            @pl.when(k == 0)
            def _():
                acc_ref[...] = part

            @pl.when(k > 0)
            def _():
                acc_ref[...] += part

            @pl.when(k == nk - 1)
            def _():
                finish(acc_ref[...])

        if swap is not None:
            @pl.when(functools.reduce(jnp.logical_and, [i == n - 1 for i, n in zip(ids, grid)]))
            def _():
                for cp in copies:
                    cp.wait()

    scratch = [pltpu.VMEM(acc_shape, F32)] if nk > 1 else []
    semantics = ("parallel", "parallel", "arbitrary")
    if swap is not None:
        hbm = pl.BlockSpec(memory_space=HBM)
        operands, in_specs = list(operands) + [swap], list(in_specs) + [hbm]
        out_shapes = list(out_shapes) + [jax.ShapeDtypeStruct((N_CHIP,) + swap.shape[1:], swap.dtype)]
        out_specs = list(out_specs) + [hbm]
        scratch += [pltpu.SemaphoreType.DMA((N_CHIP,)), pltpu.SemaphoreType.DMA((N_CHIP,))]
        semantics = ("arbitrary", "arbitrary", "arbitrary")
    return pl.pallas_call(
        body, name=name, grid=grid, in_specs=in_specs, out_specs=out_specs, out_shape=out_shapes,
        scratch_shapes=scratch, compiler_params=_params(semantics),
    )(*operands)


def _ident(acc):
    return (acc,)


ROW_TILE = 256


def _rms_fwd(x, g, name):
    T, D = x.shape
    tr = _tile(T, ROW_TILE)

    def body(x_ref, g_ref, o_ref):
        xv = x_ref[...]
        inv = lax.rsqrt(jnp.mean(xv * xv, axis=-1, keepdims=True) + EPS)
        o_ref[...] = (xv * inv * g_ref[...]).astype(o_ref.dtype)

    return pl.pallas_call(
        body, name=name, grid=(T // tr,),
        in_specs=[pl.BlockSpec((tr, D), lambda i: (i, 0)), pl.BlockSpec((1, D), lambda i: (0, 0))],
        out_specs=pl.BlockSpec((tr, D), lambda i: (i, 0)),
        out_shape=jax.ShapeDtypeStruct((T, D), BF16),
        compiler_params=_params(("parallel",)),
    )(x, g)


def _rms_bwd(dy, x, g, dres, name, want_bf16):
    T, D = x.shape
    tr = _tile(T, ROW_TILE)

    def body(dy_ref, x_ref, g_ref, dres_ref, *outs):
        dx_ref, gg_ref = outs[0], outs[-1]
        i = pl.program_id(0)
        xv = x_ref[...]
        dyv = dy_ref[...]
        inv = lax.rsqrt(jnp.mean(xv * xv, axis=-1, keepdims=True) + EPS)
        gd = dyv * g_ref[...]
        dot = jnp.mean(gd * xv, axis=-1, keepdims=True)
        dx = dres_ref[...] + (inv * gd - xv * (inv * inv * inv * dot))
        dx_ref[...] = dx
        if want_bf16:
            outs[1][...] = dx.astype(BF16)
        part = jnp.sum(dyv * xv * inv, axis=0, keepdims=True)

        @pl.when(i == 0)
        def _():
            gg_ref[...] = part

        @pl.when(i > 0)
        def _():
            gg_ref[...] += part

    row = pl.BlockSpec((tr, D), lambda i: (i, 0))
    vec = pl.BlockSpec((1, D), lambda i: (0, 0))
    out_shape = [jax.ShapeDtypeStruct((T, D), F32)]
    out_specs = [row]
    if want_bf16:
        out_shape.append(jax.ShapeDtypeStruct((T, D), BF16))
        out_specs.append(row)
    out_shape.append(jax.ShapeDtypeStruct((1, D), F32))
    out_specs.append(vec)
    return pl.pallas_call(
        body, name=name, grid=(T // tr,), in_specs=[row, row, vec, row],
        out_specs=out_specs, out_shape=out_shape, compiler_params=_params(("arbitrary",)),
    )(dy, x, g, dres)


def _loss_head(h, target, g):
    T, D = h.shape
    tr = _tile(T, ROW_TILE)

    def body(h_ref, t_ref, g_ref, dh_ref, dhb_ref, gg_ref, loss_ref):
        i = pl.program_id(0)
        hv = h_ref[...]
        gv = g_ref[...]
        inv = lax.rsqrt(jnp.mean(hv * hv, axis=-1, keepdims=True) + EPS)
        diff = hv * inv * gv - t_ref[...]
        lpart = 0.5 * jnp.sum(jnp.mean(diff * diff, axis=-1, keepdims=True), axis=0, keepdims=True)
        dout = diff * (1.0 / D)
        gd = dout * gv
        dot = jnp.mean(gd * hv, axis=-1, keepdims=True)
        dh = inv * gd - hv * (inv * inv * inv * dot)
        dh_ref[...] = dh
        dhb_ref[...] = dh.astype(BF16)
        part = jnp.sum(dout * hv * inv, axis=0, keepdims=True)
        lrow = jnp.broadcast_to(lpart, (1, 128))

        @pl.when(i == 0)
        def _():
            gg_ref[...] = part
            loss_ref[...] = lrow

        @pl.when(i > 0)
        def _():
            gg_ref[...] += part
            loss_ref[...] += lrow

    row = pl.BlockSpec((tr, D), lambda i: (i, 0))
    vec = pl.BlockSpec((1, D), lambda i: (0, 0))
    return pl.pallas_call(
        body, name="loss_head", grid=(T // tr,), in_specs=[row, row, vec],
        out_specs=[row, row, vec, pl.BlockSpec((1, 128), lambda i: (0, 0))],
        out_shape=[jax.ShapeDtypeStruct((T, D), F32), jax.ShapeDtypeStruct((T, D), BF16),
                   jax.ShapeDtypeStruct((1, D), F32), jax.ShapeDtypeStruct((1, 128), F32)],
        compiler_params=_params(("arbitrary",)),
    )(h, target, g)


HALO = 8


def _gelu_parts(x):
    th = jnp.tanh(GELU_K0 * (x + GELU_K1 * (x * x * x)))
    return x * (0.5 * (1.0 + th)), th


def _gelu_grad(x, th):
    return 0.5 * (1.0 + th) + (0.5 * GELU_K0) * x * (1.0 - th * th) * (1.0 + (3.0 * GELU_K1) * (x * x))


def _conv_fwd(b_ref, c_ref, h_ref, ch_ref, hh_ref, w_ref, first):
    tt, wc = c_ref.shape
    c = c_ref[...]
    h = h_ref[...]
    hc = c * h
    prev1 = jnp.where(first, 0.0, ch_ref[HALO - 1:HALO, :] * hh_ref[HALO - 1:HALO, :])
    prev2 = jnp.where(first, 0.0, ch_ref[HALO - 2:HALO - 1, :] * hh_ref[HALO - 2:HALO - 1, :])
    row = lax.broadcasted_iota(jnp.int32, (tt, wc), 0)
    m1 = jnp.where(row == 0, prev1, pltpu.roll(hc, 1, 0))
    m2 = jnp.where(row == 0, prev2, jnp.where(row == 1, prev1, pltpu.roll(hc, 2, 0)))
    conv = w_ref[0:1, :] * m2 + w_ref[1:2, :] * m1 + w_ref[2:3, :] * hc
    return c, h, hc, m1, m2, conv, b_ref[...] * conv


def _tril():
    r = lax.broadcasted_iota(jnp.int32, (HEAD, HEAD), 0)
    s = lax.broadcasted_iota(jnp.int32, (HEAD, HEAD), 1)
    return r >= s


def _spatial_fwd(gvb, sw_ref, sbt_ref, s_scr):
    n_head = sw_ref.shape[0]
    tri = _tril()
    for hd in range(n_head):
        sl = slice(hd * HEAD, (hd + 1) * HEAD)
        wm = jnp.where(tri, sw_ref[hd], 0.0).astype(BF16)
        s_scr[:, sl] = jnp.dot(wm, gvb[:, sl], preferred_element_type=F32) + sbt_ref[:, hd:hd + 1]


def _mixer_specs(n_tiles, wc, row_of, n_grid):
    def grp(g):
        return pl.BlockSpec((HEAD, wc), lambda *ids: (row_of(*ids), g))

    def halo(g):
        return pl.BlockSpec((HALO, wc), lambda *ids: (jnp.maximum(row_of(*ids) * (HEAD // HALO) - 1, 0), g))

    return grp, halo


def _mixer_fwd(proj, conv_w, sw, sbt, g_a, g_b, n_seq, seq):
    T, w5 = proj.shape
    wc = w5 // 5
    n_head = wc // HEAD
    nt = seq // HEAD

    def row_of(s, i, g):
        return s * nt + i

    grp, halo = _mixer_specs(nt, wc, row_of, 3)

    def body(b_ref, c_ref, h_ref, u_ref, v_ref, ch_ref, hh_ref, w_ref, sw_ref, sbt_ref, ga_ref, gb_ref,
             y_ref, s_scr):
        i = pl.program_id(1)
        g = pl.program_id(2)

        @pl.when(g == 0)
        def _():
            ya = _conv_fwd(b_ref, c_ref, h_ref, ch_ref, hh_ref, w_ref, i == 0)[-1]
            inv = lax.rsqrt(jnp.mean(ya * ya, axis=-1, keepdims=True) + EPS)
            y_ref[...] = (ya * inv * ga_ref[...]).astype(BF16)

        @pl.when(g == 1)
        def _():
            gu, _ = _gelu_parts(u_ref[...])
            gv, _ = _gelu_parts(v_ref[...])
            _spatial_fwd(gv.astype(BF16), sw_ref, sbt_ref, s_scr)
            yb = gu * s_scr[...]
            inv = lax.rsqrt(jnp.mean(yb * yb, axis=-1, keepdims=True) + EPS)
            y_ref[...] = (yb * inv * gb_ref[...]).astype(BF16)

    const2 = lambda shape: pl.BlockSpec(shape, lambda s, i, g: (0, 0))
    return pl.pallas_call(
        body, name="mixer_fwd", grid=(n_seq, nt, 2),
        in_specs=[grp(0), grp(1), grp(2), grp(3), grp(4), halo(1), halo(2),
                  const2((CONV_K, wc)), pl.BlockSpec((n_head, HEAD, HEAD), lambda s, i, g: (0, 0, 0)),
                  const2((HEAD, n_head)), const2((1, wc)), const2((1, wc))],
        out_specs=pl.BlockSpec((HEAD, wc), lambda s, i, g: (s * nt + i, g)),
        out_shape=jax.ShapeDtypeStruct((T, 2 * wc), BF16),
        scratch_shapes=[pltpu.VMEM((HEAD, wc), F32)],
        compiler_params=_params(("parallel", "parallel", "arbitrary")),
    )(proj, proj, proj, proj, proj, proj, proj, conv_w, sw, sbt, g_a, g_b)


def _mixer_bwd(proj, dy, conv_w, sw, sbt, g_a, g_b, n_seq, seq):
    T, w5 = proj.shape
    wc = w5 // 5
    n_head = wc // HEAD
    nt = seq // HEAD
    tt = HEAD

    def row_of(s, ir, g):
        return s * nt + (nt - 1 - ir)

    grp, halo = _mixer_specs(nt, wc, row_of, 3)

    def body(b_ref, c_ref, h_ref, u_ref, v_ref, ch_ref, hh_ref, dya_ref, dyb_ref, w_ref, sw_ref, sbt_ref,
             ga_ref, gb_ref, dp_ref, gw_ref, gga_ref, ggb_ref, gsw_ref, gsb_ref,
             carry_scr, stash_scr, s_scr, t_scr, dsum_scr):
        s_id = pl.program_id(0)
        ir = pl.program_id(1)
        g = pl.program_id(2)
        first_tile = jnp.logical_and(s_id == 0, ir == 0)
        last_tile = jnp.logical_and(s_id == n_seq - 1, ir == nt - 1)

        @pl.when(g == 0)
        def _():
            @pl.when(ir == 0)
            def _():
                carry_scr[...] = jnp.zeros_like(carry_scr)

            c, h, hc, m1, m2, conv, ya = _conv_fwd(b_ref, c_ref, h_ref, ch_ref, hh_ref, w_ref, ir == nt - 1)
            inv = lax.rsqrt(jnp.mean(ya * ya, axis=-1, keepdims=True) + EPS)
            dyn = dya_ref[...]
            gd = dyn * ga_ref[...]
            dot = jnp.mean(gd * ya, axis=-1, keepdims=True)
            dya = inv * gd - ya * (inv * inv * inv * dot)
            gg = jnp.sum(dyn * ya * inv, axis=0, keepdims=True)
            dconv = dya * b_ref[...]
            nxt0 = carry_scr[0:1, :]
            nxt1 = carry_scr[1:2, :]
            row = lax.broadcasted_iota(jnp.int32, (tt, wc), 0)
            p1 = jnp.where(row == tt - 1, nxt0, pltpu.roll(dconv, tt - 1, 0))
            p2 = jnp.where(row == tt - 2, nxt0, jnp.where(row == tt - 1, nxt1, pltpu.roll(dconv, tt - 2, 0)))
            dhc = w_ref[2:3, :] * dconv + w_ref[1:2, :] * p1 + w_ref[0:1, :] * p2
            carry_scr[...] = dconv[0:HALO, :]
            dp_ref[...] = (dya * conv).astype(BF16)
            stash_scr[0] = (dhc * h).astype(BF16)
            stash_scr[1] = (dhc * c).astype(BF16)
            gw0 = jnp.sum(dconv * m2, axis=0, keepdims=True)
            gw1 = jnp.sum(dconv * m1, axis=0, keepdims=True)
            gw2 = jnp.sum(dconv * hc, axis=0, keepdims=True)

            @pl.when(first_tile)
            def _():
                gw_ref[0:1, :] = gw0
                gw_ref[1:2, :] = gw1
                gw_ref[2:3, :] = gw2
                gga_ref[...] = gg

            @pl.when(jnp.logical_not(first_tile))
            def _():
                gw_ref[0:1, :] += gw0
                gw_ref[1:2, :] += gw1
                gw_ref[2:3, :] += gw2
                gga_ref[...] += gg

        @pl.when(g == 1)
        def _():
            dp_ref[...] = stash_scr[0]

        @pl.when(g == 2)
        def _():
            dp_ref[...] = stash_scr[1]

        @pl.when(g == 3)
        def _():
            u = u_ref[...]
            v = v_ref[...]
            gu, thu = _gelu_parts(u)
            gv, thv = _gelu_parts(v)
            gvb = gv.astype(BF16)
            _spatial_fwd(gvb, sw_ref, sbt_ref, s_scr)
            sv = s_scr[...]
            yb = gu * sv
            inv = lax.rsqrt(jnp.mean(yb * yb, axis=-1, keepdims=True) + EPS)
            dyn = dyb_ref[...]
            gd = dyn * gb_ref[...]
            dot = jnp.mean(gd * yb, axis=-1, keepdims=True)
            dyb = inv * gd - yb * (inv * inv * inv * dot)
            gg = jnp.sum(dyn * yb * inv, axis=0, keepdims=True)
            ds = dyb * gu
            dsb = ds.astype(BF16)
            tri = _tril()

            @pl.when(first_tile)
            def _():
                ggb_ref[...] = gg
                dsum_scr[...] = ds
                gsw_ref[...] = jnp.zeros_like(gsw_ref)

            @pl.when(jnp.logical_not(first_tile))
            def _():
                ggb_ref[...] += gg
                dsum_scr[...] += ds

            for hd in range(n_head):
                sl = slice(hd * HEAD, (hd + 1) * HEAD)
                wm = jnp.where(tri, sw_ref[hd], 0.0).astype(BF16)
                t_scr[:, sl] = lax.dot_general(wm, dsb[:, sl], TN, preferred_element_type=F32)
                gsw_ref[hd] += lax.dot_general(dsb[:, sl], gvb[:, sl], NT, preferred_element_type=F32)
            dp_ref[...] = (dyb * sv * _gelu_grad(u, thu)).astype(BF16)
            stash_scr[0] = (t_scr[...] * _gelu_grad(v, thv)).astype(BF16)

            @pl.when(last_tile)
            def _():
                for hd in range(n_head):
                    sl = slice(hd * HEAD, (hd + 1) * HEAD)
                    gsw_ref[hd] = jnp.where(tri, gsw_ref[hd], 0.0)
                    gsb_ref[:, hd:hd + 1] = jnp.sum(dsum_scr[:, sl], axis=1, keepdims=True)

        @pl.when(g == 4)
        def _():
            dp_ref[...] = stash_scr[0]

    const2 = lambda shape: pl.BlockSpec(shape, lambda s, i, g: (0, 0))
    const3 = pl.BlockSpec((n_head, HEAD, HEAD), lambda s, i, g: (0, 0, 0))
    dy_spec = lambda col: pl.BlockSpec((tt, wc), lambda s, ir, g: (row_of(s, ir, g), col))
    return pl.pallas_call(
        body, name="mixer_bwd", grid=(n_seq, nt, 5),
        in_specs=[grp(0), grp(1), grp(2), grp(3), grp(4), halo(1), halo(2), dy_spec(0), dy_spec(1),
                  const2((CONV_K, wc)), const3, const2((HEAD, n_head)), const2((1, wc)), const2((1, wc))],
        out_specs=[pl.BlockSpec((tt, wc), lambda s, ir, g: (row_of(s, ir, g), g)),
                   const2((CONV_K, wc)), const2((1, wc)), const2((1, wc)), const3, const2((HEAD, n_head))],
        out_shape=[jax.ShapeDtypeStruct((T, 5 * wc), BF16), jax.ShapeDtypeStruct((CONV_K, wc), F32),
                   jax.ShapeDtypeStruct((1, wc), F32), jax.ShapeDtypeStruct((1, wc), F32),
                   jax.ShapeDtypeStruct((n_head, HEAD, HEAD), F32), jax.ShapeDtypeStruct((HEAD, n_head), F32)],
        scratch_shapes=[pltpu.VMEM((HALO, wc), F32), pltpu.VMEM((2, tt, wc), BF16), pltpu.VMEM((tt, wc), F32),
                        pltpu.VMEM((tt, wc), F32), pltpu.VMEM((tt, wc), F32)],
        compiler_params=_params(("arbitrary", "arbitrary", "arbitrary")),
    )(proj, proj, proj, proj, proj, proj, proj, dy, dy, conv_w, sw, sbt, g_a, g_b)


def _place():
    return lax.axis_index("x"), lax.axis_index("y"), lax.axis_index("c")


def _other_chips(x, y):
    return [(1 - x, y), (x, 1 - y), (1 - x, 1 - y)]


def _all_gather(blk, name):
    def body(x_ref, out_ref, send_sems, recv_sems, local_sem):
        x, y, c = _place()
        me, sibling = (x, y, c), (x, y, 1 - c)
        chips = _other_chips(x, y)

        def slot(px, py, pc):
            return out_ref.at[4 * px + 2 * py + pc]

        def copy(k, block, to, src=None):
            return pltpu.make_async_remote_copy(
                src_ref=slot(*block) if src is None else src, dst_ref=slot(*block),
                send_sem=send_sems.at[k], recv_sem=recv_sems.at[k], device_id=to, device_id_type=MESH)

        mine = pltpu.make_async_copy(x_ref, slot(*me), local_sem)
        mine.start()
        first = [copy(0, me, sibling, src=x_ref)]
        first += [copy(1 + j, me, (*chip, c), src=x_ref) for j, chip in enumerate(chips)]
        for cp in first:
            cp.start()
        passed = [copy(4 + j, (*chip, c), sibling) for j, chip in enumerate(chips)]
        for j, chip in enumerate(chips):
            copy(1 + j, (*chip, c), me).wait_recv()
            passed[j].start()
        copy(0, sibling, me).wait_recv()
        for j, chip in enumerate(chips):
            copy(4 + j, (*chip, 1 - c), me).wait_recv()
        for cp in first + passed:
            cp.wait_send()
        mine.wait()

    return pl.pallas_call(
        body, name=name, in_specs=[pl.BlockSpec(memory_space=HBM)], out_specs=pl.BlockSpec(memory_space=HBM),
        out_shape=jax.ShapeDtypeStruct((N_DEV,) + blk.shape, blk.dtype),
        scratch_shapes=[pltpu.SemaphoreType.DMA((7,)), pltpu.SemaphoreType.DMA((7,)), pltpu.SemaphoreType.DMA],
    )(blk)


def _all_gather_async(blk, name, collective_id):
    def body(x_ref, out_ref, send_sems, recv_sems, local_sem):
        x, y, c = _place()
        me, sibling = (x, y, c), (x, y, 1 - c)
        chips = _other_chips(x, y)
        _handshake([sibling] + [(*chip, c) for chip in chips])

        def slot(px, py, pc):
            return out_ref.at[4 * px + 2 * py + pc]

        def copy(k, block, to, src=None):
            return pltpu.make_async_remote_copy(
                src_ref=slot(*block) if src is None else src, dst_ref=slot(*block),
                send_sem=send_sems.at[k], recv_sem=recv_sems.at[k], device_id=to, device_id_type=MESH)

        mine = pltpu.make_async_copy(x_ref, slot(*me), local_sem)
        mine.start()
        first = [copy(0, me, sibling, src=x_ref)]
        first += [copy(1 + j, me, (*chip, c), src=x_ref) for j, chip in enumerate(chips)]
        for cp in first:
            cp.start()
        passed = [copy(4 + j, (*chip, c), sibling) for j, chip in enumerate(chips)]
        for j, chip in enumerate(chips):
            copy(1 + j, (*chip, c), me).wait_recv()
            passed[j].start()
        copy(0, sibling, me).wait_recv()
        for j, chip in enumerate(chips):
            copy(4 + j, (*chip, 1 - c), me).wait_recv()
        for cp in first + passed:
            cp.wait_send()
        mine.wait()

    return _sequencer_call(
        body, name, collective_id, jax.ShapeDtypeStruct((N_DEV,) + blk.shape, blk.dtype),
        [pltpu.SemaphoreType.DMA((7,)), pltpu.SemaphoreType.DMA((7,)), pltpu.SemaphoreType.DMA], blk)


def _sequencer_call(body, name, collective_id, out_type, scratch_types, operand):
    return pl.kernel(
        body, name=name, out_type=out_type, mesh=plsc.ScalarSubcoreMesh(axis_name="seq_core", num_cores=1),
        scratch_types=scratch_types, compiler_params=pltpu.CompilerParams(collective_id=collective_id),
    )(operand)


def _handshake(peers):
    barrier = pltpu.get_barrier_semaphore()
    for peer in peers:
        pl.semaphore_signal(barrier, inc=1, device_id=peer, device_id_type=MESH)
    pl.semaphore_wait(barrier, len(peers))


def _add_sibling(parts, got, c_idx, name):
    _, R, C = parts.shape
    tr, tc = _tile(R, 512), _tile(C, 1024)

    def body(c_ref, p_ref, g_ref, o_ref):
        o_ref[...] = (p_ref[...].astype(F32) + g_ref[...].astype(F32)).astype(o_ref.dtype)

    grid_spec = pltpu.PrefetchScalarGridSpec(
        num_scalar_prefetch=1, grid=(N_CHIP, R // tr, C // tc),
        in_specs=[pl.BlockSpec((None, tr, tc), lambda k, i, j, c_ref: (2 * k + c_ref[0], i, j)),
                  pl.BlockSpec((None, tr, tc), lambda k, i, j, c_ref: (k, i, j))],
        out_specs=pl.BlockSpec((None, tr, tc), lambda k, i, j, c_ref: (k, i, j)))
    return pl.pallas_call(
        body, name=name, grid_spec=grid_spec, out_shape=jax.ShapeDtypeStruct((N_CHIP, R, C), parts.dtype),
        compiler_params=_params(("parallel", "parallel", "parallel")),
    )(c_idx, parts, got)


def _scatter_to_chips(sums, name, collective_id):
    _, R, C = sums.shape

    def body(q_ref, got_ref, send_sems, recv_sems, local_sem):
        x, y, c = _place()
        _handshake([(*chip, c) for chip in _other_chips(x, y)])
        my_chip = 2 * x + y
        mine = pltpu.make_async_copy(q_ref.at[my_chip], got_ref.at[my_chip], local_sem)
        mine.start()
        copies = [pltpu.make_async_remote_copy(
            src_ref=q_ref.at[2 * px + py], dst_ref=got_ref.at[my_chip], send_sem=send_sems.at[j],
            recv_sem=recv_sems.at[j], device_id=(px, py, c), device_id_type=MESH)
            for j, (px, py) in enumerate(_other_chips(x, y))]
        for cp in copies:
            cp.start()
        for cp in copies:
            cp.wait()
        mine.wait()

    return _sequencer_call(
        body, name, collective_id, jax.ShapeDtypeStruct((N_CHIP, R, C), sums.dtype),
        [pltpu.SemaphoreType.DMA((3,)), pltpu.SemaphoreType.DMA((3,)), pltpu.SemaphoreType.DMA], sums)


def _adamw_math(w, g, m, v):
    m = ADAM_B1 * m + (1.0 - ADAM_B1) * g
    v = ADAM_B2 * v + (1.0 - ADAM_B2) * (g * g)
    m_hat = m / (1.0 - ADAM_B1 ** ADAM_STEP)
    v_hat = v / (1.0 - ADAM_B2 ** ADAM_STEP)
    delta = -ADAM_LR * (m_hat / (jnp.sqrt(v_hat) + ADAM_EPS) + ADAM_WD * w)
    return delta, m, v


def _sum_adamw(parts, w, m, v, name):
    n_parts, R, C = parts.shape
    tr, tc = _tile(R, 256), _tile(C, 1024)

    def body(p_ref, w_ref, m_ref, v_ref, g_out, d_out, m_out, v_out):
        g = p_ref[0].astype(F32)
        for k in range(1, n_parts):
            g = g + p_ref[k].astype(F32)
        delta, mn, vn = _adamw_math(w_ref[...], g, m_ref[...], v_ref[...])
        g_out[...] = g
        d_out[...] = delta
        m_out[...] = mn
        v_out[...] = vn

    blk = pl.BlockSpec((tr, tc), lambda i, j: (i, j))
    shp = jax.ShapeDtypeStruct((R, C), F32)
    return pl.pallas_call(
        body, name=name, grid=(R // tr, C // tc),
        in_specs=[pl.BlockSpec((n_parts, tr, tc), lambda i, j: (0, i, j)), blk, blk, blk],
        out_specs=[blk, blk, blk, blk], out_shape=[shp, shp, shp, shp],
        compiler_params=_params(("parallel", "parallel")),
    )(parts, w, m, v)


def _after(value, dep):
    return lax.optimization_barrier((value, dep))[0]


def _rows128(a):
    return a.reshape(-1, 128)


def kernel(x, mix_norm_g, w_in, conv_w, spatial_w, spatial_b, conv_out_norm_g, gmlp_out_norm_g, w_out, mlp_norm_g, w_up, w_down, final_norm_g, loss_target, m_mix_norm_g, m_w_in, m_conv_w, m_spatial_w, m_spatial_b, m_conv_out_norm_g, m_gmlp_out_norm_g, m_w_out, m_mlp_norm_g, m_w_up, m_w_down, m_final_norm_g, v_mix_norm_g, v_w_in, v_conv_w, v_spatial_w, v_spatial_b, v_conv_out_norm_g, v_gmlp_out_norm_g, v_w_out, v_mlp_norm_g, v_w_up, v_w_down, v_final_norm_g):
    n_seq, seq, D = x.shape
    T = n_seq * seq
    n_in = w_in.shape[2]
    n_out = w_out.shape[1]
    n_up = w_up.shape[2]
    wc = conv_w.shape[2] * N_DEV
    n_head = wc // HEAD
    FF = n_up * N_DEV
    assert N_DEV * n_in == 5 * wc and seq % HEAD == 0 and D == 2 * wc

    c_idx = lax.axis_index("c").astype(jnp.int32).reshape(1)
    my_dev = 4 * lax.axis_index("x") + 2 * lax.axis_index("y") + lax.axis_index("c")

    xf = x.reshape(T, D)
    tgt = loss_target.reshape(T, D)

    cw_pad = jnp.pad(conv_w[0], ((0, HALO - CONV_K), (0, 0)))
    cw_g = _all_gather(cw_pad, "ag_conv_w")
    conv_full = jnp.transpose(cw_g[:, :CONV_K, :], (1, 0, 2)).reshape(CONV_K, wc)
    win_g = _all_gather_async(_after(w_in[0].astype(BF16), cw_g), "ag_w_in", 0)
    wout_g = _all_gather_async(_after(w_out[0].astype(BF16), cw_g), "ag_w_out", 1).reshape(D, D)
    wup_g = _all_gather_async(_after(w_up[0].astype(BF16), cw_g), "ag_w_up", 2)
    wdown_g = _all_gather_async(_after(w_down[0].astype(BF16), cw_g), "ag_w_down", 3).reshape(FF, D)

    sw = spatial_w[0]
    sbt = spatial_b[0].T
    g_mix, g_a, g_b, g_mlp = mix_norm_g, conv_out_norm_g, gmlp_out_norm_g, mlp_norm_g
    g_fin = final_norm_g.reshape(1, D)

    bm = _tile(T, 512)
    bn = _tile(D, 1024)

    xn = _rms_fwd(xf, g_mix, "norm_mix")
    proj = _matmul(
        "proj", (N_DEV, T // bm, 1), NN, [xn, win_g],
        [pl.BlockSpec((bm, D), lambda n, m, k: (m, 0)), pl.BlockSpec((None, D, n_in), lambda n, m, k: (n, 0, 0))],
        [jax.ShapeDtypeStruct((T, N_DEV * n_in), F32)], [pl.BlockSpec((bm, n_in), lambda n, m, k: (m, n))],
        None, _ident)[0]
    y = _mixer_fwd(proj, conv_full, sw, sbt, g_a, g_b, n_seq, seq)
    h1 = _matmul(
        "out_proj", (D // bn, T // bm, 1), NN, [y, wout_g, xf],
        [pl.BlockSpec((bm, D), lambda n, m, k: (m, 0)), pl.BlockSpec((D, bn), lambda n, m, k: (0, n)),
         pl.BlockSpec((bm, bn), lambda n, m, k: (m, n))],
        [jax.ShapeDtypeStruct((T, D), F32)], [pl.BlockSpec((bm, bn), lambda n, m, k: (m, n))],
        None, lambda acc, res: (res + acc,))[0]
    xn2 = _rms_fwd(h1, g_mlp, "norm_mlp")
    bu = _tile(n_up, 1024)
    per = n_up // bu

    def up_epilogue(acc):
        r = jnp.maximum(acc, 0.0)
        return acc, r * r

    up, act = _matmul(
        "up_proj", (FF // bu, T // bm, 1), NN, [xn2, wup_g],
        [pl.BlockSpec((bm, D), lambda n, m, k: (m, 0)),
         pl.BlockSpec((None, D, bu), lambda n, m, k: (n // per, 0, n % per))],
        [jax.ShapeDtypeStruct((T, FF), BF16)] * 2, [pl.BlockSpec((bm, bu), lambda n, m, k: (m, n))] * 2,
        None, up_epilogue)
    bk = _tile(FF, 2048)
    h2 = _matmul(
        "down_proj", (D // bn, T // bm, FF // bk), NN, [act, wdown_g, h1],
        [pl.BlockSpec((bm, bk), lambda n, m, k: (m, k)), pl.BlockSpec((bk, bn), lambda n, m, k: (k, n)),
         pl.BlockSpec((bm, bn), lambda n, m, k: (m, n))],
        [jax.ShapeDtypeStruct((T, D), F32)], [pl.BlockSpec((bm, bn), lambda n, m, k: (m, n))],
        (bm, bn), lambda acc, res: (res + acc,))[0]

    dh2, dh2b, gg_fin, loss_row = _loss_head(h2, tgt, g_fin)
    loss = lax.psum(loss_row[0, 0], ("x", "y", "c"))

    bt = _tile(T, 2048)
    bw = _tile(D, 1024)
    dup = _matmul(
        "d_act", (FF // bu, T // bm, 1), NT, [dh2b, wdown_g, up],
        [pl.BlockSpec((bm, D), lambda n, m, k: (m, 0)), pl.BlockSpec((bu, D), lambda n, m, k: (n, 0)),
         pl.BlockSpec((bm, bu), lambda n, m, k: (m, n))],
        [jax.ShapeDtypeStruct((T, FF), BF16)], [pl.BlockSpec((bm, bu), lambda n, m, k: (m, n))],
        None, lambda acc, u: (acc * (2.0 * jnp.maximum(u.astype(F32), 0.0)),))[0]
    bf = _tile(FF, 1024)
    gp_down = _matmul(
        "gw_down", (FF // bf, D // bw, T // bt), TN, [act, dh2b],
        [pl.BlockSpec((bt, bf), lambda m, n, k: (k, m)), pl.BlockSpec((bt, bw), lambda m, n, k: (k, n))],
        [jax.ShapeDtypeStruct((FF, D), BF16)], [pl.BlockSpec((bf, bw), lambda m, n, k: (m, n))],
        (bf, bw), _ident)[0].reshape(N_DEV, n_up, D)
    dxn2, got_down = _matmul(
        "d_xn2", (D // bn, T // bm, N_DEV), NT, [dup, wup_g],
        [pl.BlockSpec((bm, n_up), lambda n, m, k: (m, k)), pl.BlockSpec((None, bn, n_up), lambda n, m, k: (k, n, 0))],
        [jax.ShapeDtypeStruct((T, D), F32)], [pl.BlockSpec((bm, bn), lambda n, m, k: (m, n))],
        (bm, bn), _ident, swap=gp_down)
    gp_up = _matmul(
        "gw_up", (D // bw, FF // bu, T // bt), TN, [xn2, _after(dup, dxn2)],
        [pl.BlockSpec((bt, bw), lambda m, n, k: (k, m)), pl.BlockSpec((bt, bu), lambda m, n, k: (k, n))],
        [jax.ShapeDtypeStruct((N_DEV, D, n_up), BF16)],
        [pl.BlockSpec((None, bw, bu), lambda m, n, k: (n // per, m, n % per))],
        (bw, bu), _ident)[0]
    sums_down = _add_sibling(gp_down, _after(got_down, gp_up), c_idx, "rs_add_w_down")
    four_down = _scatter_to_chips(sums_down, "rs_chips_w_down", 4)
    dh1, dh1b, gg_mlp = _rms_bwd(_after(dxn2, sums_down), h1, g_mlp, dh2, "norm_mlp_bwd", True)

    dy, got_up = _matmul(
        "d_y", (D // bn, T // bm, 1), NT, [dh1b, wout_g],
        [pl.BlockSpec((bm, D), lambda n, m, k: (m, 0)), pl.BlockSpec((bn, D), lambda n, m, k: (n, 0))],
        [jax.ShapeDtypeStruct((T, D), F32)], [pl.BlockSpec((bm, bn), lambda n, m, k: (m, n))],
        None, _ident, swap=gp_up)
    gp_out = _matmul(
        "gw_out", (D // bw, D // bn, T // bt), TN, [y, _after(dh1b, dy)],
        [pl.BlockSpec((bt, bw), lambda m, n, k: (k, m)), pl.BlockSpec((bt, bn), lambda m, n, k: (k, n))],
        [jax.ShapeDtypeStruct((D, D), BF16)], [pl.BlockSpec((bw, bn), lambda m, n, k: (m, n))],
        (bw, bn), _ident)[0].reshape(N_DEV, n_out, D)
    sums_up = _add_sibling(gp_up, _after(got_up, gp_out), c_idx, "rs_add_w_up")
    four_up = _scatter_to_chips(sums_up, "rs_chips_w_up", 5)
    dproj, gl_conv, gl_a, gl_b, gl_sw, gl_sbt = _mixer_bwd(
        proj, _after(dy, sums_up), conv_full, sw, sbt, g_a, g_b, n_seq, seq)
    gp_in, got_out = _matmul(
        "gw_in", (D // bw, N_DEV, T // bt), TN, [xn, dproj],
        [pl.BlockSpec((bt, bw), lambda m, n, k: (k, m)), pl.BlockSpec((bt, n_in), lambda m, n, k: (k, n))],
        [jax.ShapeDtypeStruct((N_DEV, D, n_in), BF16)], [pl.BlockSpec((None, bw, n_in), lambda m, n, k: (n, m, 0))],
        (bw, n_in), _ident, swap=gp_out)
    sums_out = _add_sibling(gp_out, got_out, c_idx, "rs_add_w_out")
    four_out = _scatter_to_chips(sums_out, "rs_chips_w_out", 6)
    dxn, got_in = _matmul(
        "d_xn", (D // bn, T // bm, N_DEV), NT, [_after(dproj, sums_out), win_g],
        [pl.BlockSpec((bm, n_in), lambda n, m, k: (m, k)), pl.BlockSpec((None, bn, n_in), lambda n, m, k: (k, n, 0))],
        [jax.ShapeDtypeStruct((T, D), F32)], [pl.BlockSpec((bm, bn), lambda n, m, k: (m, n))],
        (bm, bn), _ident, swap=gp_in)
    sums_in = _add_sibling(gp_in, got_in, c_idx, "rs_add_w_in")
    four_in = _scatter_to_chips(sums_in, "rs_chips_w_in", 7)
    grad_x, gg_mix = _rms_bwd(_after(dxn, sums_in), xf, g_mix, dh1, "norm_mix_bwd", False)

    outs = {}
    done = grad_x
    for tag, four, w, m, v in (("w_down", four_down, w_down, m_w_down, v_w_down),
                               ("w_up", four_up, w_up, m_w_up, v_w_up),
                               ("w_out", four_out, w_out, m_w_out, v_w_out),
                               ("w_in", four_in, w_in, m_w_in, v_w_in)):
        res = _sum_adamw(_after(four, done), w[0], m[0], v[0], "adamw_" + tag)
        done = res[0]
        outs[tag] = [a[None] for a in res]

    small = [("mix_norm_g", gg_mix, mix_norm_g, m_mix_norm_g, v_mix_norm_g),
             ("conv_w", gl_conv, None, None, None),
             ("spatial_w", gl_sw, spatial_w, m_spatial_w, v_spatial_w),
             ("spatial_b", gl_sbt.T, spatial_b, m_spatial_b, v_spatial_b),
             ("conv_out_norm_g", gl_a, conv_out_norm_g, m_conv_out_norm_g, v_conv_out_norm_g),
             ("gmlp_out_norm_g", gl_b, gmlp_out_norm_g, m_gmlp_out_norm_g, v_gmlp_out_norm_g),
             ("mlp_norm_g", gg_mlp, mlp_norm_g, m_mlp_norm_g, v_mlp_norm_g),
             ("final_norm_g", gg_fin, final_norm_g, m_final_norm_g, v_final_norm_g)]
    packed_g = jnp.concatenate([_rows128(g) for _, g, _, _, _ in small], axis=0)
    zeros_cw = jnp.zeros((CONV_K * wc // 128, 128), F32)
    pack = lambda idx: jnp.concatenate(
        [zeros_cw if item[2] is None else _rows128(item[idx]) for item in small], axis=0)
    all_g = _all_gather_async(packed_g, "ag_small_grads", 8)
    sg, sd, sm, sv = _sum_adamw(_after(all_g, done), pack(2), pack(3), pack(4), "adamw_small")
    row = 0
    for name, g, w, _, _ in small:
        n_rows = g.size // 128
        if w is not None:
            outs[name] = [a[row:row + n_rows].reshape(w.shape) for a in (sg, sd, sm, sv)]
        else:
            conv_grad_full = sg[row:row + n_rows].reshape(CONV_K, wc)
        row += n_rows
    cpd = wc // N_DEV
    conv_grad = lax.dynamic_slice(conv_grad_full, (0, my_dev * cpd), (CONV_K, cpd))
    pad8 = lambda a: jnp.pad(a, ((0, HALO - CONV_K), (0, 0)))
    outs["conv_w"] = [a[:CONV_K][None] for a in _sum_adamw(
        pad8(conv_grad)[None], pad8(conv_w[0]), pad8(m_conv_w[0]), pad8(v_conv_w[0]), "adamw_conv_w")]

    order = ["mix_norm_g", "w_in", "conv_w", "spatial_w", "spatial_b", "conv_out_norm_g", "gmlp_out_norm_g",
             "w_out", "mlp_norm_g", "w_up", "w_down", "final_norm_g"]
    result = [loss, grad_x.reshape(n_seq, seq, D)]
    for k in range(4):
        result += [outs[n][k] for n in order]
    return tuple(result)
```

```python
import functools
import math

import jax
import jax.numpy as jnp
from jax import lax
from jax.experimental import pallas as pl
from jax.experimental.pallas import tpu as pltpu
from jax.experimental.pallas import tpu_sc as plsc

F32 = jnp.float32
BF16 = jnp.bfloat16
MESH = pl.DeviceIdType.MESH
HBM = pltpu.HBM

EPS = 1e-5
HEAD = 128
CONV_K = 3
N_DEV = 8
N_CHIP = 4
VMEM_LIMIT_BYTES = 62 * 1024 * 1024

ADAM_LR = 0.001
ADAM_B1 = 0.9
ADAM_B2 = 0.999
ADAM_EPS = 1e-08
ADAM_WD = 0.01
ADAM_STEP = 10

GELU_K0 = math.sqrt(2.0 / math.pi)
GELU_K1 = 0.044715

NN = (((1,), (0,)), ((), ()))
NT = (((1,), (1,)), ((), ()))
TN = (((0,), (0,)), ((), ()))


def _params(semantics):
    return pltpu.CompilerParams(dimension_semantics=semantics, vmem_limit_bytes=VMEM_LIMIT_BYTES)


def _tile(dim, want):
    if dim <= want:
        return dim
    for t in range(want - want % 8, 0, -8):
        if dim % t == 0:
            return t
    raise ValueError((dim, want))


def _owner_block(ref, j, cols):
    if cols is None:
        return ref.at[j]
    return ref.at[:, pl.ds(pl.multiple_of(j * cols, 128), cols)]


def _sibling_copies(p_ref, got_ref, send_sems, recv_sems, cols):
    x, y, c = _place()
    return [pltpu.make_async_remote_copy(
        src_ref=_owner_block(p_ref, 2 * k + (1 - c), cols), dst_ref=got_ref.at[k], send_sem=send_sems.at[k],
        recv_sem=recv_sems.at[k], device_id=(x, y, 1 - c), device_id_type=MESH) for k in range(N_CHIP)]


def _matmul(name, grid, dims, operands, in_specs, out_shapes, out_specs, epilogue, swap=None):
    n_in = len(operands)
    n_out = len(out_shapes)
    nk = grid[2]
    n_host = 0 if swap is None else 1
    assert nk == 1 or (n_out == 1 and epilogue is _ident and out_shapes[0].dtype == F32)

    def body(*refs):
        a_ref, b_ref = refs[0], refs[1]
        extra = refs[2:n_in]
        outs = refs[n_in + n_host:n_in + n_host + n_out]
        if swap is not None:
            ids = [pl.program_id(d) for d in range(3)]
            copies = _sibling_copies(refs[n_in], refs[n_in + 1 + n_out], refs[-2], refs[-1], swap[1])

            @pl.when(functools.reduce(jnp.logical_and, [i == 0 for i in ids]))
            def _():
                for cp in copies:
                    cp.start()

        part = lax.dot_general(a_ref[...], b_ref[...], dims, preferred_element_type=F32)

        def finish(acc):
            res = epilogue(acc, *[e[...] for e in extra])
            for o, r in zip(outs, res):
                o[...] = r.astype(o.dtype)

        if nk == 1:
            finish(part)
        else:
            k = pl.program_id(2)

            @pl.when(k == 0)
            def _():
                outs[0][...] = part

            @pl.when(k > 0)
            def _():
                outs[0][...] += part

        if swap is not None:
            @pl.when(functools.reduce(jnp.logical_and, [i == n - 1 for i, n in zip(ids, grid)]))
            def _():
                for cp in copies:
                    cp.wait()

    scratch = []
    semantics = ("parallel", "parallel", "arbitrary")
    if swap is not None:
        parts, cols = swap
        got_shape = parts.shape[1:] if cols is None else (parts.shape[0], cols)
        hbm = pl.BlockSpec(memory_space=HBM)
        operands, in_specs = list(operands) + [parts], list(in_specs) + [hbm]
        out_shapes = list(out_shapes) + [jax.ShapeDtypeStruct((N_CHIP,) + got_shape, parts.dtype)]
        out_specs = list(out_specs) + [hbm]
        scratch += [pltpu.SemaphoreType.DMA((N_CHIP,)), pltpu.SemaphoreType.DMA((N_CHIP,))]
        semantics = ("arbitrary", "arbitrary", "arbitrary")
    return pl.pallas_call(
        body, name=name, grid=grid, in_specs=in_specs, out_specs=out_specs, out_shape=out_shapes,
        scratch_shapes=scratch, compiler_params=_params(semantics),
    )(*operands)


def _ident(acc):
    return (acc,)


ROW_TILE = 256


def _rms_fwd(x, g, name):
    T, D = x.shape
    tr = _tile(T, ROW_TILE)

    def body(x_ref, g_ref, o_ref):
        xv = x_ref[...]
        inv = lax.rsqrt(jnp.mean(xv * xv, axis=-1, keepdims=True) + EPS)
        o_ref[...] = (xv * inv * g_ref[...]).astype(o_ref.dtype)

    return pl.pallas_call(
        body, name=name, grid=(T // tr,),
        in_specs=[pl.BlockSpec((tr, D), lambda i: (i, 0)), pl.BlockSpec((1, D), lambda i: (0, 0))],
        out_specs=pl.BlockSpec((tr, D), lambda i: (i, 0)),
        out_shape=jax.ShapeDtypeStruct((T, D), BF16),
        compiler_params=_params(("parallel",)),
    )(x, g)


def _rms_bwd(dy, x, g, dres, name, want_bf16):
    T, D = x.shape
    tr = _tile(T, ROW_TILE)

    def body(dy_ref, x_ref, g_ref, dres_ref, *outs):
        dx_ref, gg_ref = outs[0], outs[-1]
        i = pl.program_id(0)
        xv = x_ref[...]
        dyv = dy_ref[...]
        inv = lax.rsqrt(jnp.mean(xv * xv, axis=-1, keepdims=True) + EPS)
        gd = dyv * g_ref[...]
        dot = jnp.mean(gd * xv, axis=-1, keepdims=True)
        dx = dres_ref[...] + (inv * gd - xv * (inv * inv * inv * dot))
        dx_ref[...] = dx
        if want_bf16:
            outs[1][...] = dx.astype(BF16)
        part = jnp.sum(dyv * xv * inv, axis=0, keepdims=True)

        @pl.when(i == 0)
        def _():
            gg_ref[...] = part

        @pl.when(i > 0)
        def _():
            gg_ref[...] += part

    row = pl.BlockSpec((tr, D), lambda i: (i, 0))
    vec = pl.BlockSpec((1, D), lambda i: (0, 0))
    out_shape = [jax.ShapeDtypeStruct((T, D), F32)]
    out_specs = [row]
    if want_bf16:
        out_shape.append(jax.ShapeDtypeStruct((T, D), BF16))
        out_specs.append(row)
    out_shape.append(jax.ShapeDtypeStruct((1, D), F32))
    out_specs.append(vec)
    return pl.pallas_call(
        body, name=name, grid=(T // tr,), in_specs=[row, row, vec, row],
        out_specs=out_specs, out_shape=out_shape, compiler_params=_params(("arbitrary",)),
    )(dy, x, g, dres)


def _loss_head(h1, mlp, target, g):
    T, D = h1.shape
    tr = _tile(T, ROW_TILE)

    def body(h1_ref, mlp_ref, t_ref, g_ref, dh_ref, dhb_ref, gg_ref, loss_ref):
        i = pl.program_id(0)
        hv = h1_ref[...] + mlp_ref[...]
        gv = g_ref[...]
        inv = lax.rsqrt(jnp.mean(hv * hv, axis=-1, keepdims=True) + EPS)
        diff = hv * inv * gv - t_ref[...]
        lpart = 0.5 * jnp.sum(jnp.mean(diff * diff, axis=-1, keepdims=True), axis=0, keepdims=True)
        dout = diff * (1.0 / D)
        gd = dout * gv
        dot = jnp.mean(gd * hv, axis=-1, keepdims=True)
        dh = inv * gd - hv * (inv * inv * inv * dot)
        dh_ref[...] = dh
        dhb_ref[...] = dh.astype(BF16)
        part = jnp.sum(dout * hv * inv, axis=0, keepdims=True)
        lrow = jnp.broadcast_to(lpart, (1, 128))

        @pl.when(i == 0)
        def _():
            gg_ref[...] = part
            loss_ref[...] = lrow

        @pl.when(i > 0)
        def _():
            gg_ref[...] += part
            loss_ref[...] += lrow

    row = pl.BlockSpec((tr, D), lambda i: (i, 0))
    vec = pl.BlockSpec((1, D), lambda i: (0, 0))
    return pl.pallas_call(
        body, name="loss_head", grid=(T // tr,), in_specs=[row, row, row, vec],
        out_specs=[row, row, vec, pl.BlockSpec((1, 128), lambda i: (0, 0))],
        out_shape=[jax.ShapeDtypeStruct((T, D), F32), jax.ShapeDtypeStruct((T, D), BF16),
                   jax.ShapeDtypeStruct((1, D), F32), jax.ShapeDtypeStruct((1, 128), F32)],
        compiler_params=_params(("arbitrary",)),
    )(h1, mlp, target, g)


HALO = 8


def _gelu_parts(x):
    th = jnp.tanh(GELU_K0 * (x + GELU_K1 * (x * x * x)))
    return x * (0.5 * (1.0 + th)), th


def _gelu_grad(x, th):
    return 0.5 * (1.0 + th) + (0.5 * GELU_K0) * x * (1.0 - th * th) * (1.0 + (3.0 * GELU_K1) * (x * x))


def _conv_fwd(b_ref, c_ref, h_ref, ch_ref, hh_ref, w_ref, first):
    tt, wc = c_ref.shape
    c = c_ref[...]
    h = h_ref[...]
    hc = c * h
    prev1 = jnp.where(first, 0.0, ch_ref[HALO - 1:HALO, :] * hh_ref[HALO - 1:HALO, :])
    prev2 = jnp.where(first, 0.0, ch_ref[HALO - 2:HALO - 1, :] * hh_ref[HALO - 2:HALO - 1, :])
    row = lax.broadcasted_iota(jnp.int32, (tt, wc), 0)
    m1 = jnp.where(row == 0, prev1, pltpu.roll(hc, 1, 0))
    m2 = jnp.where(row == 0, prev2, jnp.where(row == 1, prev1, pltpu.roll(hc, 2, 0)))
    conv = w_ref[0:1, :] * m2 + w_ref[1:2, :] * m1 + w_ref[2:3, :] * hc
    return c, h, hc, m1, m2, conv, b_ref[...] * conv


def _tril():
    r = lax.broadcasted_iota(jnp.int32, (HEAD, HEAD), 0)
    s = lax.broadcasted_iota(jnp.int32, (HEAD, HEAD), 1)
    return r >= s


def _spatial_fwd(gvb, sw_ref, sbt_ref, s_scr):
    n_head = sw_ref.shape[0]
    tri = _tril()
    for hd in range(n_head):
        sl = slice(hd * HEAD, (hd + 1) * HEAD)
        wm = jnp.where(tri, sw_ref[hd], 0.0).astype(BF16)
        s_scr[:, sl] = jnp.dot(wm, gvb[:, sl], preferred_element_type=F32) + sbt_ref[:, hd:hd + 1]


def _mixer_specs(n_tiles, wc, row_of, n_grid):
    def grp(g):
        return pl.BlockSpec((HEAD, wc), lambda *ids: (row_of(*ids), g))

    def halo(g):
        return pl.BlockSpec((HALO, wc), lambda *ids: (jnp.maximum(row_of(*ids) * (HEAD // HALO) - 1, 0), g))

    return grp, halo


def _mixer_fwd(proj, conv_w, sw, sbt, g_a, g_b, n_seq, seq):
    T, w5 = proj.shape
    wc = w5 // 5
    n_head = wc // HEAD
    nt = seq // HEAD

    def row_of(s, i, g):
        return s * nt + i

    grp, halo = _mixer_specs(nt, wc, row_of, 3)

    def body(b_ref, c_ref, h_ref, u_ref, v_ref, ch_ref, hh_ref, w_ref, sw_ref, sbt_ref, ga_ref, gb_ref,
             y_ref, s_scr):
        i = pl.program_id(1)
        g = pl.program_id(2)

        @pl.when(g == 0)
        def _():
            ya = _conv_fwd(b_ref, c_ref, h_ref, ch_ref, hh_ref, w_ref, i == 0)[-1]
            inv = lax.rsqrt(jnp.mean(ya * ya, axis=-1, keepdims=True) + EPS)
            y_ref[...] = (ya * inv * ga_ref[...]).astype(BF16)

        @pl.when(g == 1)
        def _():
            gu, _ = _gelu_parts(u_ref[...])
            gv, _ = _gelu_parts(v_ref[...])
            _spatial_fwd(gv.astype(BF16), sw_ref, sbt_ref, s_scr)
            yb = gu * s_scr[...]
            inv = lax.rsqrt(jnp.mean(yb * yb, axis=-1, keepdims=True) + EPS)
            y_ref[...] = (yb * inv * gb_ref[...]).astype(BF16)

    const2 = lambda shape: pl.BlockSpec(shape, lambda s, i, g: (0, 0))
    return pl.pallas_call(
        body, name="mixer_fwd", grid=(n_seq, nt, 2),
        in_specs=[grp(0), grp(1), grp(2), grp(3), grp(4), halo(1), halo(2),
                  const2((CONV_K, wc)), pl.BlockSpec((n_head, HEAD, HEAD), lambda s, i, g: (0, 0, 0)),
                  const2((HEAD, n_head)), const2((1, wc)), const2((1, wc))],
        out_specs=pl.BlockSpec((HEAD, wc), lambda s, i, g: (s * nt + i, g)),
        out_shape=jax.ShapeDtypeStruct((T, 2 * wc), BF16),
        scratch_shapes=[pltpu.VMEM((HEAD, wc), F32)],
        compiler_params=_params(("parallel", "parallel", "arbitrary")),
    )(proj, proj, proj, proj, proj, proj, proj, conv_w, sw, sbt, g_a, g_b)


def _mixer_bwd(proj, dy, conv_w, sw, sbt, g_a, g_b, n_seq, seq):
    T, w5 = proj.shape
    wc = w5 // 5
    n_head = wc // HEAD
    nt = seq // HEAD
    tt = HEAD

    def row_of(s, ir, g):
        return s * nt + (nt - 1 - ir)

    grp, halo = _mixer_specs(nt, wc, row_of, 3)

    def body(b_ref, c_ref, h_ref, u_ref, v_ref, ch_ref, hh_ref, dya_ref, dyb_ref, w_ref, sw_ref, sbt_ref,
             ga_ref, gb_ref, dp_ref, gw_ref, gga_ref, ggb_ref, gsw_ref, gsb_ref,
             carry_scr, stash_scr, s_scr, t_scr, dsum_scr):
        s_id = pl.program_id(0)
        ir = pl.program_id(1)
        g = pl.program_id(2)
        first_tile = jnp.logical_and(s_id == 0, ir == 0)
        last_tile = jnp.logical_and(s_id == n_seq - 1, ir == nt - 1)

        @pl.when(g == 0)
        def _():
            @pl.when(ir == 0)
            def _():
                carry_scr[...] = jnp.zeros_like(carry_scr)

            c, h, hc, m1, m2, conv, ya = _conv_fwd(b_ref, c_ref, h_ref, ch_ref, hh_ref, w_ref, ir == nt - 1)
            inv = lax.rsqrt(jnp.mean(ya * ya, axis=-1, keepdims=True) + EPS)
            dyn = dya_ref[...]
            gd = dyn * ga_ref[...]
            dot = jnp.mean(gd * ya, axis=-1, keepdims=True)
            dya = inv * gd - ya * (inv * inv * inv * dot)
            gg = jnp.sum(dyn * ya * inv, axis=0, keepdims=True)
            dconv = dya * b_ref[...]
            nxt0 = carry_scr[0:1, :]
            nxt1 = carry_scr[1:2, :]
            row = lax.broadcasted_iota(jnp.int32, (tt, wc), 0)
            p1 = jnp.where(row == tt - 1, nxt0, pltpu.roll(dconv, tt - 1, 0))
            p2 = jnp.where(row == tt - 2, nxt0, jnp.where(row == tt - 1, nxt1, pltpu.roll(dconv, tt - 2, 0)))
            dhc = w_ref[2:3, :] * dconv + w_ref[1:2, :] * p1 + w_ref[0:1, :] * p2
            carry_scr[...] = dconv[0:HALO, :]
            dp_ref[...] = (dya * conv).astype(BF16)
            stash_scr[0] = (dhc * h).astype(BF16)
            stash_scr[1] = (dhc * c).astype(BF16)
            gw0 = jnp.sum(dconv * m2, axis=0, keepdims=True)
            gw1 = jnp.sum(dconv * m1, axis=0, keepdims=True)
            gw2 = jnp.sum(dconv * hc, axis=0, keepdims=True)

            @pl.when(first_tile)
            def _():
                gw_ref[0:1, :] = gw0
                gw_ref[1:2, :] = gw1
                gw_ref[2:3, :] = gw2
                gga_ref[...] = gg

            @pl.when(jnp.logical_not(first_tile))
            def _():
                gw_ref[0:1, :] += gw0
                gw_ref[1:2, :] += gw1
                gw_ref[2:3, :] += gw2
                gga_ref[...] += gg

        @pl.when(g == 1)
        def _():
            dp_ref[...] = stash_scr[0]

        @pl.when(g == 2)
        def _():
            dp_ref[...] = stash_scr[1]

        @pl.when(g == 3)
        def _():
            u = u_ref[...]
            v = v_ref[...]
            gu, thu = _gelu_parts(u)
            gv, thv = _gelu_parts(v)
            gvb = gv.astype(BF16)
            _spatial_fwd(gvb, sw_ref, sbt_ref, s_scr)
            sv = s_scr[...]
            yb = gu * sv
            inv = lax.rsqrt(jnp.mean(yb * yb, axis=-1, keepdims=True) + EPS)
            dyn = dyb_ref[...]
            gd = dyn * gb_ref[...]
            dot = jnp.mean(gd * yb, axis=-1, keepdims=True)
            dyb = inv * gd - yb * (inv * inv * inv * dot)
            gg = jnp.sum(dyn * yb * inv, axis=0, keepdims=True)
            ds = dyb * gu
            dsb = ds.astype(BF16)
            tri = _tril()

            @pl.when(first_tile)
            def _():
                ggb_ref[...] = gg
                dsum_scr[...] = ds
                gsw_ref[...] = jnp.zeros_like(gsw_ref)

            @pl.when(jnp.logical_not(first_tile))
            def _():
                ggb_ref[...] += gg
                dsum_scr[...] += ds

            for hd in range(n_head):
                sl = slice(hd * HEAD, (hd + 1) * HEAD)
                wm = jnp.where(tri, sw_ref[hd], 0.0).astype(BF16)
                t_scr[:, sl] = lax.dot_general(wm, dsb[:, sl], TN, preferred_element_type=F32)
                gsw_ref[hd] += lax.dot_general(dsb[:, sl], gvb[:, sl], NT, preferred_element_type=F32)
            dp_ref[...] = (dyb * sv * _gelu_grad(u, thu)).astype(BF16)
            stash_scr[0] = (t_scr[...] * _gelu_grad(v, thv)).astype(BF16)

            @pl.when(last_tile)
            def _():
                for hd in range(n_head):
                    sl = slice(hd * HEAD, (hd + 1) * HEAD)
                    gsw_ref[hd] = jnp.where(tri, gsw_ref[hd], 0.0)
                    gsb_ref[:, hd:hd + 1] = jnp.sum(dsum_scr[:, sl], axis=1, keepdims=True)

        @pl.when(g == 4)
        def _():
            dp_ref[...] = stash_scr[0]

    const2 = lambda shape: pl.BlockSpec(shape, lambda s, i, g: (0, 0))
    const3 = pl.BlockSpec((n_head, HEAD, HEAD), lambda s, i, g: (0, 0, 0))
    dy_spec = lambda col: pl.BlockSpec((tt, wc), lambda s, ir, g: (row_of(s, ir, g), col))
    return pl.pallas_call(
        body, name="mixer_bwd", grid=(n_seq, nt, 5),
        in_specs=[grp(0), grp(1), grp(2), grp(3), grp(4), halo(1), halo(2), dy_spec(0), dy_spec(1),
                  const2((CONV_K, wc)), const3, const2((HEAD, n_head)), const2((1, wc)), const2((1, wc))],
        out_specs=[pl.BlockSpec((tt, wc), lambda s, ir, g: (row_of(s, ir, g), g)),
                   const2((CONV_K, wc)), const2((1, wc)), const2((1, wc)), const3, const2((HEAD, n_head))],
        out_shape=[jax.ShapeDtypeStruct((T, 5 * wc), BF16), jax.ShapeDtypeStruct((CONV_K, wc), F32),
                   jax.ShapeDtypeStruct((1, wc), F32), jax.ShapeDtypeStruct((1, wc), F32),
                   jax.ShapeDtypeStruct((n_head, HEAD, HEAD), F32), jax.ShapeDtypeStruct((HEAD, n_head), F32)],
        scratch_shapes=[pltpu.VMEM((HALO, wc), F32), pltpu.VMEM((2, tt, wc), BF16), pltpu.VMEM((tt, wc), F32),
                        pltpu.VMEM((tt, wc), F32), pltpu.VMEM((tt, wc), F32)],
        compiler_params=_params(("arbitrary", "arbitrary", "arbitrary")),
    )(proj, proj, proj, proj, proj, proj, proj, dy, dy, conv_w, sw, sbt, g_a, g_b)


def _place():
    return lax.axis_index("x"), lax.axis_index("y"), lax.axis_index("c")


def _other_chips(x, y):
    return [(1 - x, y), (x, 1 - y), (1 - x, 1 - y)]


def _all_gather(blk, name):
    def body(x_ref, out_ref, send_sems, recv_sems, local_sem):
        x, y, c = _place()
        me, sibling = (x, y, c), (x, y, 1 - c)
        chips = _other_chips(x, y)

        def slot(px, py, pc):
            return out_ref.at[4 * px + 2 * py + pc]

        def copy(k, block, to, src=None):
            return pltpu.make_async_remote_copy(
                src_ref=slot(*block) if src is None else src, dst_ref=slot(*block),
                send_sem=send_sems.at[k], recv_sem=recv_sems.at[k], device_id=to, device_id_type=MESH)

        mine = pltpu.make_async_copy(x_ref, slot(*me), local_sem)
        mine.start()
        first = [copy(0, me, sibling, src=x_ref)]
        first += [copy(1 + j, me, (*chip, c), src=x_ref) for j, chip in enumerate(chips)]
        for cp in first:
            cp.start()
        passed = [copy(4 + j, (*chip, c), sibling) for j, chip in enumerate(chips)]
        for j, chip in enumerate(chips):
            copy(1 + j, (*chip, c), me).wait_recv()
            passed[j].start()
        copy(0, sibling, me).wait_recv()
        for j, chip in enumerate(chips):
            copy(4 + j, (*chip, 1 - c), me).wait_recv()
        for cp in first + passed:
            cp.wait_send()
        mine.wait()

    return pl.pallas_call(
        body, name=name, in_specs=[pl.BlockSpec(memory_space=HBM)], out_specs=pl.BlockSpec(memory_space=HBM),
        out_shape=jax.ShapeDtypeStruct((N_DEV,) + blk.shape, blk.dtype),
        scratch_shapes=[pltpu.SemaphoreType.DMA((7,)), pltpu.SemaphoreType.DMA((7,)), pltpu.SemaphoreType.DMA],
    )(blk)


def _all_gather_async(blk, name, collective_id, side_by_side=False):
    cols = blk.shape[1] if side_by_side else None
    out_shape = (blk.shape[0], N_DEV * cols) if side_by_side else (N_DEV,) + blk.shape

    def body(x_ref, out_ref, send_sems, recv_sems, local_sem):
        x, y, c = _place()
        me, sibling = (x, y, c), (x, y, 1 - c)
        chips = _other_chips(x, y)
        _handshake([sibling] + [(*chip, c) for chip in chips])

        def slot(px, py, pc):
            return _owner_block(out_ref, 4 * px + 2 * py + pc, cols)

        def copy(k, block, to, src=None):
            return pltpu.make_async_remote_copy(
                src_ref=slot(*block) if src is None else src, dst_ref=slot(*block),
                send_sem=send_sems.at[k], recv_sem=recv_sems.at[k], device_id=to, device_id_type=MESH)

        mine = pltpu.make_async_copy(x_ref, slot(*me), local_sem)
        mine.start()
        first = [copy(0, me, sibling, src=x_ref)]
        first += [copy(1 + j, me, (*chip, c), src=x_ref) for j, chip in enumerate(chips)]
        for cp in first:
            cp.start()
        passed = [copy(4 + j, (*chip, c), sibling) for j, chip in enumerate(chips)]
        for j, chip in enumerate(chips):
            copy(1 + j, (*chip, c), me).wait_recv()
            passed[j].start()
        copy(0, sibling, me).wait_recv()
        for j, chip in enumerate(chips):
            copy(4 + j, (*chip, 1 - c), me).wait_recv()
        for cp in first + passed:
            cp.wait_send()
        mine.wait()

    return _sequencer_call(
        body, name, collective_id, jax.ShapeDtypeStruct(out_shape, blk.dtype),
        [pltpu.SemaphoreType.DMA((7,)), pltpu.SemaphoreType.DMA((7,)), pltpu.SemaphoreType.DMA], blk)


def _sequencer_call(body, name, collective_id, out_type, scratch_types, operand):
    return pl.kernel(
        body, name=name, out_type=out_type, mesh=plsc.ScalarSubcoreMesh(axis_name="seq_core", num_cores=1),
        scratch_types=scratch_types, compiler_params=pltpu.CompilerParams(collective_id=collective_id),
    )(operand)


def _handshake(peers):
    barrier = pltpu.get_barrier_semaphore()
    for peer in peers:
        pl.semaphore_signal(barrier, inc=1, device_id=peer, device_id_type=MESH)
    pl.semaphore_wait(barrier, len(peers))


def _add_sibling(parts, got, c_idx, name):
    _, R, C = got.shape
    tr, tc = _tile(R, 512), _tile(C, 1024)

    def body(c_ref, p_ref, g_ref, o_ref):
        o_ref[...] = (p_ref[...].astype(F32) + g_ref[...].astype(F32)).astype(o_ref.dtype)

    if parts.ndim == 3:
        parts_spec = pl.BlockSpec((None, tr, tc), lambda k, i, j, c_ref: (2 * k + c_ref[0], i, j))
    else:
        parts_spec = pl.BlockSpec((tr, tc), lambda k, i, j, c_ref: (i, (2 * k + c_ref[0]) * (C // tc) + j))
    grid_spec = pltpu.PrefetchScalarGridSpec(
        num_scalar_prefetch=1, grid=(N_CHIP, R // tr, C // tc),
        in_specs=[parts_spec, pl.BlockSpec((None, tr, tc), lambda k, i, j, c_ref: (k, i, j))],
        out_specs=pl.BlockSpec((None, tr, tc), lambda k, i, j, c_ref: (k, i, j)))
    return pl.pallas_call(
        body, name=name, grid_spec=grid_spec, out_shape=jax.ShapeDtypeStruct((N_CHIP, R, C), parts.dtype),
        compiler_params=_params(("parallel", "parallel", "parallel")),
    )(c_idx, parts, got)


def _scatter_to_chips(sums, name, collective_id):
    _, R, C = sums.shape

    def body(q_ref, got_ref, send_sems, recv_sems, local_sem):
        x, y, c = _place()
        _handshake([(*chip, c) for chip in _other_chips(x, y)])
        my_chip = 2 * x + y
        mine = pltpu.make_async_copy(q_ref.at[my_chip], got_ref.at[my_chip], local_sem)
        mine.start()
        copies = [pltpu.make_async_remote_copy(
            src_ref=q_ref.at[2 * px + py], dst_ref=got_ref.at[my_chip], send_sem=send_sems.at[j],
            recv_sem=recv_sems.at[j], device_id=(px, py, c), device_id_type=MESH)
            for j, (px, py) in enumerate(_other_chips(x, y))]
        for cp in copies:
            cp.start()
        for cp in copies:
            cp.wait()
        mine.wait()

    return _sequencer_call(
        body, name, collective_id, jax.ShapeDtypeStruct((N_CHIP, R, C), sums.dtype),
        [pltpu.SemaphoreType.DMA((3,)), pltpu.SemaphoreType.DMA((3,)), pltpu.SemaphoreType.DMA], sums)


def _adamw_math(w, g, m, v):
    m = ADAM_B1 * m + (1.0 - ADAM_B1) * g
    v = ADAM_B2 * v + (1.0 - ADAM_B2) * (g * g)
    m_hat = m / (1.0 - ADAM_B1 ** ADAM_STEP)
    v_hat = v / (1.0 - ADAM_B2 ** ADAM_STEP)
    delta = -ADAM_LR * (m_hat / (jnp.sqrt(v_hat) + ADAM_EPS) + ADAM_WD * w)
    return delta, m, v


def _sum_adamw(parts, w, m, v, name):
    n_parts, R, C = parts.shape
    tr, tc = _tile(R, 256), _tile(C, 1024)

    def body(p_ref, w_ref, m_ref, v_ref, g_out, d_out, m_out, v_out):
        g = p_ref[0].astype(F32)
        for k in range(1, n_parts):
            g = g + p_ref[k].astype(F32)
        delta, mn, vn = _adamw_math(w_ref[...], g, m_ref[...], v_ref[...])
        g_out[...] = g
        d_out[...] = delta
        m_out[...] = mn
        v_out[...] = vn

    blk = pl.BlockSpec((tr, tc), lambda i, j: (i, j))
    shp = jax.ShapeDtypeStruct((R, C), F32)
    return pl.pallas_call(
        body, name=name, grid=(R // tr, C // tc),
        in_specs=[pl.BlockSpec((n_parts, tr, tc), lambda i, j: (0, i, j)), blk, blk, blk],
        out_specs=[blk, blk, blk, blk], out_shape=[shp, shp, shp, shp],
        compiler_params=_params(("parallel", "parallel")),
    )(parts, w, m, v)


def _after(value, dep):
    return lax.optimization_barrier((value, dep))[0]


def _rows128(a):
    return a.reshape(-1, 128)


def kernel(x, mix_norm_g, w_in, conv_w, spatial_w, spatial_b, conv_out_norm_g, gmlp_out_norm_g, w_out, mlp_norm_g, w_up, w_down, final_norm_g, loss_target, m_mix_norm_g, m_w_in, m_conv_w, m_spatial_w, m_spatial_b, m_conv_out_norm_g, m_gmlp_out_norm_g, m_w_out, m_mlp_norm_g, m_w_up, m_w_down, m_final_norm_g, v_mix_norm_g, v_w_in, v_conv_w, v_spatial_w, v_spatial_b, v_conv_out_norm_g, v_gmlp_out_norm_g, v_w_out, v_mlp_norm_g, v_w_up, v_w_down, v_final_norm_g):
    n_seq, seq, D = x.shape
    T = n_seq * seq
    n_in = w_in.shape[2]
    n_out = w_out.shape[1]
    n_up = w_up.shape[2]
    wc = conv_w.shape[2] * N_DEV
    n_head = wc // HEAD
    FF = n_up * N_DEV
    assert N_DEV * n_in == 5 * wc and seq % HEAD == 0 and D == 2 * wc

    c_idx = lax.axis_index("c").astype(jnp.int32).reshape(1)
    my_dev = 4 * lax.axis_index("x") + 2 * lax.axis_index("y") + lax.axis_index("c")

    xf = x.reshape(T, D)
    tgt = loss_target.reshape(T, D)

    cw_pad = jnp.pad(conv_w[0], ((0, HALO - CONV_K), (0, 0)))
    cw_g = _all_gather(cw_pad, "ag_conv_w")
    conv_full = jnp.transpose(cw_g[:, :CONV_K, :], (1, 0, 2)).reshape(CONV_K, wc)
    cast = lambda w: _after(w[0].astype(BF16), cw_g)
    win_g = _all_gather_async(cast(w_in), "ag_w_in", 0, side_by_side=True)
    wout_g = _all_gather_async(cast(w_out), "ag_w_out", 1).reshape(D, D)
    wup_g = _all_gather_async(cast(w_up), "ag_w_up", 2, side_by_side=True)
    wdown_g = _all_gather_async(cast(w_down), "ag_w_down", 3).reshape(FF, D)

    sw = spatial_w[0]
    sbt = spatial_b[0].T
    g_mix, g_a, g_b, g_mlp = mix_norm_g, conv_out_norm_g, gmlp_out_norm_g, mlp_norm_g
    g_fin = final_norm_g.reshape(1, D)

    IN = N_DEV * n_in
    bp = _tile(T, 512)
    bm = _tile(T, 1024)
    bn = _tile(D, 1024)
    bu = _tile(FF, 1024)
    bw = _tile(D, 512)
    k_ff = _tile(FF, 4096)
    k_in = _tile(IN, 5120)

    def tile(shape, index):
        return pl.BlockSpec(shape, index)

    by_m0 = lambda n, m, k: (m, 0)
    by_0n = lambda n, m, k: (0, n)
    by_n0 = lambda n, m, k: (n, 0)
    by_mn = lambda n, m, k: (m, n)
    by_mk = lambda n, m, k: (m, k)
    by_kn = lambda n, m, k: (k, n)
    by_nk = lambda n, m, k: (n, k)
    by_0m = lambda n, m, k: (0, m)
    f32_td = [jax.ShapeDtypeStruct((T, D), F32)]

    xn = _rms_fwd(xf, g_mix, "norm_mix")
    proj = _matmul(
        "proj", (N_DEV, T // bp, 1), NN, [xn, win_g], [tile((bp, D), by_m0), tile((D, n_in), by_0n)],
        [jax.ShapeDtypeStruct((T, IN), F32)], [tile((bp, n_in), by_mn)], _ident)[0]
    y = _mixer_fwd(proj, conv_full, sw, sbt, g_a, g_b, n_seq, seq)
    h1 = _matmul(
        "out_proj", (D // bn, T // bp, 1), NN, [y, wout_g, xf],
        [tile((bp, D), by_m0), tile((D, bn), by_0n), tile((bp, bn), by_mn)],
        f32_td, [tile((bp, bn), by_mn)], lambda acc, res: (res + acc,))[0]
    xn2 = _rms_fwd(h1, g_mlp, "norm_mlp")

    def up_epilogue(acc):
        r = jnp.maximum(acc, 0.0)
        return acc, r * r

    up, act = _matmul(
        "up_proj", (FF // bu, T // bm, 1), NN, [xn2, wup_g], [tile((bm, D), by_m0), tile((D, bu), by_0n)],
        [jax.ShapeDtypeStruct((T, FF), BF16)] * 2, [tile((bm, bu), by_mn)] * 2, up_epilogue)
    mlp = _matmul(
        "down_proj", (D // bn, T // bm, FF // k_ff), NN, [act, wdown_g],
        [tile((bm, k_ff), by_mk), tile((k_ff, bn), by_kn)], f32_td, [tile((bm, bn), by_mn)], _ident)[0]

    dh2, dh2b, gg_fin, loss_row = _loss_head(h1, mlp, tgt, g_fin)
    loss = lax.psum(loss_row[0, 0], ("x", "y", "c"))

    dup = _matmul(
        "d_act", (FF // bu, T // bm, 1), NT, [dh2b, wdown_g, up],
        [tile((bm, D), by_m0), tile((bu, D), by_n0), tile((bm, bu), by_mn)],
        [jax.ShapeDtypeStruct((T, FF), BF16)], [tile((bm, bu), by_mn)],
        lambda acc, u: (acc * (2.0 * jnp.maximum(u.astype(F32), 0.0)),))[0]
    gp_down = _matmul(
        "gw_down", (D // bn, FF // bw, 1), TN, [act, dh2b], [tile((T, bw), by_0m), tile((T, bn), by_0n)],
        [jax.ShapeDtypeStruct((FF, D), BF16)], [tile((bw, bn), by_mn)], _ident)[0].reshape(N_DEV, n_up, D)
    dxn2, got_down = _matmul(
        "d_xn2", (D // bn, T // bm, FF // k_ff), NT, [dup, wup_g],
        [tile((bm, k_ff), by_mk), tile((bn, k_ff), by_nk)], f32_td, [tile((bm, bn), by_mn)], _ident,
        swap=(gp_down, None))
    gp_up = _matmul(
        "gw_up", (FF // bu, D // bw, 1), TN, [xn2, _after(dup, dxn2)],
        [tile((T, bw), by_0m), tile((T, bu), by_0n)],
        [jax.ShapeDtypeStruct((D, FF), BF16)], [tile((bw, bu), by_mn)], _ident)[0]
    sums_down = _add_sibling(gp_down, _after(got_down, gp_up), c_idx, "rs_add_w_down")
    four_down = _scatter_to_chips(sums_down, "rs_chips_w_down", 4)
    dh1, dh1b, gg_mlp = _rms_bwd(_after(dxn2, sums_down), h1, g_mlp, dh2, "norm_mlp_bwd", True)

    dy, got_up = _matmul(
        "d_y", (D // bn, T // bm, 1), NT, [dh1b, wout_g], [tile((bm, D), by_m0), tile((bn, D), by_n0)],
        f32_td, [tile((bm, bn), by_mn)], _ident, swap=(gp_up, n_up))
    gp_out = _matmul(
        "gw_out", (D // bn, D // bw, 1), TN, [y, _after(dh1b, dy)], [tile((T, bw), by_0m), tile((T, bn), by_0n)],
        [jax.ShapeDtypeStruct((D, D), BF16)], [tile((bw, bn), by_mn)], _ident)[0].reshape(N_DEV, n_out, D)
    sums_up = _add_sibling(gp_up, _after(got_up, gp_out), c_idx, "rs_add_w_up")
    four_up = _scatter_to_chips(sums_up, "rs_chips_w_up", 5)
    dproj, gl_conv, gl_a, gl_b, gl_sw, gl_sbt = _mixer_bwd(
        proj, _after(dy, sums_up), conv_full, sw, sbt, g_a, g_b, n_seq, seq)
    gp_in, got_out = _matmul(
        "gw_in", (N_DEV, D // bw, 1), TN, [xn, dproj], [tile((T, bw), by_0m), tile((T, n_in), by_0n)],
        [jax.ShapeDtypeStruct((D, IN), BF16)], [tile((bw, n_in), by_mn)], _ident, swap=(gp_out, None))
    sums_out = _add_sibling(gp_out, got_out, c_idx, "rs_add_w_out")
    four_out = _scatter_to_chips(sums_out, "rs_chips_w_out", 6)
    dxn, got_in = _matmul(
        "d_xn", (D // bn, T // bm, IN // k_in), NT, [_after(dproj, sums_out), win_g],
        [tile((bm, k_in), by_mk), tile((bn, k_in), by_nk)], f32_td, [tile((bm, bn), by_mn)], _ident,
        swap=(gp_in, n_in))
    sums_in = _add_sibling(gp_in, got_in, c_idx, "rs_add_w_in")
    four_in = _scatter_to_chips(sums_in, "rs_chips_w_in", 7)
    grad_x, gg_mix = _rms_bwd(_after(dxn, sums_in), xf, g_mix, dh1, "norm_mix_bwd", False)

    outs = {}
    done = grad_x
    for tag, four, w, m, v in (("w_down", four_down, w_down, m_w_down, v_w_down),
                               ("w_up", four_up, w_up, m_w_up, v_w_up),
                               ("w_out", four_out, w_out, m_w_out, v_w_out),
                               ("w_in", four_in, w_in, m_w_in, v_w_in)):
        res = _sum_adamw(_after(four, done), w[0], m[0], v[0], "adamw_" + tag)
        done = res[0]
        outs[tag] = [a[None] for a in res]

    small = [("mix_norm_g", gg_mix, mix_norm_g, m_mix_norm_g, v_mix_norm_g),
             ("conv_w", gl_conv, None, None, None),
             ("spatial_w", gl_sw, spatial_w, m_spatial_w, v_spatial_w),
             ("spatial_b", gl_sbt.T, spatial_b, m_spatial_b, v_spatial_b),
             ("conv_out_norm_g", gl_a, conv_out_norm_g, m_conv_out_norm_g, v_conv_out_norm_g),
             ("gmlp_out_norm_g", gl_b, gmlp_out_norm_g, m_gmlp_out_norm_g, v_gmlp_out_norm_g),
             ("mlp_norm_g", gg_mlp, mlp_norm_g, m_mlp_norm_g, v_mlp_norm_g),
             ("final_norm_g", gg_fin, final_norm_g, m_final_norm_g, v_final_norm_g)]
    packed_g = jnp.concatenate([_rows128(g) for _, g, _, _, _ in small], axis=0)
    zeros_cw = jnp.zeros((CONV_K * wc // 128, 128), F32)
    pack = lambda idx: jnp.concatenate(
        [zeros_cw if item[2] is None else _rows128(item[idx]) for item in small], axis=0)
    all_g = _all_gather_async(packed_g, "ag_small_grads", 8)
    sg, sd, sm, sv = _sum_adamw(_after(all_g, done), pack(2), pack(3), pack(4), "adamw_small")
    row = 0
    for name, g, w, _, _ in small:
        n_rows = g.size // 128
        if w is not None:
            outs[name] = [a[row:row + n_rows].reshape(w.shape) for a in (sg, sd, sm, sv)]
        else:
            conv_grad_full = sg[row:row + n_rows].reshape(CONV_K, wc)
        row += n_rows
    cpd = wc // N_DEV
    conv_grad = lax.dynamic_slice(conv_grad_full, (0, my_dev * cpd), (CONV_K, cpd))
    pad8 = lambda a: jnp.pad(a, ((0, HALO - CONV_K), (0, 0)))
    outs["conv_w"] = [a[:CONV_K][None] for a in _sum_adamw(
        pad8(conv_grad)[None], pad8(conv_w[0]), pad8(m_conv_w[0]), pad8(v_conv_w[0]), "adamw_conv_w")]

    order = ["mix_norm_g", "w_in", "conv_w", "spatial_w", "spatial_b", "conv_out_norm_g", "gmlp_out_norm_g",
             "w_out", "mlp_norm_g", "w_up", "w_down", "final_norm_g"]
    result = [loss, grad_x.reshape(n_seq, seq, D)]
    for k in range(4):
        result += [outs[n][k] for n in order]
    return tuple(result)
```

```python
import functools
import math

import jax
import jax.numpy as jnp
from jax import lax
from jax.experimental import pallas as pl
from jax.experimental.pallas import tpu as pltpu
from jax.experimental.pallas import tpu_sc as plsc

F32 = jnp.float32
BF16 = jnp.bfloat16
MESH = pl.DeviceIdType.MESH
HBM = pltpu.HBM

EPS = 1e-5
HEAD = 128
CONV_K = 3
N_DEV = 8
N_CHIP = 4
VMEM_LIMIT_BYTES = 62 * 1024 * 1024

ADAM_LR = 0.001
ADAM_B1 = 0.9
ADAM_B2 = 0.999
ADAM_EPS = 1e-08
ADAM_WD = 0.01
ADAM_STEP = 10

GELU_K0 = math.sqrt(2.0 / math.pi)
GELU_K1 = 0.044715

NN = (((1,), (0,)), ((), ()))
NT = (((1,), (1,)), ((), ()))
TN = (((0,), (0,)), ((), ()))


def _params(semantics):
    return pltpu.CompilerParams(dimension_semantics=semantics, vmem_limit_bytes=VMEM_LIMIT_BYTES)


def _tile(dim, want):
    if dim <= want:
        return dim
    for t in range(want - want % 8, 0, -8):
        if dim % t == 0:
            return t
    raise ValueError((dim, want))


def _owner_block(ref, j, cols):
    if cols is None:
        return ref.at[j]
    return ref.at[:, pl.ds(pl.multiple_of(j * cols, 128), cols)]


def _sibling_copies(p_ref, got_ref, send_sems, recv_sems, cols):
    x, y, c = _place()
    return [pltpu.make_async_remote_copy(
        src_ref=_owner_block(p_ref, 2 * k + (1 - c), cols), dst_ref=got_ref.at[k], send_sem=send_sems.at[k],
        recv_sem=recv_sems.at[k], device_id=(x, y, 1 - c), device_id_type=MESH) for k in range(N_CHIP)]


def _matmul(name, grid, dims, operands, in_specs, out_shapes, out_specs, epilogue, swap=None):
    n_in = len(operands)
    n_out = len(out_shapes)
    nk = grid[2]
    n_host = 0 if swap is None else 1
    assert nk == 1 or (n_out == 1 and epilogue is _ident and out_shapes[0].dtype == F32)

    def body(*refs):
        a_ref, b_ref = refs[0], refs[1]
        extra = refs[2:n_in]
        outs = refs[n_in + n_host:n_in + n_host + n_out]
        if swap is not None:
            ids = [pl.program_id(d) for d in range(3)]
            copies = _sibling_copies(refs[n_in], refs[n_in + 1 + n_out], refs[-2], refs[-1], swap[1])

            @pl.when(functools.reduce(jnp.logical_and, [i == 0 for i in ids]))
            def _():
                for cp in copies:
                    cp.start()

        part = lax.dot_general(a_ref[...], b_ref[...], dims, preferred_element_type=F32)

        def finish(acc):
            res = epilogue(acc, *[e[...] for e in extra])
            for o, r in zip(outs, res):
                o[...] = r.astype(o.dtype)

        if nk == 1:
            finish(part)
        else:
            k = pl.program_id(2)

            @pl.when(k == 0)
            def _():
                outs[0][...] = part

            @pl.when(k > 0)
            def _():
                outs[0][...] += part

        if swap is not None:
            @pl.when(functools.reduce(jnp.logical_and, [i == n - 1 for i, n in zip(ids, grid)]))
            def _():
                for cp in copies:
                    cp.wait()

    scratch = []
    semantics = ("parallel", "parallel", "arbitrary")
    if swap is not None:
        parts, cols = swap
        got_shape = parts.shape[1:] if cols is None else (parts.shape[0], cols)
        hbm = pl.BlockSpec(memory_space=HBM)
        operands, in_specs = list(operands) + [parts], list(in_specs) + [hbm]
        out_shapes = list(out_shapes) + [jax.ShapeDtypeStruct((N_CHIP,) + got_shape, parts.dtype)]
        out_specs = list(out_specs) + [hbm]
        scratch += [pltpu.SemaphoreType.DMA((N_CHIP,)), pltpu.SemaphoreType.DMA((N_CHIP,))]
        semantics = ("arbitrary", "arbitrary", "arbitrary")
    return pl.pallas_call(
        body, name=name, grid=grid, in_specs=in_specs, out_specs=out_specs, out_shape=out_shapes,
        scratch_shapes=scratch, compiler_params=_params(semantics),
    )(*operands)


def _ident(acc):
    return (acc,)


ROW_TILE = 256


def _rms_fwd(x, g, name):
    T, D = x.shape
    tr = _tile(T, ROW_TILE)

    def body(x_ref, g_ref, o_ref):
        xv = x_ref[...]
        inv = lax.rsqrt(jnp.mean(xv * xv, axis=-1, keepdims=True) + EPS)
        o_ref[...] = (xv * inv * g_ref[...]).astype(o_ref.dtype)

    return pl.pallas_call(
        body, name=name, grid=(T // tr,),
        in_specs=[pl.BlockSpec((tr, D), lambda i: (i, 0)), pl.BlockSpec((1, D), lambda i: (0, 0))],
        out_specs=pl.BlockSpec((tr, D), lambda i: (i, 0)),
        out_shape=jax.ShapeDtypeStruct((T, D), BF16),
        compiler_params=_params(("parallel",)),
    )(x, g)


def _rms_bwd(dy, x, g, dres, name, want_bf16):
    T, D = x.shape
    tr = _tile(T, ROW_TILE)

    def body(dy_ref, x_ref, g_ref, dres_ref, *outs):
        dx_ref, gg_ref = outs[0], outs[-1]
        i = pl.program_id(0)
        xv = x_ref[...]
        dyv = dy_ref[...]
        inv = lax.rsqrt(jnp.mean(xv * xv, axis=-1, keepdims=True) + EPS)
        gd = dyv * g_ref[...]
        dot = jnp.mean(gd * xv, axis=-1, keepdims=True)
        dx = dres_ref[...] + (inv * gd - xv * (inv * inv * inv * dot))
        dx_ref[...] = dx
        if want_bf16:
            outs[1][...] = dx.astype(BF16)
        part = jnp.sum(dyv * xv * inv, axis=0, keepdims=True)

        @pl.when(i == 0)
        def _():
            gg_ref[...] = part

        @pl.when(i > 0)
        def _():
            gg_ref[...] += part

    row = pl.BlockSpec((tr, D), lambda i: (i, 0))
    vec = pl.BlockSpec((1, D), lambda i: (0, 0))
    out_shape = [jax.ShapeDtypeStruct((T, D), F32)]
    out_specs = [row]
    if want_bf16:
        out_shape.append(jax.ShapeDtypeStruct((T, D), BF16))
        out_specs.append(row)
    out_shape.append(jax.ShapeDtypeStruct((1, D), F32))
    out_specs.append(vec)
    return pl.pallas_call(
        body, name=name, grid=(T // tr,), in_specs=[row, row, vec, row],
        out_specs=out_specs, out_shape=out_shape, compiler_params=_params(("arbitrary",)),
    )(dy, x, g, dres)


def _loss_head(h1, mlp, target, g):
    T, D = h1.shape
    tr = _tile(T, ROW_TILE)

    def body(h1_ref, mlp_ref, t_ref, g_ref, dh_ref, dhb_ref, gg_ref, loss_ref):
        i = pl.program_id(0)
        hv = h1_ref[...] + mlp_ref[...]
        gv = g_ref[...]
        inv = lax.rsqrt(jnp.mean(hv * hv, axis=-1, keepdims=True) + EPS)
        diff = hv * inv * gv - t_ref[...]
        lpart = 0.5 * jnp.sum(jnp.mean(diff * diff, axis=-1, keepdims=True), axis=0, keepdims=True)
        dout = diff * (1.0 / D)
        gd = dout * gv
        dot = jnp.mean(gd * hv, axis=-1, keepdims=True)
        dh = inv * gd - hv * (inv * inv * inv * dot)
        dh_ref[...] = dh
        dhb_ref[...] = dh.astype(BF16)
        part = jnp.sum(dout * hv * inv, axis=0, keepdims=True)
        lrow = jnp.broadcast_to(lpart, (1, 128))

        @pl.when(i == 0)
        def _():
            gg_ref[...] = part
            loss_ref[...] = lrow

        @pl.when(i > 0)
        def _():
            gg_ref[...] += part
            loss_ref[...] += lrow

    row = pl.BlockSpec((tr, D), lambda i: (i, 0))
    vec = pl.BlockSpec((1, D), lambda i: (0, 0))
    return pl.pallas_call(
        body, name="loss_head", grid=(T // tr,), in_specs=[row, row, row, vec],
        out_specs=[row, row, vec, pl.BlockSpec((1, 128), lambda i: (0, 0))],
        out_shape=[jax.ShapeDtypeStruct((T, D), F32), jax.ShapeDtypeStruct((T, D), BF16),
                   jax.ShapeDtypeStruct((1, D), F32), jax.ShapeDtypeStruct((1, 128), F32)],
        compiler_params=_params(("arbitrary",)),
    )(h1, mlp, target, g)


HALO = 8


def _gelu_parts(x):
    th = jnp.tanh(GELU_K0 * (x + GELU_K1 * (x * x * x)))
    return x * (0.5 * (1.0 + th)), th


def _gelu_grad(x, th):
    return 0.5 * (1.0 + th) + (0.5 * GELU_K0) * x * (1.0 - th * th) * (1.0 + (3.0 * GELU_K1) * (x * x))


def _conv_fwd(b_ref, c_ref, h_ref, ch_ref, hh_ref, w_ref, first):
    tt, wc = c_ref.shape
    c = c_ref[...]
    h = h_ref[...]
    hc = c * h
    prev1 = jnp.where(first, 0.0, ch_ref[HALO - 1:HALO, :] * hh_ref[HALO - 1:HALO, :])
    prev2 = jnp.where(first, 0.0, ch_ref[HALO - 2:HALO - 1, :] * hh_ref[HALO - 2:HALO - 1, :])
    row = lax.broadcasted_iota(jnp.int32, (tt, wc), 0)
    m1 = jnp.where(row == 0, prev1, pltpu.roll(hc, 1, 0))
    m2 = jnp.where(row == 0, prev2, jnp.where(row == 1, prev1, pltpu.roll(hc, 2, 0)))
    conv = w_ref[0:1, :] * m2 + w_ref[1:2, :] * m1 + w_ref[2:3, :] * hc
    return c, h, hc, m1, m2, conv, b_ref[...] * conv


def _tril():
    r = lax.broadcasted_iota(jnp.int32, (HEAD, HEAD), 0)
    s = lax.broadcasted_iota(jnp.int32, (HEAD, HEAD), 1)
    return r >= s


def _spatial_fwd(gvb, sw_ref, sbt_ref, s_scr):
    n_head = sw_ref.shape[0]
    tri = _tril()
    for hd in range(n_head):
        sl = slice(hd * HEAD, (hd + 1) * HEAD)
        wm = jnp.where(tri, sw_ref[hd], 0.0).astype(BF16)
        s_scr[:, sl] = jnp.dot(wm, gvb[:, sl], preferred_element_type=F32) + sbt_ref[:, hd:hd + 1]


def _mixer_specs(n_tiles, wc, row_of, n_grid):
    def grp(g):
        return pl.BlockSpec((HEAD, wc), lambda *ids: (row_of(*ids), g))

    def halo(g):
        return pl.BlockSpec((HALO, wc), lambda *ids: (jnp.maximum(row_of(*ids) * (HEAD // HALO) - 1, 0), g))

    return grp, halo


def _mixer_fwd(proj, conv_w, sw, sbt, g_a, g_b, n_seq, seq):
    T, w5 = proj.shape
    wc = w5 // 5
    n_head = wc // HEAD
    nt = seq // HEAD

    def row_of(s, i, g):
        return s * nt + i

    grp, halo = _mixer_specs(nt, wc, row_of, 3)

    def body(b_ref, c_ref, h_ref, u_ref, v_ref, ch_ref, hh_ref, w_ref, sw_ref, sbt_ref, ga_ref, gb_ref,
             y_ref, s_scr):
        i = pl.program_id(1)
        g = pl.program_id(2)

        @pl.when(g == 0)
        def _():
            ya = _conv_fwd(b_ref, c_ref, h_ref, ch_ref, hh_ref, w_ref, i == 0)[-1]
            inv = lax.rsqrt(jnp.mean(ya * ya, axis=-1, keepdims=True) + EPS)
            y_ref[...] = (ya * inv * ga_ref[...]).astype(BF16)

        @pl.when(g == 1)
        def _():
            gu, _ = _gelu_parts(u_ref[...])
            gv, _ = _gelu_parts(v_ref[...])
            _spatial_fwd(gv.astype(BF16), sw_ref, sbt_ref, s_scr)
            yb = gu * s_scr[...]
            inv = lax.rsqrt(jnp.mean(yb * yb, axis=-1, keepdims=True) + EPS)
            y_ref[...] = (yb * inv * gb_ref[...]).astype(BF16)

    const2 = lambda shape: pl.BlockSpec(shape, lambda s, i, g: (0, 0))
    return pl.pallas_call(
        body, name="mixer_fwd", grid=(n_seq, nt, 2),
        in_specs=[grp(0), grp(1), grp(2), grp(3), grp(4), halo(1), halo(2),
                  const2((CONV_K, wc)), pl.BlockSpec((n_head, HEAD, HEAD), lambda s, i, g: (0, 0, 0)),
                  const2((HEAD, n_head)), const2((1, wc)), const2((1, wc))],
        out_specs=pl.BlockSpec((HEAD, wc), lambda s, i, g: (s * nt + i, g)),
        out_shape=jax.ShapeDtypeStruct((T, 2 * wc), BF16),
        scratch_shapes=[pltpu.VMEM((HEAD, wc), F32)],
        compiler_params=_params(("parallel", "parallel", "arbitrary")),
    )(proj, proj, proj, proj, proj, proj, proj, conv_w, sw, sbt, g_a, g_b)


def _mixer_bwd(proj, dy, conv_w, sw, sbt, g_a, g_b, n_seq, seq):
    T, w5 = proj.shape
    wc = w5 // 5
    n_head = wc // HEAD
    nt = seq // HEAD
    tt = HEAD

    def row_of(s, ir, g):
        return s * nt + (nt - 1 - ir)

    grp, halo = _mixer_specs(nt, wc, row_of, 3)

    def body(b_ref, c_ref, h_ref, u_ref, v_ref, ch_ref, hh_ref, dya_ref, dyb_ref, w_ref, sw_ref, sbt_ref,
             ga_ref, gb_ref, dp_ref, gw_ref, gga_ref, ggb_ref, gsw_ref, gsb_ref,
             carry_scr, stash_scr, s_scr, t_scr, dsum_scr):
        s_id = pl.program_id(0)
        ir = pl.program_id(1)
        g = pl.program_id(2)
        first_tile = jnp.logical_and(s_id == 0, ir == 0)
        last_tile = jnp.logical_and(s_id == n_seq - 1, ir == nt - 1)

        @pl.when(g == 0)
        def _():
            @pl.when(ir == 0)
            def _():
                carry_scr[...] = jnp.zeros_like(carry_scr)

            c, h, hc, m1, m2, conv, ya = _conv_fwd(b_ref, c_ref, h_ref, ch_ref, hh_ref, w_ref, ir == nt - 1)
            inv = lax.rsqrt(jnp.mean(ya * ya, axis=-1, keepdims=True) + EPS)
            dyn = dya_ref[...]
            gd = dyn * ga_ref[...]
            dot = jnp.mean(gd * ya, axis=-1, keepdims=True)
            dya = inv * gd - ya * (inv * inv * inv * dot)
            gg = jnp.sum(dyn * ya * inv, axis=0, keepdims=True)
            dconv = dya * b_ref[...]
            nxt0 = carry_scr[0:1, :]
            nxt1 = carry_scr[1:2, :]
            row = lax.broadcasted_iota(jnp.int32, (tt, wc), 0)
            p1 = jnp.where(row == tt - 1, nxt0, pltpu.roll(dconv, tt - 1, 0))
            p2 = jnp.where(row == tt - 2, nxt0, jnp.where(row == tt - 1, nxt1, pltpu.roll(dconv, tt - 2, 0)))
            dhc = w_ref[2:3, :] * dconv + w_ref[1:2, :] * p1 + w_ref[0:1, :] * p2
            carry_scr[...] = dconv[0:HALO, :]
            dp_ref[...] = (dya * conv).astype(BF16)
            stash_scr[0] = (dhc * h).astype(BF16)
            stash_scr[1] = (dhc * c).astype(BF16)
            gw0 = jnp.sum(dconv * m2, axis=0, keepdims=True)
            gw1 = jnp.sum(dconv * m1, axis=0, keepdims=True)
            gw2 = jnp.sum(dconv * hc, axis=0, keepdims=True)

            @pl.when(first_tile)
            def _():
                gw_ref[0:1, :] = gw0
                gw_ref[1:2, :] = gw1
                gw_ref[2:3, :] = gw2
                gga_ref[...] = gg

            @pl.when(jnp.logical_not(first_tile))
            def _():
                gw_ref[0:1, :] += gw0
                gw_ref[1:2, :] += gw1
                gw_ref[2:3, :] += gw2
                gga_ref[...] += gg

        @pl.when(g == 1)
        def _():
            dp_ref[...] = stash_scr[0]

        @pl.when(g == 2)
        def _():
            dp_ref[...] = stash_scr[1]

        @pl.when(g == 3)
        def _():
            u = u_ref[...]
            v = v_ref[...]
            gu, thu = _gelu_parts(u)
            gv, thv = _gelu_parts(v)
            gvb = gv.astype(BF16)
            _spatial_fwd(gvb, sw_ref, sbt_ref, s_scr)
            sv = s_scr[...]
            yb = gu * sv
            inv = lax.rsqrt(jnp.mean(yb * yb, axis=-1, keepdims=True) + EPS)
            dyn = dyb_ref[...]
            gd = dyn * gb_ref[...]
            dot = jnp.mean(gd * yb, axis=-1, keepdims=True)
            dyb = inv * gd - yb * (inv * inv * inv * dot)
            gg = jnp.sum(dyn * yb * inv, axis=0, keepdims=True)
            ds = dyb * gu
            dsb = ds.astype(BF16)
            tri = _tril()

            @pl.when(first_tile)
            def _():
                ggb_ref[...] = gg
                dsum_scr[...] = ds
                gsw_ref[...] = jnp.zeros_like(gsw_ref)

            @pl.when(jnp.logical_not(first_tile))
            def _():
                ggb_ref[...] += gg
                dsum_scr[...] += ds

            for hd in range(n_head):
                sl = slice(hd * HEAD, (hd + 1) * HEAD)
                wm = jnp.where(tri, sw_ref[hd], 0.0).astype(BF16)
                t_scr[:, sl] = lax.dot_general(wm, dsb[:, sl], TN, preferred_element_type=F32)
                gsw_ref[hd] += lax.dot_general(dsb[:, sl], gvb[:, sl], NT, preferred_element_type=F32)
            dp_ref[...] = (dyb * sv * _gelu_grad(u, thu)).astype(BF16)
            stash_scr[0] = (t_scr[...] * _gelu_grad(v, thv)).astype(BF16)

            @pl.when(last_tile)
            def _():
                for hd in range(n_head):
                    sl = slice(hd * HEAD, (hd + 1) * HEAD)
                    gsw_ref[hd] = jnp.where(tri, gsw_ref[hd], 0.0)
                    gsb_ref[:, hd:hd + 1] = jnp.sum(dsum_scr[:, sl], axis=1, keepdims=True)

        @pl.when(g == 4)
        def _():
            dp_ref[...] = stash_scr[0]

    const2 = lambda shape: pl.BlockSpec(shape, lambda s, i, g: (0, 0))
    const3 = pl.BlockSpec((n_head, HEAD, HEAD), lambda s, i, g: (0, 0, 0))
    dy_spec = lambda col: pl.BlockSpec((tt, wc), lambda s, ir, g: (row_of(s, ir, g), col))
    return pl.pallas_call(
        body, name="mixer_bwd", grid=(n_seq, nt, 5),
        in_specs=[grp(0), grp(1), grp(2), grp(3), grp(4), halo(1), halo(2), dy_spec(0), dy_spec(1),
                  const2((CONV_K, wc)), const3, const2((HEAD, n_head)), const2((1, wc)), const2((1, wc))],
        out_specs=[pl.BlockSpec((tt, wc), lambda s, ir, g: (row_of(s, ir, g), g)),
                   const2((CONV_K, wc)), const2((1, wc)), const2((1, wc)), const3, const2((HEAD, n_head))],
        out_shape=[jax.ShapeDtypeStruct((T, 5 * wc), BF16), jax.ShapeDtypeStruct((CONV_K, wc), F32),
                   jax.ShapeDtypeStruct((1, wc), F32), jax.ShapeDtypeStruct((1, wc), F32),
                   jax.ShapeDtypeStruct((n_head, HEAD, HEAD), F32), jax.ShapeDtypeStruct((HEAD, n_head), F32)],
        scratch_shapes=[pltpu.VMEM((HALO, wc), F32), pltpu.VMEM((2, tt, wc), BF16), pltpu.VMEM((tt, wc), F32),
                        pltpu.VMEM((tt, wc), F32), pltpu.VMEM((tt, wc), F32)],
        compiler_params=_params(("arbitrary", "arbitrary", "arbitrary")),
    )(proj, proj, proj, proj, proj, proj, proj, dy, dy, conv_w, sw, sbt, g_a, g_b)


def _place():
    return lax.axis_index("x"), lax.axis_index("y"), lax.axis_index("c")


def _other_chips(x, y):
    return [(1 - x, y), (x, 1 - y), (1 - x, 1 - y)]


def _all_gather(blk, name):
    def body(x_ref, out_ref, send_sems, recv_sems, local_sem):
        x, y, c = _place()
        me, sibling = (x, y, c), (x, y, 1 - c)
        chips = _other_chips(x, y)

        def slot(px, py, pc):
            return out_ref.at[4 * px + 2 * py + pc]

        def copy(k, block, to, src=None):
            return pltpu.make_async_remote_copy(
                src_ref=slot(*block) if src is None else src, dst_ref=slot(*block),
                send_sem=send_sems.at[k], recv_sem=recv_sems.at[k], device_id=to, device_id_type=MESH)

        mine = pltpu.make_async_copy(x_ref, slot(*me), local_sem)
        mine.start()
        first = [copy(0, me, sibling, src=x_ref)]
        first += [copy(1 + j, me, (*chip, c), src=x_ref) for j, chip in enumerate(chips)]
        for cp in first:
            cp.start()
        passed = [copy(4 + j, (*chip, c), sibling) for j, chip in enumerate(chips)]
        for j, chip in enumerate(chips):
            copy(1 + j, (*chip, c), me).wait_recv()
            passed[j].start()
        copy(0, sibling, me).wait_recv()
        for j, chip in enumerate(chips):
            copy(4 + j, (*chip, 1 - c), me).wait_recv()
        for cp in first + passed:
            cp.wait_send()
        mine.wait()

    return pl.pallas_call(
        body, name=name, in_specs=[pl.BlockSpec(memory_space=HBM)], out_specs=pl.BlockSpec(memory_space=HBM),
        out_shape=jax.ShapeDtypeStruct((N_DEV,) + blk.shape, blk.dtype),
        scratch_shapes=[pltpu.SemaphoreType.DMA((7,)), pltpu.SemaphoreType.DMA((7,)), pltpu.SemaphoreType.DMA],
    )(blk)


def _all_gather_async(blk, name, collective_id, side_by_side=False):
    cols = blk.shape[1] if side_by_side else None
    out_shape = (blk.shape[0], N_DEV * cols) if side_by_side else (N_DEV,) + blk.shape

    def body(x_ref, out_ref, send_sems, recv_sems, local_sem):
        x, y, c = _place()
        me, sibling = (x, y, c), (x, y, 1 - c)
        x_nbr, y_nbr, diagonal = (1 - x, y, c), (x, 1 - y, c), (1 - x, 1 - y, c)
        near = (c * x + (1 - c) * (1 - x), c * (1 - y) + (1 - c) * y, c)
        far = ((1 - c) * x + c * (1 - x), (1 - c) * (1 - y) + c * y, c)
        _handshake([sibling, x_nbr, y_nbr])

        def slot(px, py, pc):
            return _owner_block(out_ref, 4 * px + 2 * py + pc, cols)

        def copy(k, block, to, src=None):
            return pltpu.make_async_remote_copy(
                src_ref=slot(*block) if src is None else src, dst_ref=slot(*block),
                send_sem=send_sems.at[k], recv_sem=recv_sems.at[k], device_id=to, device_id_type=MESH)

        mine = pltpu.make_async_copy(x_ref, slot(*me), local_sem)
        mine.start()
        sent = [copy(0, me, sibling, src=x_ref), copy(1, me, x_nbr, src=x_ref), copy(2, me, y_nbr, src=x_ref)]
        for cp in sent:
            cp.start()
        copy(1 + c, near, me).wait_recv()
        sent += [copy(3, near, far), copy(4, near, sibling)]
        sent[-2].start()
        sent[-1].start()
        copy(2 - c, far, me).wait_recv()
        sent.append(copy(5, far, sibling))
        sent[-1].start()
        copy(3, diagonal, me).wait_recv()
        sent.append(copy(6, diagonal, sibling))
        sent[-1].start()
        for k in (0, 4, 5, 6):
            copy(k, sibling, me).wait_recv()
        for cp in sent:
            cp.wait_send()
        mine.wait()

    return _sequencer_call(
        body, name, collective_id, jax.ShapeDtypeStruct(out_shape, blk.dtype),
        [pltpu.SemaphoreType.DMA((7,)), pltpu.SemaphoreType.DMA((7,)), pltpu.SemaphoreType.DMA], blk)


def _sequencer_call(body, name, collective_id, out_type, scratch_types, operand):
    return pl.kernel(
        body, name=name, out_type=out_type, mesh=plsc.ScalarSubcoreMesh(axis_name="seq_core", num_cores=1),
        scratch_types=scratch_types, compiler_params=pltpu.CompilerParams(collective_id=collective_id),
    )(operand)


def _handshake(peers):
    barrier = pltpu.get_barrier_semaphore()
    for peer in peers:
        pl.semaphore_signal(barrier, inc=1, device_id=peer, device_id_type=MESH)
    pl.semaphore_wait(barrier, len(peers))


def _add_sibling(parts, got, c_idx, name):
    _, R, C = got.shape
    tr, tc = _tile(R, 512), _tile(C, 1024)

    def body(c_ref, p_ref, g_ref, o_ref):
        o_ref[...] = (p_ref[...].astype(F32) + g_ref[...].astype(F32)).astype(o_ref.dtype)

    if parts.ndim == 3:
        parts_spec = pl.BlockSpec((None, tr, tc), lambda k, i, j, c_ref: (2 * k + c_ref[0], i, j))
    else:
        parts_spec = pl.BlockSpec((tr, tc), lambda k, i, j, c_ref: (i, (2 * k + c_ref[0]) * (C // tc) + j))
    grid_spec = pltpu.PrefetchScalarGridSpec(
        num_scalar_prefetch=1, grid=(N_CHIP, R // tr, C // tc),
        in_specs=[parts_spec, pl.BlockSpec((None, tr, tc), lambda k, i, j, c_ref: (k, i, j))],
        out_specs=pl.BlockSpec((None, tr, tc), lambda k, i, j, c_ref: (k, i, j)))
    return pl.pallas_call(
        body, name=name, grid_spec=grid_spec, out_shape=jax.ShapeDtypeStruct((N_CHIP, R, C), parts.dtype),
        compiler_params=_params(("parallel", "parallel", "parallel")),
    )(c_idx, parts, got)


def _scatter_to_chips(sums, name, collective_id):
    _, R, C = sums.shape

    def body(q_ref, got_ref, send_sems, recv_sems, local_sem):
        x, y, c = _place()
        _handshake([(*chip, c) for chip in _other_chips(x, y)])
        my_chip = 2 * x + y
        mine = pltpu.make_async_copy(q_ref.at[my_chip], got_ref.at[my_chip], local_sem)
        mine.start()
        copies = [pltpu.make_async_remote_copy(
            src_ref=q_ref.at[2 * px + py], dst_ref=got_ref.at[my_chip], send_sem=send_sems.at[j],
            recv_sem=recv_sems.at[j], device_id=(px, py, c), device_id_type=MESH)
            for j, (px, py) in enumerate(_other_chips(x, y))]
        for cp in copies:
            cp.start()
        for cp in copies:
            cp.wait()
        mine.wait()

    return _sequencer_call(
        body, name, collective_id, jax.ShapeDtypeStruct((N_CHIP, R, C), sums.dtype),
        [pltpu.SemaphoreType.DMA((3,)), pltpu.SemaphoreType.DMA((3,)), pltpu.SemaphoreType.DMA], sums)


def _adamw_math(w, g, m, v):
    m = ADAM_B1 * m + (1.0 - ADAM_B1) * g
    v = ADAM_B2 * v + (1.0 - ADAM_B2) * (g * g)
    m_hat = m / (1.0 - ADAM_B1 ** ADAM_STEP)
    v_hat = v / (1.0 - ADAM_B2 ** ADAM_STEP)
    delta = -ADAM_LR * (m_hat / (jnp.sqrt(v_hat) + ADAM_EPS) + ADAM_WD * w)
    return delta, m, v


def _sum_adamw(parts, w, m, v, name):
    n_parts, R, C = parts.shape
    tr, tc = _tile(R, 256), _tile(C, 1024)

    def body(p_ref, w_ref, m_ref, v_ref, g_out, d_out, m_out, v_out):
        g = p_ref[0].astype(F32)
        for k in range(1, n_parts):
            g = g + p_ref[k].astype(F32)
        delta, mn, vn = _adamw_math(w_ref[...], g, m_ref[...], v_ref[...])
        g_out[...] = g
        d_out[...] = delta
        m_out[...] = mn
        v_out[...] = vn

    blk = pl.BlockSpec((tr, tc), lambda i, j: (i, j))
    shp = jax.ShapeDtypeStruct((R, C), F32)
    return pl.pallas_call(
        body, name=name, grid=(R // tr, C // tc),
        in_specs=[pl.BlockSpec((n_parts, tr, tc), lambda i, j: (0, i, j)), blk, blk, blk],
        out_specs=[blk, blk, blk, blk], out_shape=[shp, shp, shp, shp],
        compiler_params=_params(("parallel", "parallel")),
    )(parts, w, m, v)


def _after(value, dep):
    return lax.optimization_barrier((value, dep))[0]


def _rows128(a):
    return a.reshape(-1, 128)


def kernel(x, mix_norm_g, w_in, conv_w, spatial_w, spatial_b, conv_out_norm_g, gmlp_out_norm_g, w_out, mlp_norm_g, w_up, w_down, final_norm_g, loss_target, m_mix_norm_g, m_w_in, m_conv_w, m_spatial_w, m_spatial_b, m_conv_out_norm_g, m_gmlp_out_norm_g, m_w_out, m_mlp_norm_g, m_w_up, m_w_down, m_final_norm_g, v_mix_norm_g, v_w_in, v_conv_w, v_spatial_w, v_spatial_b, v_conv_out_norm_g, v_gmlp_out_norm_g, v_w_out, v_mlp_norm_g, v_w_up, v_w_down, v_final_norm_g):
    n_seq, seq, D = x.shape
    T = n_seq * seq
    n_in = w_in.shape[2]
    n_out = w_out.shape[1]
    n_up = w_up.shape[2]
    wc = conv_w.shape[2] * N_DEV
    n_head = wc // HEAD
    FF = n_up * N_DEV
    assert N_DEV * n_in == 5 * wc and seq % HEAD == 0 and D == 2 * wc

    c_idx = lax.axis_index("c").astype(jnp.int32).reshape(1)
    my_dev = 4 * lax.axis_index("x") + 2 * lax.axis_index("y") + lax.axis_index("c")

    xf = x.reshape(T, D)
    tgt = loss_target.reshape(T, D)

    cw_pad = jnp.pad(conv_w[0], ((0, HALO - CONV_K), (0, 0)))
    cw_g = _all_gather(cw_pad, "ag_conv_w")
    conv_full = jnp.transpose(cw_g[:, :CONV_K, :], (1, 0, 2)).reshape(CONV_K, wc)
    cast = lambda w: _after(w[0].astype(BF16), cw_g)
    win_g = _all_gather_async(cast(w_in), "ag_w_in", 0, side_by_side=True)
    wout_g = _all_gather_async(cast(w_out), "ag_w_out", 1).reshape(D, D)
    wup_g = _all_gather_async(cast(w_up), "ag_w_up", 2, side_by_side=True)
    wdown_g = _all_gather_async(cast(w_down), "ag_w_down", 3).reshape(FF, D)

    sw = spatial_w[0]
    sbt = spatial_b[0].T
    g_mix, g_a, g_b, g_mlp = mix_norm_g, conv_out_norm_g, gmlp_out_norm_g, mlp_norm_g
    g_fin = final_norm_g.reshape(1, D)

    IN = N_DEV * n_in
    bp = _tile(T, 512)
    bm = _tile(T, 1024)
    bn = _tile(D, 1024)
    bu = _tile(FF, 1024)
    bw = _tile(D, 512)
    k_ff = _tile(FF, 4096)
    k_in = _tile(IN, 5120)

    def tile(shape, index):
        return pl.BlockSpec(shape, index)

    by_m0 = lambda n, m, k: (m, 0)
    by_0n = lambda n, m, k: (0, n)
    by_n0 = lambda n, m, k: (n, 0)
    by_mn = lambda n, m, k: (m, n)
    by_mk = lambda n, m, k: (m, k)
    by_kn = lambda n, m, k: (k, n)
    by_nk = lambda n, m, k: (n, k)
    by_0m = lambda n, m, k: (0, m)
    f32_td = [jax.ShapeDtypeStruct((T, D), F32)]

    xn = _rms_fwd(xf, g_mix, "norm_mix")
    proj = _matmul(
        "proj", (N_DEV, T // bp, 1), NN, [xn, win_g], [tile((bp, D), by_m0), tile((D, n_in), by_0n)],
        [jax.ShapeDtypeStruct((T, IN), F32)], [tile((bp, n_in), by_mn)], _ident)[0]
    y = _mixer_fwd(proj, conv_full, sw, sbt, g_a, g_b, n_seq, seq)
    h1 = _matmul(
        "out_proj", (D // bn, T // bp, 1), NN, [y, wout_g, xf],
        [tile((bp, D), by_m0), tile((D, bn), by_0n), tile((bp, bn), by_mn)],
        f32_td, [tile((bp, bn), by_mn)], lambda acc, res: (res + acc,))[0]
    xn2 = _rms_fwd(h1, g_mlp, "norm_mlp")

    def up_epilogue(acc):
        r = jnp.maximum(acc, 0.0)
        return acc, r * r

    up, act = _matmul(
        "up_proj", (FF // bu, T // bm, 1), NN, [xn2, wup_g], [tile((bm, D), by_m0), tile((D, bu), by_0n)],
        [jax.ShapeDtypeStruct((T, FF), BF16)] * 2, [tile((bm, bu), by_mn)] * 2, up_epilogue)
    mlp = _matmul(
        "down_proj", (D // bn, T // bm, FF // k_ff), NN, [act, wdown_g],
        [tile((bm, k_ff), by_mk), tile((k_ff, bn), by_kn)], f32_td, [tile((bm, bn), by_mn)], _ident)[0]

    dh2, dh2b, gg_fin, loss_row = _loss_head(h1, mlp, tgt, g_fin)
    loss = lax.psum(loss_row[0, 0], ("x", "y", "c"))

    dup = _matmul(
        "d_act", (FF // bu, T // bm, 1), NT, [dh2b, wdown_g, up],
        [tile((bm, D), by_m0), tile((bu, D), by_n0), tile((bm, bu), by_mn)],
        [jax.ShapeDtypeStruct((T, FF), BF16)], [tile((bm, bu), by_mn)],
        lambda acc, u: (acc * (2.0 * jnp.maximum(u.astype(F32), 0.0)),))[0]
    gp_down = _matmul(
        "gw_down", (D // bn, FF // bw, 1), TN, [act, dh2b], [tile((T, bw), by_0m), tile((T, bn), by_0n)],
        [jax.ShapeDtypeStruct((FF, D), BF16)], [tile((bw, bn), by_mn)], _ident)[0].reshape(N_DEV, n_up, D)
    dxn2, got_down = _matmul(
        "d_xn2", (D // bn, T // bm, FF // k_ff), NT, [dup, wup_g],
        [tile((bm, k_ff), by_mk), tile((bn, k_ff), by_nk)], f32_td, [tile((bm, bn), by_mn)], _ident,
        swap=(gp_down, None))
    gp_up = _matmul(
        "gw_up", (FF // bu, D // bw, 1), TN, [xn2, _after(dup, dxn2)],
        [tile((T, bw), by_0m), tile((T, bu), by_0n)],
        [jax.ShapeDtypeStruct((D, FF), BF16)], [tile((bw, bu), by_mn)], _ident)[0]
    sums_down = _add_sibling(gp_down, _after(got_down, gp_up), c_idx, "rs_add_w_down")
    four_down = _scatter_to_chips(sums_down, "rs_chips_w_down", 4)
    dh1, dh1b, gg_mlp = _rms_bwd(_after(dxn2, sums_down), h1, g_mlp, dh2, "norm_mlp_bwd", True)

    dy, got_up = _matmul(
        "d_y", (D // bn, T // bm, 1), NT, [dh1b, wout_g], [tile((bm, D), by_m0), tile((bn, D), by_n0)],
        f32_td, [tile((bm, bn), by_mn)], _ident, swap=(gp_up, n_up))
    gp_out = _matmul(
        "gw_out", (D // bn, D // bw, 1), TN, [y, _after(dh1b, dy)], [tile((T, bw), by_0m), tile((T, bn), by_0n)],
        [jax.ShapeDtypeStruct((D, D), BF16)], [tile((bw, bn), by_mn)], _ident)[0].reshape(N_DEV, n_out, D)
    sums_up = _add_sibling(gp_up, _after(got_up, gp_out), c_idx, "rs_add_w_up")
    four_up = _scatter_to_chips(sums_up, "rs_chips_w_up", 5)
    dproj, gl_conv, gl_a, gl_b, gl_sw, gl_sbt = _mixer_bwd(
        proj, _after(dy, sums_up), conv_full, sw, sbt, g_a, g_b, n_seq, seq)
    gp_in, got_out = _matmul(
        "gw_in", (N_DEV, D // bw, 1), TN, [xn, dproj], [tile((T, bw), by_0m), tile((T, n_in), by_0n)],
        [jax.ShapeDtypeStruct((D, IN), BF16)], [tile((bw, n_in), by_mn)], _ident, swap=(gp_out, None))
    sums_out = _add_sibling(gp_out, got_out, c_idx, "rs_add_w_out")
    four_out = _scatter_to_chips(sums_out, "rs_chips_w_out", 6)
    dxn, got_in = _matmul(
        "d_xn", (D // bn, T // bm, IN // k_in), NT, [_after(dproj, sums_out), win_g],
        [tile((bm, k_in), by_mk), tile((bn, k_in), by_nk)], f32_td, [tile((bm, bn), by_mn)], _ident,
        swap=(gp_in, n_in))
    sums_in = _add_sibling(gp_in, got_in, c_idx, "rs_add_w_in")
    four_in = _scatter_to_chips(sums_in, "rs_chips_w_in", 7)
    grad_x, gg_mix = _rms_bwd(_after(dxn, sums_in), xf, g_mix, dh1, "norm_mix_bwd", False)

    outs = {}
    done = grad_x
    for tag, four, w, m, v in (("w_down", four_down, w_down, m_w_down, v_w_down),
                               ("w_up", four_up, w_up, m_w_up, v_w_up),
                               ("w_out", four_out, w_out, m_w_out, v_w_out),
                               ("w_in", four_in, w_in, m_w_in, v_w_in)):
        res = _sum_adamw(_after(four, done), w[0], m[0], v[0], "adamw_" + tag)
        done = res[0]
        outs[tag] = [a[None] for a in res]

    small = [("mix_norm_g", gg_mix, mix_norm_g, m_mix_norm_g, v_mix_norm_g),
             ("conv_w", gl_conv, None, None, None),
             ("spatial_w", gl_sw, spatial_w, m_spatial_w, v_spatial_w),
             ("spatial_b", gl_sbt.T, spatial_b, m_spatial_b, v_spatial_b),
             ("conv_out_norm_g", gl_a, conv_out_norm_g, m_conv_out_norm_g, v_conv_out_norm_g),
             ("gmlp_out_norm_g", gl_b, gmlp_out_norm_g, m_gmlp_out_norm_g, v_gmlp_out_norm_g),
             ("mlp_norm_g", gg_mlp, mlp_norm_g, m_mlp_norm_g, v_mlp_norm_g),
             ("final_norm_g", gg_fin, final_norm_g, m_final_norm_g, v_final_norm_g)]
    packed_g = jnp.concatenate([_rows128(g) for _, g, _, _, _ in small], axis=0)
    zeros_cw = jnp.zeros((CONV_K * wc // 128, 128), F32)
    pack = lambda idx: jnp.concatenate(
        [zeros_cw if item[2] is None else _rows128(item[idx]) for item in small], axis=0)
    all_g = _all_gather_async(packed_g, "ag_small_grads", 8)
    sg, sd, sm, sv = _sum_adamw(_after(all_g, done), pack(2), pack(3), pack(4), "adamw_small")
    row = 0
    for name, g, w, _, _ in small:
        n_rows = g.size // 128
        if w is not None:
            outs[name] = [a[row:row + n_rows].reshape(w.shape) for a in (sg, sd, sm, sv)]
        else:
            conv_grad_full = sg[row:row + n_rows].reshape(CONV_K, wc)
        row += n_rows
    cpd = wc // N_DEV
    conv_grad = lax.dynamic_slice(conv_grad_full, (0, my_dev * cpd), (CONV_K, cpd))
    pad8 = lambda a: jnp.pad(a, ((0, HALO - CONV_K), (0, 0)))
    outs["conv_w"] = [a[:CONV_K][None] for a in _sum_adamw(
        pad8(conv_grad)[None], pad8(conv_w[0]), pad8(m_conv_w[0]), pad8(v_conv_w[0]), "adamw_conv_w")]

    order = ["mix_norm_g", "w_in", "conv_w", "spatial_w", "spatial_b", "conv_out_norm_g", "gmlp_out_norm_g",
             "w_out", "mlp_norm_g", "w_up", "w_down", "final_norm_g"]
    result = [loss, grad_x.reshape(n_seq, seq, D)]
    for k in range(4):
        result += [outs[n][k] for n in order]
    return tuple(result)
```

```python
import functools
import math
from typing import Callable, NamedTuple

import jax
import jax.numpy as jnp
from jax import lax
from jax.experimental import pallas as pl
from jax.experimental.pallas import tpu as pltpu
from jax.experimental.pallas import tpu_sc as plsc

F32 = jnp.float32
BF16 = jnp.bfloat16
MESH = pl.DeviceIdType.MESH
HBM = pltpu.HBM

EPS = 1e-5
HEAD = 128
CONV_K = 3
N_DEV = 8
N_CHIP = 4
VMEM_LIMIT_BYTES = 62 * 1024 * 1024

ADAM_LR = 0.001
ADAM_B1 = 0.9
ADAM_B2 = 0.999
ADAM_EPS = 1e-08
ADAM_WD = 0.01
ADAM_STEP = 10

GELU_K0 = math.sqrt(2.0 / math.pi)
GELU_K1 = 0.044715

NN = (((1,), (0,)), ((), ()))
NT = (((1,), (1,)), ((), ()))
TN = (((0,), (0,)), ((), ()))


def _params(semantics):
    return pltpu.CompilerParams(dimension_semantics=semantics, vmem_limit_bytes=VMEM_LIMIT_BYTES)


def _tile(dim, want):
    if dim <= want:
        return dim
    for t in range(want - want % 8, 0, -8):
        if dim % t == 0:
            return t
    raise ValueError((dim, want))


def _owner_block(ref, j, cols):
    if cols is None:
        return ref.at[j]
    return ref.at[:, pl.ds(pl.multiple_of(j * cols, 128), cols)]


def _sibling_copies(p_ref, got_ref, send_sems, recv_sems, cols):
    x, y, c = _place()
    return [pltpu.make_async_remote_copy(
        src_ref=_owner_block(p_ref, 2 * k + (1 - c), cols), dst_ref=got_ref.at[k], send_sem=send_sems.at[k],
        recv_sem=recv_sems.at[k], device_id=(x, y, 1 - c), device_id_type=MESH) for k in range(N_CHIP)]


class _SideJob(NamedTuple):
    c_idx: jax.Array
    arrays: list
    in_tiles: list
    out_shapes: list
    out_tiles: list
    fn: Callable


def _matmul(name, grid, dims, operands, in_specs, out_shapes, out_specs, epilogue, swap=None, side=None, fill=None):
    n_in = len(operands)
    n_out = len(out_shapes)
    nk = grid[2]
    n_host = 0 if swap is None else 1
    n_si = 0 if side is None else len(side.arrays)
    n_so = 0 if side is None else len(side.out_shapes)
    n_fill = 0 if fill is None else 1
    assert nk == 1 or (n_out == 1 and epilogue is _ident and out_shapes[0].dtype == F32)
    assert fill is None or side is None

    def body(*refs):
        if side is not None:
            refs = refs[1:]
        a_ref, b_ref = refs[0], refs[1]
        extra = refs[2:n_in]
        first_out = n_in + n_si + n_host + n_fill
        outs = refs[first_out:first_out + n_out]
        if side is not None:
            res = side.fn(*[r[...] for r in refs[n_in:n_in + n_si]])
            for o, r in zip(refs[first_out + n_out:first_out + n_out + n_so], res):
                o[...] = r
        if swap is not None:
            ids = [pl.program_id(d) for d in range(3)]
            copies = _sibling_copies(refs[n_in + n_si], refs[first_out + n_out + n_so], refs[-2], refs[-1], swap[1])

            @pl.when(functools.reduce(jnp.logical_and, [i == 0 for i in ids]))
            def _():
                for cp in copies:
                    cp.start()

        part = lax.dot_general(a_ref[...], b_ref[...], dims, preferred_element_type=F32)

        def finish(acc):
            res = epilogue(acc, *[e[...] for e in extra])
            for o, r in zip(outs, res):
                o[...] = r.astype(o.dtype)

        if nk == 1:
            finish(part)
        else:
            k = pl.program_id(2)

            @pl.when(k == 0)
            def _():
                outs[0][...] = part

            @pl.when(k > 0)
            def _():
                outs[0][...] += part

        if swap is not None:
            @pl.when(functools.reduce(jnp.logical_and, [i == n - 1 for i, n in zip(ids, grid)]))
            def _():
                for cp in copies:
                    cp.wait()

    scratch = []
    semantics = ("parallel", "parallel", "arbitrary")
    operands, in_specs = list(operands), list(in_specs)
    out_shapes, out_specs = list(out_shapes), list(out_specs)
    if side is not None:
        def with_c(spec):
            return pl.BlockSpec(spec.block_shape, lambda n, m, k, c_ref, f=spec.index_map: f(n, m, k))

        def side_tile(shape, index):
            return pl.BlockSpec(shape, lambda n, m, k, c_ref: index((n * grid[1] + m) * grid[2] + k, c_ref[0]))

        in_specs = [with_c(s) for s in in_specs] + [side_tile(*t) for t in side.in_tiles]
        out_specs = [with_c(s) for s in out_specs] + [side_tile(*t) for t in side.out_tiles]
        operands += side.arrays
        out_shapes += side.out_shapes
    if swap is not None:
        parts, cols = swap
        got_shape = parts.shape[1:] if cols is None else (parts.shape[0], cols)
        hbm = pl.BlockSpec(memory_space=HBM)
        operands, in_specs = operands + [parts], in_specs + [hbm]
        out_shapes = out_shapes + [jax.ShapeDtypeStruct((N_CHIP,) + got_shape, parts.dtype)]
        out_specs = out_specs + [hbm]
        scratch += [pltpu.SemaphoreType.DMA((N_CHIP,)), pltpu.SemaphoreType.DMA((N_CHIP,))]
        semantics = ("arbitrary", "arbitrary", "arbitrary")
    aliases = {}
    if fill is not None:
        aliases = {len(operands): 0}
        operands, in_specs = operands + [fill], in_specs + [pl.BlockSpec(memory_space=pl.ANY)]
    if side is None:
        return pl.pallas_call(
            body, name=name, grid=grid, in_specs=in_specs, out_specs=out_specs, out_shape=out_shapes,
            scratch_shapes=scratch, input_output_aliases=aliases, compiler_params=_params(semantics),
        )(*operands)
    grid_spec = pltpu.PrefetchScalarGridSpec(
        num_scalar_prefetch=1, grid=grid, in_specs=in_specs, out_specs=out_specs, scratch_shapes=scratch)
    return pl.pallas_call(
        body, name=name, grid_spec=grid_spec, out_shape=out_shapes, compiler_params=_params(semantics),
    )(side.c_idx, *operands)


def _add_job(parts, got, c_idx, steps):
    _, R, C = got.shape
    rows = N_CHIP * R
    tr = rows // steps
    assert rows % steps == 0 and R % tr == 0 and tr % 16 == 0, (rows, steps)
    per = R // tr
    if parts.ndim == 3:
        flat = parts.reshape(N_DEV * R, C)
        parts_index = lambda s, c: ((2 * (s // per) + c) * per + s % per, 0)
    else:
        flat = parts
        parts_index = lambda s, c: (s % per, 2 * (s // per) + c)
    by_step = lambda s, c: (s, 0)
    return _SideJob(
        c_idx, [flat, got.reshape(rows, C)], [((tr, C), parts_index), ((tr, C), by_step)],
        [jax.ShapeDtypeStruct((rows, C), parts.dtype)], [((tr, C), by_step)],
        lambda p, g: ((p.astype(F32) + g.astype(F32)).astype(p.dtype),))


def _ident(acc):
    return (acc,)


ROW_TILE = 256


def _rms_fwd(x, g, name):
    T, D = x.shape
    tr = _tile(T, ROW_TILE)

    def body(x_ref, g_ref, o_ref):
        xv = x_ref[...]
        inv = lax.rsqrt(jnp.mean(xv * xv, axis=-1, keepdims=True) + EPS)
        o_ref[...] = (xv * inv * g_ref[...]).astype(o_ref.dtype)

    return pl.pallas_call(
        body, name=name, grid=(T // tr,),
        in_specs=[pl.BlockSpec((tr, D), lambda i: (i, 0)), pl.BlockSpec((1, D), lambda i: (0, 0))],
        out_specs=pl.BlockSpec((tr, D), lambda i: (i, 0)),
        out_shape=jax.ShapeDtypeStruct((T, D), BF16),
        compiler_params=_params(("parallel",)),
    )(x, g)


def _rms_bwd(dy, x, g, dres, name, want_bf16):
    T, D = x.shape
    tr = _tile(T, ROW_TILE)

    def body(dy_ref, x_ref, g_ref, dres_ref, *outs):
        dx_ref, gg_ref = outs[0], outs[-1]
        i = pl.program_id(0)
        xv = x_ref[...]
        dyv = dy_ref[...]
        inv = lax.rsqrt(jnp.mean(xv * xv, axis=-1, keepdims=True) + EPS)
        gd = dyv * g_ref[...]
        dot = jnp.mean(gd * xv, axis=-1, keepdims=True)
        dx = dres_ref[...] + (inv * gd - xv * (inv * inv * inv * dot))
        dx_ref[...] = dx
        if want_bf16:
            outs[1][...] = dx.astype(BF16)
        part = jnp.sum(dyv * xv * inv, axis=0, keepdims=True)

        @pl.when(i == 0)
        def _():
            gg_ref[...] = part

        @pl.when(i > 0)
        def _():
            gg_ref[...] += part

    row = pl.BlockSpec((tr, D), lambda i: (i, 0))
    vec = pl.BlockSpec((1, D), lambda i: (0, 0))
    out_shape = [jax.ShapeDtypeStruct((T, D), F32)]
    out_specs = [row]
    if want_bf16:
        out_shape.append(jax.ShapeDtypeStruct((T, D), BF16))
        out_specs.append(row)
    out_shape.append(jax.ShapeDtypeStruct((1, D), F32))
    out_specs.append(vec)
    return pl.pallas_call(
        body, name=name, grid=(T // tr,), in_specs=[row, row, vec, row],
        out_specs=out_specs, out_shape=out_shape, compiler_params=_params(("arbitrary",)),
    )(dy, x, g, dres)


def _loss_head(h1, mlp, target, g):
    T, D = h1.shape
    tr = _tile(T, ROW_TILE)

    def body(h1_ref, mlp_ref, t_ref, g_ref, dh_ref, dhb_ref, gg_ref, loss_ref):
        i = pl.program_id(0)
        hv = h1_ref[...] + mlp_ref[...]
        gv = g_ref[...]
        inv = lax.rsqrt(jnp.mean(hv * hv, axis=-1, keepdims=True) + EPS)
        diff = hv * inv * gv - t_ref[...]
        lpart = 0.5 * jnp.sum(jnp.mean(diff * diff, axis=-1, keepdims=True), axis=0, keepdims=True)
        dout = diff * (1.0 / D)
        gd = dout * gv
        dot = jnp.mean(gd * hv, axis=-1, keepdims=True)
        dh = inv * gd - hv * (inv * inv * inv * dot)
        dh_ref[...] = dh
        dhb_ref[...] = dh.astype(BF16)
        part = jnp.sum(dout * hv * inv, axis=0, keepdims=True)
        lrow = jnp.broadcast_to(lpart, (1, 128))

        @pl.when(i == 0)
        def _():
            gg_ref[...] = part
            loss_ref[...] = lrow

        @pl.when(i > 0)
        def _():
            gg_ref[...] += part
            loss_ref[...] += lrow

    row = pl.BlockSpec((tr, D), lambda i: (i, 0))
    vec = pl.BlockSpec((1, D), lambda i: (0, 0))
    return pl.pallas_call(
        body, name="loss_head", grid=(T // tr,), in_specs=[row, row, row, vec],
        out_specs=[row, row, vec, pl.BlockSpec((1, 128), lambda i: (0, 0))],
        out_shape=[jax.ShapeDtypeStruct((T, D), F32), jax.ShapeDtypeStruct((T, D), BF16),
                   jax.ShapeDtypeStruct((1, D), F32), jax.ShapeDtypeStruct((1, 128), F32)],
        compiler_params=_params(("arbitrary",)),
    )(h1, mlp, target, g)


HALO = 8


def _gelu_parts(x):
    th = jnp.tanh(GELU_K0 * (x + GELU_K1 * (x * x * x)))
    return x * (0.5 * (1.0 + th)), th


def _gelu_grad(x, th):
    return 0.5 * (1.0 + th) + (0.5 * GELU_K0) * x * (1.0 - th * th) * (1.0 + (3.0 * GELU_K1) * (x * x))


def _conv_fwd(b_ref, c_ref, h_ref, ch_ref, hh_ref, w_ref, first):
    tt, wc = c_ref.shape
    c = c_ref[...]
    h = h_ref[...]
    hc = c * h
    prev1 = jnp.where(first, 0.0, ch_ref[HALO - 1:HALO, :] * hh_ref[HALO - 1:HALO, :])
    prev2 = jnp.where(first, 0.0, ch_ref[HALO - 2:HALO - 1, :] * hh_ref[HALO - 2:HALO - 1, :])
    row = lax.broadcasted_iota(jnp.int32, (tt, wc), 0)
    m1 = jnp.where(row == 0, prev1, pltpu.roll(hc, 1, 0))
    m2 = jnp.where(row == 0, prev2, jnp.where(row == 1, prev1, pltpu.roll(hc, 2, 0)))
    conv = w_ref[0:1, :] * m2 + w_ref[1:2, :] * m1 + w_ref[2:3, :] * hc
    return c, h, hc, m1, m2, conv, b_ref[...] * conv


def _tril():
    r = lax.broadcasted_iota(jnp.int32, (HEAD, HEAD), 0)
    s = lax.broadcasted_iota(jnp.int32, (HEAD, HEAD), 1)
    return r >= s


def _spatial_fwd(gvb, sw_ref, sbt_ref, s_scr):
    n_head = sw_ref.shape[0]
    tri = _tril()
    for hd in range(n_head):
        sl = slice(hd * HEAD, (hd + 1) * HEAD)
        wm = jnp.where(tri, sw_ref[hd], 0.0).astype(BF16)
        s_scr[:, sl] = jnp.dot(wm, gvb[:, sl], preferred_element_type=F32) + sbt_ref[:, hd:hd + 1]


def _mixer_specs(n_tiles, wc, row_of, n_grid):
    def grp(g):
        return pl.BlockSpec((HEAD, wc), lambda *ids: (row_of(*ids), g))

    def halo(g):
        return pl.BlockSpec((HALO, wc), lambda *ids: (jnp.maximum(row_of(*ids) * (HEAD // HALO) - 1, 0), g))

    return grp, halo


def _mixer_fwd(proj, conv_w, sw, sbt, g_a, g_b, n_seq, seq):
    T, w5 = proj.shape
    wc = w5 // 5
    n_head = wc // HEAD
    nt = seq // HEAD

    def row_of(s, i):
        return s * nt + i

    grp, halo = _mixer_specs(nt, wc, row_of, 2)

    def body(b_ref, c_ref, h_ref, u_ref, v_ref, ch_ref, hh_ref, w_ref, sw_ref, sbt_ref, ga_ref, gb_ref,
             y_ref, s_scr):
        i = pl.program_id(1)
        ya = _conv_fwd(b_ref, c_ref, h_ref, ch_ref, hh_ref, w_ref, i == 0)[-1]
        inv = lax.rsqrt(jnp.mean(ya * ya, axis=-1, keepdims=True) + EPS)
        y_ref[:, :wc] = (ya * inv * ga_ref[...]).astype(BF16)
        gu, _ = _gelu_parts(u_ref[...])
        gv, _ = _gelu_parts(v_ref[...])
        _spatial_fwd(gv.astype(BF16), sw_ref, sbt_ref, s_scr)
        yb = gu * s_scr[...]
        inv = lax.rsqrt(jnp.mean(yb * yb, axis=-1, keepdims=True) + EPS)
        y_ref[:, wc:] = (yb * inv * gb_ref[...]).astype(BF16)

    const2 = lambda shape: pl.BlockSpec(shape, lambda s, i: (0, 0))
    return pl.pallas_call(
        body, name="mixer_fwd", grid=(n_seq, nt),
        in_specs=[grp(0), grp(1), grp(2), grp(3), grp(4), halo(1), halo(2),
                  const2((CONV_K, wc)), pl.BlockSpec((n_head, HEAD, HEAD), lambda s, i: (0, 0, 0)),
                  const2((HEAD, n_head)), const2((1, wc)), const2((1, wc))],
        out_specs=pl.BlockSpec((HEAD, 2 * wc), lambda s, i: (s * nt + i, 0)),
        out_shape=jax.ShapeDtypeStruct((T, 2 * wc), BF16),
        scratch_shapes=[pltpu.VMEM((HEAD, wc), F32)],
        compiler_params=_params(("parallel", "parallel")),
    )(proj, proj, proj, proj, proj, proj, proj, conv_w, sw, sbt, g_a, g_b)


def _mixer_bwd(proj, dy, conv_w, sw, sbt, g_a, g_b, n_seq, seq):
    T, w5 = proj.shape
    wc = w5 // 5
    n_head = wc // HEAD
    nt = seq // HEAD
    tt = HEAD

    def row_of(s, ir):
        return s * nt + (nt - 1 - ir)

    grp, halo = _mixer_specs(nt, wc, row_of, 2)

    def body(b_ref, c_ref, h_ref, u_ref, v_ref, ch_ref, hh_ref, dya_ref, dyb_ref, w_ref, sw_ref, sbt_ref,
             ga_ref, gb_ref, dp_ref, gw_ref, gga_ref, ggb_ref, gsw_ref, gsb_ref,
             carry_scr, s_scr, t_scr, dsum_scr):
        s_id = pl.program_id(0)
        ir = pl.program_id(1)
        first_tile = jnp.logical_and(s_id == 0, ir == 0)
        last_tile = jnp.logical_and(s_id == n_seq - 1, ir == nt - 1)

        def conv_part():
            @pl.when(ir == 0)
            def _():
                carry_scr[...] = jnp.zeros_like(carry_scr)

            c, h, hc, m1, m2, conv, ya = _conv_fwd(b_ref, c_ref, h_ref, ch_ref, hh_ref, w_ref, ir == nt - 1)
            inv = lax.rsqrt(jnp.mean(ya * ya, axis=-1, keepdims=True) + EPS)
            dyn = dya_ref[...]
            gd = dyn * ga_ref[...]
            dot = jnp.mean(gd * ya, axis=-1, keepdims=True)
            dya = inv * gd - ya * (inv * inv * inv * dot)
            gg = jnp.sum(dyn * ya * inv, axis=0, keepdims=True)
            dconv = dya * b_ref[...]
            nxt0 = carry_scr[0:1, :]
            nxt1 = carry_scr[1:2, :]
            row = lax.broadcasted_iota(jnp.int32, (tt, wc), 0)
            p1 = jnp.where(row == tt - 1, nxt0, pltpu.roll(dconv, tt - 1, 0))
            p2 = jnp.where(row == tt - 2, nxt0, jnp.where(row == tt - 1, nxt1, pltpu.roll(dconv, tt - 2, 0)))
            dhc = w_ref[2:3, :] * dconv + w_ref[1:2, :] * p1 + w_ref[0:1, :] * p2
            carry_scr[...] = dconv[0:HALO, :]
            dp_ref[:, 0:wc] = (dya * conv).astype(BF16)
            dp_ref[:, wc:2 * wc] = (dhc * h).astype(BF16)
            dp_ref[:, 2 * wc:3 * wc] = (dhc * c).astype(BF16)
            gw0 = jnp.sum(dconv * m2, axis=0, keepdims=True)
            gw1 = jnp.sum(dconv * m1, axis=0, keepdims=True)
            gw2 = jnp.sum(dconv * hc, axis=0, keepdims=True)

            @pl.when(first_tile)
            def _():
                gw_ref[0:1, :] = gw0
                gw_ref[1:2, :] = gw1
                gw_ref[2:3, :] = gw2
                gga_ref[...] = gg

            @pl.when(jnp.logical_not(first_tile))
            def _():
                gw_ref[0:1, :] += gw0
                gw_ref[1:2, :] += gw1
                gw_ref[2:3, :] += gw2
                gga_ref[...] += gg

        def gate_part():
            u = u_ref[...]
            v = v_ref[...]
            gu, thu = _gelu_parts(u)
            gv, thv = _gelu_parts(v)
            gvb = gv.astype(BF16)
            _spatial_fwd(gvb, sw_ref, sbt_ref, s_scr)
            sv = s_scr[...]
            yb = gu * sv
            inv = lax.rsqrt(jnp.mean(yb * yb, axis=-1, keepdims=True) + EPS)
            dyn = dyb_ref[...]
            gd = dyn * gb_ref[...]
            dot = jnp.mean(gd * yb, axis=-1, keepdims=True)
            dyb = inv * gd - yb * (inv * inv * inv * dot)
            gg = jnp.sum(dyn * yb * inv, axis=0, keepdims=True)
            ds = dyb * gu
            dsb = ds.astype(BF16)
            tri = _tril()

            @pl.when(first_tile)
            def _():
                ggb_ref[...] = gg
                dsum_scr[...] = ds
                gsw_ref[...] = jnp.zeros_like(gsw_ref)

            @pl.when(jnp.logical_not(first_tile))
            def _():
                ggb_ref[...] += gg
                dsum_scr[...] += ds

            for hd in range(n_head):
                sl = slice(hd * HEAD, (hd + 1) * HEAD)
                wm = jnp.where(tri, sw_ref[hd], 0.0).astype(BF16)
                t_scr[:, sl] = lax.dot_general(wm, dsb[:, sl], TN, preferred_element_type=F32)
                gsw_ref[hd] += lax.dot_general(dsb[:, sl], gvb[:, sl], NT, preferred_element_type=F32)
            dp_ref[:, 3 * wc:4 * wc] = (dyb * sv * _gelu_grad(u, thu)).astype(BF16)
            dp_ref[:, 4 * wc:5 * wc] = (t_scr[...] * _gelu_grad(v, thv)).astype(BF16)

            @pl.when(last_tile)
            def _():
                for hd in range(n_head):
                    sl = slice(hd * HEAD, (hd + 1) * HEAD)
                    gsw_ref[hd] = jnp.where(tri, gsw_ref[hd], 0.0)
                    gsb_ref[:, hd:hd + 1] = jnp.sum(dsum_scr[:, sl], axis=1, keepdims=True)

        conv_part()
        gate_part()

    const2 = lambda shape: pl.BlockSpec(shape, lambda s, i: (0, 0))
    const3 = pl.BlockSpec((n_head, HEAD, HEAD), lambda s, i: (0, 0, 0))
    dy_spec = lambda col: pl.BlockSpec((tt, wc), lambda s, ir: (row_of(s, ir), col))
    return pl.pallas_call(
        body, name="mixer_bwd", grid=(n_seq, nt),
        in_specs=[grp(0), grp(1), grp(2), grp(3), grp(4), halo(1), halo(2), dy_spec(0), dy_spec(1),
                  const2((CONV_K, wc)), const3, const2((HEAD, n_head)), const2((1, wc)), const2((1, wc))],
        out_specs=[pl.BlockSpec((tt, 5 * wc), lambda s, ir: (row_of(s, ir), 0)),
                   const2((CONV_K, wc)), const2((1, wc)), const2((1, wc)), const3, const2((HEAD, n_head))],
        out_shape=[jax.ShapeDtypeStruct((T, 5 * wc), BF16), jax.ShapeDtypeStruct((CONV_K, wc), F32),
                   jax.ShapeDtypeStruct((1, wc), F32), jax.ShapeDtypeStruct((1, wc), F32),
                   jax.ShapeDtypeStruct((n_head, HEAD, HEAD), F32), jax.ShapeDtypeStruct((HEAD, n_head), F32)],
        scratch_shapes=[pltpu.VMEM((HALO, wc), F32), pltpu.VMEM((tt, wc), F32), pltpu.VMEM((tt, wc), F32),
                        pltpu.VMEM((tt, wc), F32)],
        compiler_params=_params(("arbitrary", "arbitrary")),
    )(proj, proj, proj, proj, proj, proj, proj, dy, dy, conv_w, sw, sbt, g_a, g_b)


def _place():
    return lax.axis_index("x"), lax.axis_index("y"), lax.axis_index("c")


def _other_chips(x, y):
    return [(1 - x, y), (x, 1 - y), (1 - x, 1 - y)]


def _all_gather(blk, name):
    def body(x_ref, out_ref, send_sems, recv_sems, local_sem):
        x, y, c = _place()
        me, sibling = (x, y, c), (x, y, 1 - c)
        chips = _other_chips(x, y)

        def slot(px, py, pc):
            return out_ref.at[4 * px + 2 * py + pc]

        def copy(k, block, to, src=None):
            return pltpu.make_async_remote_copy(
                src_ref=slot(*block) if src is None else src, dst_ref=slot(*block),
                send_sem=send_sems.at[k], recv_sem=recv_sems.at[k], device_id=to, device_id_type=MESH)

        mine = pltpu.make_async_copy(x_ref, slot(*me), local_sem)
        mine.start()
        first = [copy(0, me, sibling, src=x_ref)]
        first += [copy(1 + j, me, (*chip, c), src=x_ref) for j, chip in enumerate(chips)]
        for cp in first:
            cp.start()
        passed = [copy(4 + j, (*chip, c), sibling) for j, chip in enumerate(chips)]
        for j, chip in enumerate(chips):
            copy(1 + j, (*chip, c), me).wait_recv()
            passed[j].start()
        copy(0, sibling, me).wait_recv()
        for j, chip in enumerate(chips):
            copy(4 + j, (*chip, 1 - c), me).wait_recv()
        for cp in first + passed:
            cp.wait_send()
        mine.wait()

    return pl.pallas_call(
        body, name=name, in_specs=[pl.BlockSpec(memory_space=HBM)], out_specs=pl.BlockSpec(memory_space=HBM),
        out_shape=jax.ShapeDtypeStruct((N_DEV,) + blk.shape, blk.dtype),
        scratch_shapes=[pltpu.SemaphoreType.DMA((7,)), pltpu.SemaphoreType.DMA((7,)), pltpu.SemaphoreType.DMA],
    )(blk)


def _all_gather_async(blk, name, collective_id, side_by_side=False):
    cols = blk.shape[1] if side_by_side else None
    out_shape = (blk.shape[0], N_DEV * cols) if side_by_side else (N_DEV,) + blk.shape

    def body(x_ref, out_ref, send_sems, recv_sems, local_sem):
        x, y, c = _place()
        me, sibling = (x, y, c), (x, y, 1 - c)
        x_nbr, y_nbr, diagonal = (1 - x, y, c), (x, 1 - y, c), (1 - x, 1 - y, c)
        near = (c * x + (1 - c) * (1 - x), c * (1 - y) + (1 - c) * y, c)
        far = ((1 - c) * x + c * (1 - x), (1 - c) * (1 - y) + c * y, c)
        _handshake([sibling, x_nbr, y_nbr])

        def slot(px, py, pc):
            return _owner_block(out_ref, 4 * px + 2 * py + pc, cols)

        def copy(k, block, to, src=None):
            return pltpu.make_async_remote_copy(
                src_ref=slot(*block) if src is None else src, dst_ref=slot(*block),
                send_sem=send_sems.at[k], recv_sem=recv_sems.at[k], device_id=to, device_id_type=MESH)

        mine = pltpu.make_async_copy(x_ref, slot(*me), local_sem)
        mine.start()
        sent = [copy(0, me, sibling, src=x_ref), copy(1, me, x_nbr, src=x_ref), copy(2, me, y_nbr, src=x_ref)]
        for cp in sent:
            cp.start()
        copy(1 + c, near, me).wait_recv()
        sent += [copy(3, near, far), copy(4, near, sibling)]
        sent[-2].start()
        sent[-1].start()
        copy(2 - c, far, me).wait_recv()
        sent.append(copy(5, far, sibling))
        sent[-1].start()
        copy(3, diagonal, me).wait_recv()
        sent.append(copy(6, diagonal, sibling))
        sent[-1].start()
        for k in (0, 4, 5, 6):
            copy(k, sibling, me).wait_recv()
        for cp in sent:
            cp.wait_send()
        mine.wait()

    return _sequencer_call(
        body, name, collective_id, jax.ShapeDtypeStruct(out_shape, blk.dtype),
        [pltpu.SemaphoreType.DMA((7,)), pltpu.SemaphoreType.DMA((7,)), pltpu.SemaphoreType.DMA], blk)


def _sequencer_call(body, name, collective_id, out_type, scratch_types, operand):
    return pl.kernel(
        body, name=name, out_type=out_type, mesh=plsc.ScalarSubcoreMesh(axis_name="seq_core", num_cores=1),
        scratch_types=scratch_types, compiler_params=pltpu.CompilerParams(collective_id=collective_id),
    )(operand)


def _handshake(peers):
    barrier = pltpu.get_barrier_semaphore()
    for peer in peers:
        pl.semaphore_signal(barrier, inc=1, device_id=peer, device_id_type=MESH)
    pl.semaphore_wait(barrier, len(peers))


def _add_sibling(parts, got, c_idx, name):
    _, R, C = got.shape
    tr, tc = _tile(R, 512), _tile(C, 1024)

    def body(c_ref, p_ref, g_ref, o_ref):
        o_ref[...] = (p_ref[...].astype(F32) + g_ref[...].astype(F32)).astype(o_ref.dtype)

    if parts.ndim == 3:
        parts_spec = pl.BlockSpec((None, tr, tc), lambda k, i, j, c_ref: (2 * k + c_ref[0], i, j))
    else:
        parts_spec = pl.BlockSpec((tr, tc), lambda k, i, j, c_ref: (i, (2 * k + c_ref[0]) * (C // tc) + j))
    grid_spec = pltpu.PrefetchScalarGridSpec(
        num_scalar_prefetch=1, grid=(N_CHIP, R // tr, C // tc),
        in_specs=[parts_spec, pl.BlockSpec((None, tr, tc), lambda k, i, j, c_ref: (k, i, j))],
        out_specs=pl.BlockSpec((None, tr, tc), lambda k, i, j, c_ref: (k, i, j)))
    return pl.pallas_call(
        body, name=name, grid_spec=grid_spec, out_shape=jax.ShapeDtypeStruct((N_CHIP, R, C), parts.dtype),
        compiler_params=_params(("parallel", "parallel", "parallel")),
    )(c_idx, parts, got)


def _scatter_to_chips(sums, name, collective_id):
    _, R, C = sums.shape

    def body(q_ref, got_ref, send_sems, recv_sems, local_sem):
        x, y, c = _place()
        _handshake([(*chip, c) for chip in _other_chips(x, y)])
        my_chip = 2 * x + y
        mine = pltpu.make_async_copy(q_ref.at[my_chip], got_ref.at[my_chip], local_sem)
        mine.start()
        copies = [pltpu.make_async_remote_copy(
            src_ref=q_ref.at[2 * px + py], dst_ref=got_ref.at[my_chip], send_sem=send_sems.at[j],
            recv_sem=recv_sems.at[j], device_id=(px, py, c), device_id_type=MESH)
            for j, (px, py) in enumerate(_other_chips(x, y))]
        for cp in copies:
            cp.start()
        for cp in copies:
            cp.wait()
        mine.wait()

    return _sequencer_call(
        body, name, collective_id, jax.ShapeDtypeStruct((N_CHIP, R, C), sums.dtype),
        [pltpu.SemaphoreType.DMA((3,)), pltpu.SemaphoreType.DMA((3,)), pltpu.SemaphoreType.DMA], sums)


def _adamw_math(w, g, m, v):
    m = ADAM_B1 * m + (1.0 - ADAM_B1) * g
    v = ADAM_B2 * v + (1.0 - ADAM_B2) * (g * g)
    m_hat = m / (1.0 - ADAM_B1 ** ADAM_STEP)
    v_hat = v / (1.0 - ADAM_B2 ** ADAM_STEP)
    delta = -ADAM_LR * (m_hat / (jnp.sqrt(v_hat) + ADAM_EPS) + ADAM_WD * w)
    return delta, m, v


def _sum_adamw(parts, w, m, v, name):
    n_parts, R, C = parts.shape
    tr, tc = _tile(R, 256), _tile(C, 1024)

    def body(p_ref, w_ref, m_ref, v_ref, g_out, d_out, m_out, v_out):
        g = p_ref[0].astype(F32)
        for k in range(1, n_parts):
            g = g + p_ref[k].astype(F32)
        delta, mn, vn = _adamw_math(w_ref[...], g, m_ref[...], v_ref[...])
        g_out[...] = g
        d_out[...] = delta
        m_out[...] = mn
        v_out[...] = vn

    blk = pl.BlockSpec((tr, tc), lambda i, j: (i, j))
    shp = jax.ShapeDtypeStruct((R, C), F32)
    return pl.pallas_call(
        body, name=name, grid=(R // tr, C // tc),
        in_specs=[pl.BlockSpec((n_parts, tr, tc), lambda i, j: (0, i, j)), blk, blk, blk],
        out_specs=[blk, blk, blk, blk], out_shape=[shp, shp, shp, shp],
        compiler_params=_params(("parallel", "parallel")),
    )(parts, w, m, v)


def _after(value, dep):
    return lax.optimization_barrier((value, dep))[0]


def _rows128(a):
    return a.reshape(-1, 128)


def kernel(x, mix_norm_g, w_in, conv_w, spatial_w, spatial_b, conv_out_norm_g, gmlp_out_norm_g, w_out, mlp_norm_g, w_up, w_down, final_norm_g, loss_target, m_mix_norm_g, m_w_in, m_conv_w, m_spatial_w, m_spatial_b, m_conv_out_norm_g, m_gmlp_out_norm_g, m_w_out, m_mlp_norm_g, m_w_up, m_w_down, m_final_norm_g, v_mix_norm_g, v_w_in, v_conv_w, v_spatial_w, v_spatial_b, v_conv_out_norm_g, v_gmlp_out_norm_g, v_w_out, v_mlp_norm_g, v_w_up, v_w_down, v_final_norm_g):
    n_seq, seq, D = x.shape
    T = n_seq * seq
    n_in = w_in.shape[2]
    n_out = w_out.shape[1]
    n_up = w_up.shape[2]
    wc = conv_w.shape[2] * N_DEV
    n_head = wc // HEAD
    FF = n_up * N_DEV
    assert N_DEV * n_in == 5 * wc and seq % HEAD == 0 and D == 2 * wc

    c_idx = lax.axis_index("c").astype(jnp.int32).reshape(1)
    my_dev = 4 * lax.axis_index("x") + 2 * lax.axis_index("y") + lax.axis_index("c")

    xf = x.reshape(T, D)
    tgt = loss_target.reshape(T, D)

    cw_pad = jnp.pad(conv_w[0], ((0, HALO - CONV_K), (0, 0)))
    cw_g = _all_gather(cw_pad, "ag_conv_w")
    conv_full = jnp.transpose(cw_g[:, :CONV_K, :], (1, 0, 2)).reshape(CONV_K, wc)
    cast = lambda w: _after(w[0].astype(BF16), cw_g)
    win_g = _all_gather_async(cast(w_in), "ag_w_in", 0, side_by_side=True)
    wout_g = _all_gather_async(cast(w_out), "ag_w_out", 1).reshape(D, D)
    wup_g = _all_gather_async(cast(w_up), "ag_w_up", 2, side_by_side=True)
    wdown_g = _all_gather_async(cast(w_down), "ag_w_down", 3).reshape(FF, D)

    sw = spatial_w[0]
    sbt = spatial_b[0].T
    g_mix, g_a, g_b, g_mlp = mix_norm_g, conv_out_norm_g, gmlp_out_norm_g, mlp_norm_g
    g_fin = final_norm_g.reshape(1, D)

    IN = N_DEV * n_in
    bp = _tile(T, 512)
    bm = _tile(T, 1024)
    bn = _tile(D, 1024)
    bu = _tile(FF, 1024)
    bw = _tile(D, 512)
    k_ff = _tile(FF, 4096)
    k_in = _tile(IN, 5120)

    def tile(shape, index):
        return pl.BlockSpec(shape, index)

    by_m0 = lambda n, m, k: (m, 0)
    by_0n = lambda n, m, k: (0, n)
    by_n0 = lambda n, m, k: (n, 0)
    by_mn = lambda n, m, k: (m, n)
    by_mk = lambda n, m, k: (m, k)
    by_kn = lambda n, m, k: (k, n)
    by_nk = lambda n, m, k: (n, k)
    by_0m = lambda n, m, k: (0, m)
    f32_td = [jax.ShapeDtypeStruct((T, D), F32)]

    xn = _rms_fwd(xf, g_mix, "norm_mix")
    proj = _matmul(
        "proj", (N_DEV, T // bp, 1), NN, [xn, win_g], [tile((bp, D), by_m0), tile((D, n_in), by_0n)],
        [jax.ShapeDtypeStruct((T, IN), F32)], [tile((bp, n_in), by_mn)], _ident)[0]
    y = _mixer_fwd(proj, conv_full, sw, sbt, g_a, g_b, n_seq, seq)
    h1 = _matmul(
        "out_proj", (D // bn, T // bp, 1), NN, [y, wout_g, xf],
        [tile((bp, D), by_m0), tile((D, bn), by_0n), tile((bp, bn), by_mn)],
        f32_td, [tile((bp, bn), by_mn)], lambda acc, res: (res + acc,))[0]
    xn2 = _rms_fwd(h1, g_mlp, "norm_mlp")

    def up_epilogue(acc):
        r = jnp.maximum(acc, 0.0)
        return acc, r * r

    up, act = _matmul(
        "up_proj", (FF // bu, T // bm, 1), NN, [xn2, wup_g], [tile((bm, D), by_m0), tile((D, bu), by_0n)],
        [jax.ShapeDtypeStruct((T, FF), BF16)] * 2, [tile((bm, bu), by_mn)] * 2, up_epilogue)
    mlp = _matmul(
        "down_proj", (D // bn, T // bm, FF // k_ff), NN, [act, wdown_g],
        [tile((bm, k_ff), by_mk), tile((k_ff, bn), by_kn)], f32_td, [tile((bm, bn), by_mn)], _ident)[0]

    dh2, dh2b, gg_fin, loss_row = _loss_head(h1, mlp, tgt, g_fin)
    loss = lax.psum(loss_row[0, 0], ("x", "y", "c"))

    gp_down = _matmul(
        "gw_down", (D // bn, FF // bw, 1), TN, [act, dh2b], [tile((T, bw), by_0m), tile((T, bn), by_0n)],
        [jax.ShapeDtypeStruct((FF, D), BF16)], [tile((bw, bn), by_mn)], _ident)[0].reshape(N_DEV, n_up, D)
    dup, got_down = _matmul(
        "d_act", (FF // bu, T // bm, 1), NT, [dh2b, wdown_g, up],
        [tile((bm, D), by_m0), tile((bu, D), by_n0), tile((bm, bu), by_mn)],
        [jax.ShapeDtypeStruct((T, FF), BF16)], [tile((bm, bu), by_mn)],
        lambda acc, u: (acc * (2.0 * jnp.maximum(u.astype(F32), 0.0)),), swap=(gp_down, None))
    dxn2, sums_down = _matmul(
        "d_xn2", (D // bn, T // bm, FF // k_ff), NT, [dup, wup_g],
        [tile((bm, k_ff), by_mk), tile((bn, k_ff), by_nk)], f32_td, [tile((bm, bn), by_mn)], _ident,
        side=_add_job(gp_down, got_down, c_idx, (D // bn) * (T // bm) * (FF // k_ff)))
    sums_down = sums_down.reshape(N_CHIP, n_up, D)
    four_down = _scatter_to_chips(sums_down, "rs_chips_w_down", 4)
    gp_up = _matmul(
        "gw_up", (FF // bu, D // bw, 1), TN, [xn2, _after(dup, sums_down)],
        [tile((T, bw), by_0m), tile((T, bu), by_0n)],
        [jax.ShapeDtypeStruct((D, FF), BF16)], [tile((bw, bu), by_mn)], _ident)[0]
    dh1, dh1b, gg_mlp = _rms_bwd(_after(dxn2, gp_up), h1, g_mlp, dh2, "norm_mlp_bwd", True)

    dy, got_up = _matmul(
        "d_y", (D // bn, T // bm, 1), NT, [dh1b, wout_g], [tile((bm, D), by_m0), tile((bn, D), by_n0)],
        f32_td, [tile((bm, bn), by_mn)], _ident, swap=(gp_up, n_up))
    gp_out, sums_up = _matmul(
        "gw_out", (D // bn, D // bw, 1), TN, [y, _after(dh1b, dy)], [tile((T, bw), by_0m), tile((T, bn), by_0n)],
        [jax.ShapeDtypeStruct((D, D), BF16)], [tile((bw, bn), by_mn)], _ident,
        side=_add_job(gp_up, got_up, c_idx, (D // bn) * (D // bw)))
    gp_out = gp_out.reshape(N_DEV, n_out, D)
    sums_up = sums_up.reshape(N_CHIP, D, n_up)
    four_up = _scatter_to_chips(sums_up, "rs_chips_w_up", 5)
    dproj, gl_conv, gl_a, gl_b, gl_sw, gl_sbt = _mixer_bwd(
        proj, _after(dy, sums_up), conv_full, sw, sbt, g_a, g_b, n_seq, seq)
    gp_in, got_out = _matmul(
        "gw_in", (N_DEV, D // bw, 1), TN, [xn, dproj], [tile((T, bw), by_0m), tile((T, n_in), by_0n)],
        [jax.ShapeDtypeStruct((D, IN), BF16)], [tile((bw, n_in), by_mn)], _ident, swap=(gp_out, None))
    sums_out = _add_sibling(gp_out, got_out, c_idx, "rs_add_w_out")
    four_out = _scatter_to_chips(sums_out, "rs_chips_w_out", 6)
    bh = _tile(T // 2, bm)
    hm = (T // 2) // bh
    dproj = _after(dproj, sums_out)
    dxn, got_in = _matmul(
        "d_xn_top", (D // bn, hm, IN // k_in), NT, [dproj, win_g],
        [tile((bh, k_in), by_mk), tile((bn, k_in), by_nk)], f32_td, [tile((bh, bn), by_mn)], _ident,
        swap=(gp_in, n_in))
    sums_in = _add_sibling(gp_in, got_in, c_idx, "rs_add_w_in")
    four_in = _scatter_to_chips(sums_in, "rs_chips_w_in", 7)
    dxn = _matmul(
        "d_xn_bottom", (D // bn, hm, IN // k_in), NT, [_after(dproj, sums_in), win_g],
        [tile((bh, k_in), lambda n, m, k: (m + hm, k)), tile((bn, k_in), by_nk)], f32_td,
        [tile((bh, bn), lambda n, m, k: (m + hm, n))], _ident, fill=dxn)[0]
    grad_x, gg_mix = _rms_bwd(_after(dxn, sums_in), xf, g_mix, dh1, "norm_mix_bwd", False)

    outs = {}
    done = grad_x
    for tag, four, w, m, v in (("w_down", four_down, w_down, m_w_down, v_w_down),
                               ("w_up", four_up, w_up, m_w_up, v_w_up),
                               ("w_out", four_out, w_out, m_w_out, v_w_out),
                               ("w_in", four_in, w_in, m_w_in, v_w_in)):
        res = _sum_adamw(_after(four, done), w[0], m[0], v[0], "adamw_" + tag)
        done = res[0]
        outs[tag] = [a[None] for a in res]

    small = [("mix_norm_g", gg_mix, mix_norm_g, m_mix_norm_g, v_mix_norm_g),
             ("conv_w", gl_conv, None, None, None),
             ("spatial_w", gl_sw, spatial_w, m_spatial_w, v_spatial_w),
             ("spatial_b", gl_sbt.T, spatial_b, m_spatial_b, v_spatial_b),
             ("conv_out_norm_g", gl_a, conv_out_norm_g, m_conv_out_norm_g, v_conv_out_norm_g),
             ("gmlp_out_norm_g", gl_b, gmlp_out_norm_g, m_gmlp_out_norm_g, v_gmlp_out_norm_g),
             ("mlp_norm_g", gg_mlp, mlp_norm_g, m_mlp_norm_g, v_mlp_norm_g),
             ("final_norm_g", gg_fin, final_norm_g, m_final_norm_g, v_final_norm_g)]
    packed_g = jnp.concatenate([_rows128(g) for _, g, _, _, _ in small], axis=0)
    zeros_cw = jnp.zeros((CONV_K * wc // 128, 128), F32)
    pack = lambda idx: jnp.concatenate(
        [zeros_cw if item[2] is None else _rows128(item[idx]) for item in small], axis=0)
    all_g = _all_gather_async(packed_g, "ag_small_grads", 8)
    sg, sd, sm, sv = _sum_adamw(_after(all_g, done), pack(2), pack(3), pack(4), "adamw_small")
    row = 0
    for name, g, w, _, _ in small:
        n_rows = g.size // 128
        if w is not None:
            outs[name] = [a[row:row + n_rows].reshape(w.shape) for a in (sg, sd, sm, sv)]
        else:
            conv_grad_full = sg[row:row + n_rows].reshape(CONV_K, wc)
        row += n_rows
    cpd = wc // N_DEV
    conv_grad = lax.dynamic_slice(conv_grad_full, (0, my_dev * cpd), (CONV_K, cpd))
    pad8 = lambda a: jnp.pad(a, ((0, HALO - CONV_K), (0, 0)))
    outs["conv_w"] = [a[:CONV_K][None] for a in _sum_adamw(
        pad8(conv_grad)[None], pad8(conv_w[0]), pad8(m_conv_w[0]), pad8(v_conv_w[0]), "adamw_conv_w")]

    order = ["mix_norm_g", "w_in", "conv_w", "spatial_w", "spatial_b", "conv_out_norm_g", "gmlp_out_norm_g",
             "w_out", "mlp_norm_g", "w_up", "w_down", "final_norm_g"]
    result = [loss, grad_x.reshape(n_seq, seq, D)]
    for k in range(4):
        result += [outs[n][k] for n in order]
    return tuple(result)
```

```python
import functools
import math
from typing import Callable, NamedTuple

import jax
import jax.numpy as jnp
from jax import lax
from jax.experimental import pallas as pl
from jax.experimental.pallas import tpu as pltpu
from jax.experimental.pallas import tpu_sc as plsc

F32 = jnp.float32
BF16 = jnp.bfloat16
MESH = pl.DeviceIdType.MESH
HBM = pltpu.HBM

EPS = 1e-5
HEAD = 128
CONV_K = 3
N_DEV = 8
N_CHIP = 4
VMEM_LIMIT_BYTES = 62 * 1024 * 1024

ADAM_LR = 0.001
ADAM_B1 = 0.9
ADAM_B2 = 0.999
ADAM_EPS = 1e-08
ADAM_WD = 0.01
ADAM_STEP = 10

GELU_K0 = math.sqrt(2.0 / math.pi)
GELU_K1 = 0.044715

NN = (((1,), (0,)), ((), ()))
NT = (((1,), (1,)), ((), ()))
TN = (((0,), (0,)), ((), ()))


def _params(semantics):
    return pltpu.CompilerParams(dimension_semantics=semantics, vmem_limit_bytes=VMEM_LIMIT_BYTES)


def _tile(dim, want):
    if dim <= want:
        return dim
    for t in range(want - want % 8, 0, -8):
        if dim % t == 0:
            return t
    raise ValueError((dim, want))


def _owner_block(ref, j, cols):
    if cols is None:
        return ref.at[j]
    return ref.at[:, pl.ds(pl.multiple_of(j * cols, 128), cols)]


def _sibling_copies(p_ref, got_ref, send_sems, recv_sems, cols):
    x, y, c = _place()
    return [pltpu.make_async_remote_copy(
        src_ref=_owner_block(p_ref, 2 * k + (1 - c), cols), dst_ref=got_ref.at[k], send_sem=send_sems.at[k],
        recv_sem=recv_sems.at[k], device_id=(x, y, 1 - c), device_id_type=MESH) for k in range(N_CHIP)]


class _SideJob(NamedTuple):
    c_idx: jax.Array
    arrays: list
    in_tiles: list
    out_shapes: list
    out_tiles: list
    fn: Callable


def _matmul(name, grid, dims, operands, in_specs, out_shapes, out_specs, epilogue, swap=None, side=None, fill=None):
    n_in = len(operands)
    n_out = len(out_shapes)
    nk = grid[2]
    n_host = 0 if swap is None else 1
    n_si = 0 if side is None else len(side.arrays)
    n_so = 0 if side is None else len(side.out_shapes)
    n_fill = 0 if fill is None else 1
    assert nk == 1 or (n_out == 1 and epilogue is _ident and out_shapes[0].dtype == F32)
    assert fill is None or side is None

    def body(*refs):
        if side is not None:
            refs = refs[1:]
        a_ref, b_ref = refs[0], refs[1]
        extra = refs[2:n_in]
        first_out = n_in + n_si + n_host + n_fill
        outs = refs[first_out:first_out + n_out]
        if side is not None:
            res = side.fn(*[r[...] for r in refs[n_in:n_in + n_si]])
            for o, r in zip(refs[first_out + n_out:first_out + n_out + n_so], res):
                o[...] = r
        if swap is not None:
            ids = [pl.program_id(d) for d in range(3)]
            copies = _sibling_copies(refs[n_in + n_si], refs[first_out + n_out + n_so], refs[-2], refs[-1], swap[1])

            @pl.when(functools.reduce(jnp.logical_and, [i == 0 for i in ids]))
            def _():
                for cp in copies:
                    cp.start()

        part = lax.dot_general(a_ref[...], b_ref[...], dims, preferred_element_type=F32)

        def finish(acc):
            res = epilogue(acc, *[e[...] for e in extra])
            for o, r in zip(outs, res):
                o[...] = r.astype(o.dtype)

        if nk == 1:
            finish(part)
        else:
            k = pl.program_id(2)

            @pl.when(k == 0)
            def _():
                outs[0][...] = part

            @pl.when(k > 0)
            def _():
                outs[0][...] += part

        if swap is not None:
            @pl.when(functools.reduce(jnp.logical_and, [i == n - 1 for i, n in zip(ids, grid)]))
            def _():
                for cp in copies:
                    cp.wait()

    scratch = []
    semantics = ("parallel", "parallel", "arbitrary")
    operands, in_specs = list(operands), list(in_specs)
    out_shapes, out_specs = list(out_shapes), list(out_specs)
    if side is not None:
        def with_c(spec):
            return pl.BlockSpec(spec.block_shape, lambda n, m, k, c_ref, f=spec.index_map: f(n, m, k))

        def side_tile(shape, index):
            return pl.BlockSpec(shape, lambda n, m, k, c_ref: index((n * grid[1] + m) * grid[2] + k, c_ref[0]))

        in_specs = [with_c(s) for s in in_specs] + [side_tile(*t) for t in side.in_tiles]
        out_specs = [with_c(s) for s in out_specs] + [side_tile(*t) for t in side.out_tiles]
        operands += side.arrays
        out_shapes += side.out_shapes
    if swap is not None:
        parts, cols = swap
        got_shape = parts.shape[1:] if cols is None else (parts.shape[0], cols)
        hbm = pl.BlockSpec(memory_space=HBM)
        operands, in_specs = operands + [parts], in_specs + [hbm]
        out_shapes = out_shapes + [jax.ShapeDtypeStruct((N_CHIP,) + got_shape, parts.dtype)]
        out_specs = out_specs + [hbm]
        scratch += [pltpu.SemaphoreType.DMA((N_CHIP,)), pltpu.SemaphoreType.DMA((N_CHIP,))]
        semantics = ("arbitrary", "arbitrary", "arbitrary")
    aliases = {}
    if fill is not None:
        aliases = {len(operands): 0}
        operands, in_specs = operands + [fill], in_specs + [pl.BlockSpec(memory_space=pl.ANY)]
    if side is None:
        return pl.pallas_call(
            body, name=name, grid=grid, in_specs=in_specs, out_specs=out_specs, out_shape=out_shapes,
            scratch_shapes=scratch, input_output_aliases=aliases, compiler_params=_params(semantics),
        )(*operands)
    grid_spec = pltpu.PrefetchScalarGridSpec(
        num_scalar_prefetch=1, grid=grid, in_specs=in_specs, out_specs=out_specs, scratch_shapes=scratch)
    return pl.pallas_call(
        body, name=name, grid_spec=grid_spec, out_shape=out_shapes, compiler_params=_params(semantics),
    )(side.c_idx, *operands)


def _add_job(parts, got, c_idx, steps):
    _, R, C = got.shape
    rows = N_CHIP * R
    tr = rows // steps
    assert rows % steps == 0 and R % tr == 0 and tr % 16 == 0, (rows, steps)
    per = R // tr
    if parts.ndim == 3:
        flat = parts.reshape(N_DEV * R, C)
        parts_index = lambda s, c: ((2 * (s // per) + c) * per + s % per, 0)
    else:
        flat = parts
        parts_index = lambda s, c: (s % per, 2 * (s // per) + c)
    by_step = lambda s, c: (s, 0)
    return _SideJob(
        c_idx, [flat, got.reshape(rows, C)], [((tr, C), parts_index), ((tr, C), by_step)],
        [jax.ShapeDtypeStruct((rows, C), parts.dtype)], [((tr, C), by_step)],
        lambda p, g: ((p.astype(F32) + g.astype(F32)).astype(p.dtype),))


def _ident(acc):
    return (acc,)


ROW_TILE = 256


def _rms_fwd(x, g, name):
    T, D = x.shape
    tr = _tile(T, ROW_TILE)

    def body(x_ref, g_ref, o_ref):
        xv = x_ref[...]
        inv = lax.rsqrt(jnp.mean(xv * xv, axis=-1, keepdims=True) + EPS)
        o_ref[...] = (xv * inv * g_ref[...]).astype(o_ref.dtype)

    return pl.pallas_call(
        body, name=name, grid=(T // tr,),
        in_specs=[pl.BlockSpec((tr, D), lambda i: (i, 0)), pl.BlockSpec((1, D), lambda i: (0, 0))],
        out_specs=pl.BlockSpec((tr, D), lambda i: (i, 0)),
        out_shape=jax.ShapeDtypeStruct((T, D), BF16),
        compiler_params=_params(("parallel",)),
    )(x, g)


def _rms_bwd(dy, x, g, dres, name, want_bf16):
    T, D = x.shape
    tr = _tile(T, ROW_TILE)

    def body(dy_ref, x_ref, g_ref, dres_ref, *outs):
        dx_ref, gg_ref = outs[0], outs[-1]
        i = pl.program_id(0)
        xv = x_ref[...]
        dyv = dy_ref[...]
        inv = lax.rsqrt(jnp.mean(xv * xv, axis=-1, keepdims=True) + EPS)
        gd = dyv * g_ref[...]
        dot = jnp.mean(gd * xv, axis=-1, keepdims=True)
        dx = dres_ref[...] + (inv * gd - xv * (inv * inv * inv * dot))
        dx_ref[...] = dx
        if want_bf16:
            outs[1][...] = dx.astype(BF16)
        part = jnp.sum(dyv * xv * inv, axis=0, keepdims=True)

        @pl.when(i == 0)
        def _():
            gg_ref[...] = part

        @pl.when(i > 0)
        def _():
            gg_ref[...] += part

    row = pl.BlockSpec((tr, D), lambda i: (i, 0))
    vec = pl.BlockSpec((1, D), lambda i: (0, 0))
    out_shape = [jax.ShapeDtypeStruct((T, D), F32)]
    out_specs = [row]
    if want_bf16:
        out_shape.append(jax.ShapeDtypeStruct((T, D), BF16))
        out_specs.append(row)
    out_shape.append(jax.ShapeDtypeStruct((1, D), F32))
    out_specs.append(vec)
    return pl.pallas_call(
        body, name=name, grid=(T // tr,), in_specs=[row, row, vec, row],
        out_specs=out_specs, out_shape=out_shape, compiler_params=_params(("arbitrary",)),
    )(dy, x, g, dres)


def _loss_head(h1, mlp, target, g):
    T, D = h1.shape
    tr = _tile(T, ROW_TILE)

    def body(h1_ref, mlp_ref, t_ref, g_ref, dh_ref, dhb_ref, gg_ref, loss_ref):
        i = pl.program_id(0)
        hv = h1_ref[...] + mlp_ref[...]
        gv = g_ref[...]
        inv = lax.rsqrt(jnp.mean(hv * hv, axis=-1, keepdims=True) + EPS)
        diff = hv * inv * gv - t_ref[...]
        lpart = 0.5 * jnp.sum(jnp.mean(diff * diff, axis=-1, keepdims=True), axis=0, keepdims=True)
        dout = diff * (1.0 / D)
        gd = dout * gv
        dot = jnp.mean(gd * hv, axis=-1, keepdims=True)
        dh = inv * gd - hv * (inv * inv * inv * dot)
        dh_ref[...] = dh
        dhb_ref[...] = dh.astype(BF16)
        part = jnp.sum(dout * hv * inv, axis=0, keepdims=True)
        lrow = jnp.broadcast_to(lpart, (1, 128))

        @pl.when(i == 0)
        def _():
            gg_ref[...] = part
            loss_ref[...] = lrow

        @pl.when(i > 0)
        def _():
            gg_ref[...] += part
            loss_ref[...] += lrow

    row = pl.BlockSpec((tr, D), lambda i: (i, 0))
    vec = pl.BlockSpec((1, D), lambda i: (0, 0))
    return pl.pallas_call(
        body, name="loss_head", grid=(T // tr,), in_specs=[row, row, row, vec],
        out_specs=[row, row, vec, pl.BlockSpec((1, 128), lambda i: (0, 0))],
        out_shape=[jax.ShapeDtypeStruct((T, D), F32), jax.ShapeDtypeStruct((T, D), BF16),
                   jax.ShapeDtypeStruct((1, D), F32), jax.ShapeDtypeStruct((1, 128), F32)],
        compiler_params=_params(("arbitrary",)),
    )(h1, mlp, target, g)


HALO = 8


def _gelu_parts(x):
    th = jnp.tanh(GELU_K0 * (x + GELU_K1 * (x * x * x)))
    return x * (0.5 * (1.0 + th)), th


def _gelu_grad(x, th):
    return 0.5 * (1.0 + th) + (0.5 * GELU_K0) * x * (1.0 - th * th) * (1.0 + (3.0 * GELU_K1) * (x * x))


def _conv_fwd(b_ref, c_ref, h_ref, ch_ref, hh_ref, w_ref, first):
    tt, wc = c_ref.shape
    c = c_ref[...]
    h = h_ref[...]
    hc = c * h
    prev1 = jnp.where(first, 0.0, ch_ref[HALO - 1:HALO, :] * hh_ref[HALO - 1:HALO, :])
    prev2 = jnp.where(first, 0.0, ch_ref[HALO - 2:HALO - 1, :] * hh_ref[HALO - 2:HALO - 1, :])
    row = lax.broadcasted_iota(jnp.int32, (tt, wc), 0)
    m1 = jnp.where(row == 0, prev1, pltpu.roll(hc, 1, 0))
    m2 = jnp.where(row == 0, prev2, jnp.where(row == 1, prev1, pltpu.roll(hc, 2, 0)))
    conv = w_ref[0:1, :] * m2 + w_ref[1:2, :] * m1 + w_ref[2:3, :] * hc
    return c, h, hc, m1, m2, conv, b_ref[...] * conv


def _tril():
    r = lax.broadcasted_iota(jnp.int32, (HEAD, HEAD), 0)
    s = lax.broadcasted_iota(jnp.int32, (HEAD, HEAD), 1)
    return r >= s


def _spatial_fwd(gvb, sw_ref, sbt_ref, s_scr):
    n_head = sw_ref.shape[0]
    tri = _tril()
    for hd in range(n_head):
        sl = slice(hd * HEAD, (hd + 1) * HEAD)
        wm = jnp.where(tri, sw_ref[hd], 0.0).astype(BF16)
        s_scr[:, sl] = jnp.dot(wm, gvb[:, sl], preferred_element_type=F32) + sbt_ref[:, hd:hd + 1]


def _mixer_specs(n_tiles, wc, row_of, n_grid):
    def grp(g):
        return pl.BlockSpec((HEAD, wc), lambda *ids: (row_of(*ids), g))

    def halo(g):
        return pl.BlockSpec((HALO, wc), lambda *ids: (jnp.maximum(row_of(*ids) * (HEAD // HALO) - 1, 0), g))

    return grp, halo


def _mixer_fwd(proj, conv_w, sw, sbt, g_a, g_b, n_seq, seq):
    T, w5 = proj.shape
    wc = w5 // 5
    n_head = wc // HEAD
    nt = seq // HEAD

    def row_of(s, i):
        return s * nt + i

    grp, halo = _mixer_specs(nt, wc, row_of, 2)

    def body(b_ref, c_ref, h_ref, u_ref, v_ref, ch_ref, hh_ref, w_ref, sw_ref, sbt_ref, ga_ref, gb_ref,
             y_ref, s_scr):
        i = pl.program_id(1)
        ya = _conv_fwd(b_ref, c_ref, h_ref, ch_ref, hh_ref, w_ref, i == 0)[-1]
        inv = lax.rsqrt(jnp.mean(ya * ya, axis=-1, keepdims=True) + EPS)
        y_ref[:, :wc] = (ya * inv * ga_ref[...]).astype(BF16)
        gu, _ = _gelu_parts(u_ref[...])
        gv, _ = _gelu_parts(v_ref[...])
        _spatial_fwd(gv.astype(BF16), sw_ref, sbt_ref, s_scr)
        yb = gu * s_scr[...]
        inv = lax.rsqrt(jnp.mean(yb * yb, axis=-1, keepdims=True) + EPS)
        y_ref[:, wc:] = (yb * inv * gb_ref[...]).astype(BF16)

    const2 = lambda shape: pl.BlockSpec(shape, lambda s, i: (0, 0))
    return pl.pallas_call(
        body, name="mixer_fwd", grid=(n_seq, nt),
        in_specs=[grp(0), grp(1), grp(2), grp(3), grp(4), halo(1), halo(2),
                  const2((CONV_K, wc)), pl.BlockSpec((n_head, HEAD, HEAD), lambda s, i: (0, 0, 0)),
                  const2((HEAD, n_head)), const2((1, wc)), const2((1, wc))],
        out_specs=pl.BlockSpec((HEAD, 2 * wc), lambda s, i: (s * nt + i, 0)),
        out_shape=jax.ShapeDtypeStruct((T, 2 * wc), BF16),
        scratch_shapes=[pltpu.VMEM((HEAD, wc), F32)],
        compiler_params=_params(("parallel", "parallel")),
    )(proj, proj, proj, proj, proj, proj, proj, conv_w, sw, sbt, g_a, g_b)


def _mixer_bwd(proj, dy, conv_w, sw, sbt, g_a, g_b, n_seq, seq):
    T, w5 = proj.shape
    wc = w5 // 5
    n_head = wc // HEAD
    nt = seq // HEAD
    tt = HEAD

    def row_of(s, ir):
        return s * nt + (nt - 1 - ir)

    grp, halo = _mixer_specs(nt, wc, row_of, 2)

    def body(b_ref, c_ref, h_ref, u_ref, v_ref, ch_ref, hh_ref, dya_ref, dyb_ref, w_ref, sw_ref, sbt_ref,
             ga_ref, gb_ref, dp_ref, gw_ref, gga_ref, ggb_ref, gsw_ref, gsb_ref,
             carry_scr, s_scr, t_scr, dsum_scr):
        s_id = pl.program_id(0)
        ir = pl.program_id(1)
        first_tile = jnp.logical_and(s_id == 0, ir == 0)
        last_tile = jnp.logical_and(s_id == n_seq - 1, ir == nt - 1)

        def conv_part():
            @pl.when(ir == 0)
            def _():
                carry_scr[...] = jnp.zeros_like(carry_scr)

            c, h, hc, m1, m2, conv, ya = _conv_fwd(b_ref, c_ref, h_ref, ch_ref, hh_ref, w_ref, ir == nt - 1)
            inv = lax.rsqrt(jnp.mean(ya * ya, axis=-1, keepdims=True) + EPS)
            dyn = dya_ref[...]
            gd = dyn * ga_ref[...]
            dot = jnp.mean(gd * ya, axis=-1, keepdims=True)
            dya = inv * gd - ya * (inv * inv * inv * dot)
            gg = jnp.sum(dyn * ya * inv, axis=0, keepdims=True)
            dconv = dya * b_ref[...]
            nxt0 = carry_scr[0:1, :]
            nxt1 = carry_scr[1:2, :]
            row = lax.broadcasted_iota(jnp.int32, (tt, wc), 0)
            p1 = jnp.where(row == tt - 1, nxt0, pltpu.roll(dconv, tt - 1, 0))
            p2 = jnp.where(row == tt - 2, nxt0, jnp.where(row == tt - 1, nxt1, pltpu.roll(dconv, tt - 2, 0)))
            dhc = w_ref[2:3, :] * dconv + w_ref[1:2, :] * p1 + w_ref[0:1, :] * p2
            carry_scr[...] = dconv[0:HALO, :]
            dp_ref[:, 0:wc] = (dya * conv).astype(BF16)
            dp_ref[:, wc:2 * wc] = (dhc * h).astype(BF16)
            dp_ref[:, 2 * wc:3 * wc] = (dhc * c).astype(BF16)
            gw0 = jnp.sum(dconv * m2, axis=0, keepdims=True)
            gw1 = jnp.sum(dconv * m1, axis=0, keepdims=True)
            gw2 = jnp.sum(dconv * hc, axis=0, keepdims=True)

            @pl.when(first_tile)
            def _():
                gw_ref[0:1, :] = gw0
                gw_ref[1:2, :] = gw1
                gw_ref[2:3, :] = gw2
                gga_ref[...] = gg

            @pl.when(jnp.logical_not(first_tile))
            def _():
                gw_ref[0:1, :] += gw0
                gw_ref[1:2, :] += gw1
                gw_ref[2:3, :] += gw2
                gga_ref[...] += gg

        def gate_part():
            u = u_ref[...]
            v = v_ref[...]
            gu, thu = _gelu_parts(u)
            gv, thv = _gelu_parts(v)
            gvb = gv.astype(BF16)
            _spatial_fwd(gvb, sw_ref, sbt_ref, s_scr)
            sv = s_scr[...]
            yb = gu * sv
            inv = lax.rsqrt(jnp.mean(yb * yb, axis=-1, keepdims=True) + EPS)
            dyn = dyb_ref[...]
            gd = dyn * gb_ref[...]
            dot = jnp.mean(gd * yb, axis=-1, keepdims=True)
            dyb = inv * gd - yb * (inv * inv * inv * dot)
            gg = jnp.sum(dyn * yb * inv, axis=0, keepdims=True)
            ds = dyb * gu
            dsb = ds.astype(BF16)
            tri = _tril()

            @pl.when(first_tile)
            def _():
                ggb_ref[...] = gg
                dsum_scr[...] = ds
                gsw_ref[...] = jnp.zeros_like(gsw_ref)

            @pl.when(jnp.logical_not(first_tile))
            def _():
                ggb_ref[...] += gg
                dsum_scr[...] += ds

            for hd in range(n_head):
                sl = slice(hd * HEAD, (hd + 1) * HEAD)
                wm = jnp.where(tri, sw_ref[hd], 0.0).astype(BF16)
                t_scr[:, sl] = lax.dot_general(wm, dsb[:, sl], TN, preferred_element_type=F32)
                gsw_ref[hd] += lax.dot_general(dsb[:, sl], gvb[:, sl], NT, preferred_element_type=F32)
            dp_ref[:, 3 * wc:4 * wc] = (dyb * sv * _gelu_grad(u, thu)).astype(BF16)
            dp_ref[:, 4 * wc:5 * wc] = (t_scr[...] * _gelu_grad(v, thv)).astype(BF16)

            @pl.when(last_tile)
            def _():
                for hd in range(n_head):
                    sl = slice(hd * HEAD, (hd + 1) * HEAD)
                    gsw_ref[hd] = jnp.where(tri, gsw_ref[hd], 0.0)
                    gsb_ref[:, hd:hd + 1] = jnp.sum(dsum_scr[:, sl], axis=1, keepdims=True)

        conv_part()
        gate_part()

    const2 = lambda shape: pl.BlockSpec(shape, lambda s, i: (0, 0))
    const3 = pl.BlockSpec((n_head, HEAD, HEAD), lambda s, i: (0, 0, 0))
    dy_spec = lambda col: pl.BlockSpec((tt, wc), lambda s, ir: (row_of(s, ir), col))
    return pl.pallas_call(
        body, name="mixer_bwd", grid=(n_seq, nt),
        in_specs=[grp(0), grp(1), grp(2), grp(3), grp(4), halo(1), halo(2), dy_spec(0), dy_spec(1),
                  const2((CONV_K, wc)), const3, const2((HEAD, n_head)), const2((1, wc)), const2((1, wc))],
        out_specs=[pl.BlockSpec((tt, 5 * wc), lambda s, ir: (row_of(s, ir), 0)),
                   const2((CONV_K, wc)), const2((1, wc)), const2((1, wc)), const3, const2((HEAD, n_head))],
        out_shape=[jax.ShapeDtypeStruct((T, 5 * wc), BF16), jax.ShapeDtypeStruct((CONV_K, wc), F32),
                   jax.ShapeDtypeStruct((1, wc), F32), jax.ShapeDtypeStruct((1, wc), F32),
                   jax.ShapeDtypeStruct((n_head, HEAD, HEAD), F32), jax.ShapeDtypeStruct((HEAD, n_head), F32)],
        scratch_shapes=[pltpu.VMEM((HALO, wc), F32), pltpu.VMEM((tt, wc), F32), pltpu.VMEM((tt, wc), F32),
                        pltpu.VMEM((tt, wc), F32)],
        compiler_params=_params(("arbitrary", "arbitrary")),
    )(proj, proj, proj, proj, proj, proj, proj, dy, dy, conv_w, sw, sbt, g_a, g_b)


def _place():
    return lax.axis_index("x"), lax.axis_index("y"), lax.axis_index("c")


def _other_chips(x, y):
    return [(1 - x, y), (x, 1 - y), (1 - x, 1 - y)]


def _all_gather(blk, name):
    def body(x_ref, out_ref, send_sems, recv_sems, local_sem):
        x, y, c = _place()
        me, sibling = (x, y, c), (x, y, 1 - c)
        chips = _other_chips(x, y)

        def slot(px, py, pc):
            return out_ref.at[4 * px + 2 * py + pc]

        def copy(k, block, to, src=None):
            return pltpu.make_async_remote_copy(
                src_ref=slot(*block) if src is None else src, dst_ref=slot(*block),
                send_sem=send_sems.at[k], recv_sem=recv_sems.at[k], device_id=to, device_id_type=MESH)

        mine = pltpu.make_async_copy(x_ref, slot(*me), local_sem)
        mine.start()
        first = [copy(0, me, sibling, src=x_ref)]
        first += [copy(1 + j, me, (*chip, c), src=x_ref) for j, chip in enumerate(chips)]
        for cp in first:
            cp.start()
        passed = [copy(4 + j, (*chip, c), sibling) for j, chip in enumerate(chips)]
        for j, chip in enumerate(chips):
            copy(1 + j, (*chip, c), me).wait_recv()
            passed[j].start()
        copy(0, sibling, me).wait_recv()
        for j, chip in enumerate(chips):
            copy(4 + j, (*chip, 1 - c), me).wait_recv()
        for cp in first + passed:
            cp.wait_send()
        mine.wait()

    return pl.pallas_call(
        body, name=name, in_specs=[pl.BlockSpec(memory_space=HBM)], out_specs=pl.BlockSpec(memory_space=HBM),
        out_shape=jax.ShapeDtypeStruct((N_DEV,) + blk.shape, blk.dtype),
        scratch_shapes=[pltpu.SemaphoreType.DMA((7,)), pltpu.SemaphoreType.DMA((7,)), pltpu.SemaphoreType.DMA],
    )(blk)


def _all_gather_async(blk, name, collective_id, side_by_side=False):
    cols = blk.shape[1] if side_by_side else None
    out_shape = (blk.shape[0], N_DEV * cols) if side_by_side else (N_DEV,) + blk.shape

    def body(x_ref, out_ref, send_sems, recv_sems, local_sem):
        x, y, c = _place()
        me, sibling = (x, y, c), (x, y, 1 - c)
        x_nbr, y_nbr, diagonal = (1 - x, y, c), (x, 1 - y, c), (1 - x, 1 - y, c)
        near = (c * x + (1 - c) * (1 - x), c * (1 - y) + (1 - c) * y, c)
        far = ((1 - c) * x + c * (1 - x), (1 - c) * (1 - y) + c * y, c)
        _handshake([sibling, x_nbr, y_nbr])

        def slot(px, py, pc):
            return _owner_block(out_ref, 4 * px + 2 * py + pc, cols)

        def copy(k, block, to, src=None):
            return pltpu.make_async_remote_copy(
                src_ref=slot(*block) if src is None else src, dst_ref=slot(*block),
                send_sem=send_sems.at[k], recv_sem=recv_sems.at[k], device_id=to, device_id_type=MESH)

        mine = pltpu.make_async_copy(x_ref, slot(*me), local_sem)
        mine.start()
        sent = [copy(0, me, sibling, src=x_ref), copy(1, me, x_nbr, src=x_ref), copy(2, me, y_nbr, src=x_ref)]
        for cp in sent:
            cp.start()
        copy(1 + c, near, me).wait_recv()
        sent += [copy(3, near, far), copy(4, near, sibling)]
        sent[-2].start()
        sent[-1].start()
        copy(2 - c, far, me).wait_recv()
        sent.append(copy(5, far, sibling))
        sent[-1].start()
        copy(3, diagonal, me).wait_recv()
        sent.append(copy(6, diagonal, sibling))
        sent[-1].start()
        for k in (0, 4, 5, 6):
            copy(k, sibling, me).wait_recv()
        for cp in sent:
            cp.wait_send()
        mine.wait()

    return _sequencer_call(
        body, name, collective_id, jax.ShapeDtypeStruct(out_shape, blk.dtype),
        [pltpu.SemaphoreType.DMA((7,)), pltpu.SemaphoreType.DMA((7,)), pltpu.SemaphoreType.DMA], blk)


def _sequencer_call(body, name, collective_id, out_type, scratch_types, operand):
    return pl.kernel(
        body, name=name, out_type=out_type, mesh=plsc.ScalarSubcoreMesh(axis_name="seq_core", num_cores=1),
        scratch_types=scratch_types, compiler_params=pltpu.CompilerParams(collective_id=collective_id),
    )(operand)


def _handshake(peers):
    barrier = pltpu.get_barrier_semaphore()
    for peer in peers:
        pl.semaphore_signal(barrier, inc=1, device_id=peer, device_id_type=MESH)
    pl.semaphore_wait(barrier, len(peers))


def _add_sibling(parts, got, c_idx, name):
    _, R, C = got.shape
    tr, tc = _tile(R, 512), _tile(C, 1024)

    def body(c_ref, p_ref, g_ref, o_ref):
        o_ref[...] = (p_ref[...].astype(F32) + g_ref[...].astype(F32)).astype(o_ref.dtype)

    if parts.ndim == 3:
        parts_spec = pl.BlockSpec((None, tr, tc), lambda k, i, j, c_ref: (2 * k + c_ref[0], i, j))
    else:
        parts_spec = pl.BlockSpec((tr, tc), lambda k, i, j, c_ref: (i, (2 * k + c_ref[0]) * (C // tc) + j))
    grid_spec = pltpu.PrefetchScalarGridSpec(
        num_scalar_prefetch=1, grid=(N_CHIP, R // tr, C // tc),
        in_specs=[parts_spec, pl.BlockSpec((None, tr, tc), lambda k, i, j, c_ref: (k, i, j))],
        out_specs=pl.BlockSpec((None, tr, tc), lambda k, i, j, c_ref: (k, i, j)))
    return pl.pallas_call(
        body, name=name, grid_spec=grid_spec, out_shape=jax.ShapeDtypeStruct((N_CHIP, R, C), parts.dtype),
        compiler_params=_params(("parallel", "parallel", "parallel")),
    )(c_idx, parts, got)


def _scatter_to_chips(sums, name, collective_id):
    _, R, C = sums.shape

    def body(q_ref, got_ref, send_sems, recv_sems, local_sem):
        x, y, c = _place()
        _handshake([(*chip, c) for chip in _other_chips(x, y)])
        my_chip = 2 * x + y
        mine = pltpu.make_async_copy(q_ref.at[my_chip], got_ref.at[my_chip], local_sem)
        mine.start()
        copies = [pltpu.make_async_remote_copy(
            src_ref=q_ref.at[2 * px + py], dst_ref=got_ref.at[my_chip], send_sem=send_sems.at[j],
            recv_sem=recv_sems.at[j], device_id=(px, py, c), device_id_type=MESH)
            for j, (px, py) in enumerate(_other_chips(x, y))]
        for cp in copies:
            cp.start()
        for cp in copies:
            cp.wait()
        mine.wait()

    return _sequencer_call(
        body, name, collective_id, jax.ShapeDtypeStruct((N_CHIP, R, C), sums.dtype),
        [pltpu.SemaphoreType.DMA((3,)), pltpu.SemaphoreType.DMA((3,)), pltpu.SemaphoreType.DMA], sums)


def _adamw_math(w, g, m, v):
    m = ADAM_B1 * m + (1.0 - ADAM_B1) * g
    v = ADAM_B2 * v + (1.0 - ADAM_B2) * (g * g)
    m_hat = m / (1.0 - ADAM_B1 ** ADAM_STEP)
    v_hat = v / (1.0 - ADAM_B2 ** ADAM_STEP)
    delta = -ADAM_LR * (m_hat / (jnp.sqrt(v_hat) + ADAM_EPS) + ADAM_WD * w)
    return delta, m, v


def _sum_adamw(parts, w, m, v, name):
    n_parts, R, C = parts.shape
    tr, tc = _tile(R, 256), _tile(C, 1024)

    def body(p_ref, w_ref, m_ref, v_ref, g_out, d_out, m_out, v_out):
        g = p_ref[0].astype(F32)
        for k in range(1, n_parts):
            g = g + p_ref[k].astype(F32)
        delta, mn, vn = _adamw_math(w_ref[...], g, m_ref[...], v_ref[...])
        g_out[...] = g
        d_out[...] = delta
        m_out[...] = mn
        v_out[...] = vn

    blk = pl.BlockSpec((tr, tc), lambda i, j: (i, j))
    shp = jax.ShapeDtypeStruct((R, C), F32)
    return pl.pallas_call(
        body, name=name, grid=(R // tr, C // tc),
        in_specs=[pl.BlockSpec((n_parts, tr, tc), lambda i, j: (0, i, j)), blk, blk, blk],
        out_specs=[blk, blk, blk, blk], out_shape=[shp, shp, shp, shp],
        compiler_params=_params(("parallel", "parallel")),
    )(parts, w, m, v)


def _after(value, dep):
    return lax.optimization_barrier((value, dep))[0]


def _rows128(a):
    return a.reshape(-1, 128)


def kernel(x, mix_norm_g, w_in, conv_w, spatial_w, spatial_b, conv_out_norm_g, gmlp_out_norm_g, w_out, mlp_norm_g, w_up, w_down, final_norm_g, loss_target, m_mix_norm_g, m_w_in, m_conv_w, m_spatial_w, m_spatial_b, m_conv_out_norm_g, m_gmlp_out_norm_g, m_w_out, m_mlp_norm_g, m_w_up, m_w_down, m_final_norm_g, v_mix_norm_g, v_w_in, v_conv_w, v_spatial_w, v_spatial_b, v_conv_out_norm_g, v_gmlp_out_norm_g, v_w_out, v_mlp_norm_g, v_w_up, v_w_down, v_final_norm_g):
    n_seq, seq, D = x.shape
    T = n_seq * seq
    n_in = w_in.shape[2]
    n_out = w_out.shape[1]
    n_up = w_up.shape[2]
    wc = conv_w.shape[2] * N_DEV
    n_head = wc // HEAD
    FF = n_up * N_DEV
    assert N_DEV * n_in == 5 * wc and seq % HEAD == 0 and D == 2 * wc

    c_idx = lax.axis_index("c").astype(jnp.int32).reshape(1)
    my_dev = 4 * lax.axis_index("x") + 2 * lax.axis_index("y") + lax.axis_index("c")

    xf = x.reshape(T, D)
    tgt = loss_target.reshape(T, D)

    cw_pad = jnp.pad(conv_w[0], ((0, HALO - CONV_K), (0, 0)))
    cw_g = _all_gather(cw_pad, "ag_conv_w")
    conv_full = jnp.transpose(cw_g[:, :CONV_K, :], (1, 0, 2)).reshape(CONV_K, wc)
    cast = lambda w: _after(w[0].astype(BF16), cw_g)
    win_g = _all_gather_async(cast(w_in), "ag_w_in", 0, side_by_side=True)
    wout_g = _all_gather_async(cast(w_out), "ag_w_out", 1).reshape(D, D)
    wup_g = _all_gather_async(cast(w_up), "ag_w_up", 2, side_by_side=True)
    wdown_g = _all_gather_async(cast(w_down), "ag_w_down", 3).reshape(FF, D)

    sw = spatial_w[0]
    sbt = spatial_b[0].T
    g_mix, g_a, g_b, g_mlp = mix_norm_g, conv_out_norm_g, gmlp_out_norm_g, mlp_norm_g
    g_fin = final_norm_g.reshape(1, D)

    IN = N_DEV * n_in
    bp = _tile(T, 1024)
    bm = _tile(T, 1024)
    bn = _tile(D, 1024)
    bu = _tile(FF, 1024)
    bw = _tile(D, 512)
    bg = _tile(D, 1024)
    k_ff = _tile(FF, 4096)
    k_in = _tile(IN, 5120)

    def tile(shape, index):
        return pl.BlockSpec(shape, index)

    by_m0 = lambda n, m, k: (m, 0)
    by_0n = lambda n, m, k: (0, n)
    by_n0 = lambda n, m, k: (n, 0)
    by_mn = lambda n, m, k: (m, n)
    by_mk = lambda n, m, k: (m, k)
    by_kn = lambda n, m, k: (k, n)
    by_nk = lambda n, m, k: (n, k)
    by_0m = lambda n, m, k: (0, m)
    f32_td = [jax.ShapeDtypeStruct((T, D), F32)]

    xn = _rms_fwd(xf, g_mix, "norm_mix")
    proj = _matmul(
        "proj", (N_DEV, T // bp, 1), NN, [xn, win_g], [tile((bp, D), by_m0), tile((D, n_in), by_0n)],
        [jax.ShapeDtypeStruct((T, IN), F32)], [tile((bp, n_in), by_mn)], _ident)[0]
    y = _mixer_fwd(proj, conv_full, sw, sbt, g_a, g_b, n_seq, seq)
    h1 = _matmul(
        "out_proj", (D // bn, T // bp, 1), NN, [y, wout_g, xf],
        [tile((bp, D), by_m0), tile((D, bn), by_0n), tile((bp, bn), by_mn)],
        f32_td, [tile((bp, bn), by_mn)], lambda acc, res: (res + acc,))[0]
    xn2 = _rms_fwd(h1, g_mlp, "norm_mlp")

    def up_epilogue(acc):
        r = jnp.maximum(acc, 0.0)
        return acc, r * r

    up, act = _matmul(
        "up_proj", (FF // bu, T // bm, 1), NN, [xn2, wup_g], [tile((bm, D), by_m0), tile((D, bu), by_0n)],
        [jax.ShapeDtypeStruct((T, FF), BF16)] * 2, [tile((bm, bu), by_mn)] * 2, up_epilogue)
    mlp = _matmul(
        "down_proj", (D // bn, T // bm, FF // k_ff), NN, [act, wdown_g],
        [tile((bm, k_ff), by_mk), tile((k_ff, bn), by_kn)], f32_td, [tile((bm, bn), by_mn)], _ident)[0]

    dh2, dh2b, gg_fin, loss_row = _loss_head(h1, mlp, tgt, g_fin)
    loss = lax.psum(loss_row[0, 0], ("x", "y", "c"))
    dh2b = _after(dh2b, loss)

    gp_down = _matmul(
        "gw_down", (D // bn, FF // bg, 1), TN, [act, dh2b], [tile((T, bg), by_0m), tile((T, bn), by_0n)],
        [jax.ShapeDtypeStruct((FF, D), BF16)], [tile((bg, bn), by_mn)], _ident)[0].reshape(N_DEV, n_up, D)
    dup, got_down = _matmul(
        "d_act", (FF // bu, T // bm, 1), NT, [dh2b, wdown_g, up],
        [tile((bm, D), by_m0), tile((bu, D), by_n0), tile((bm, bu), by_mn)],
        [jax.ShapeDtypeStruct((T, FF), BF16)], [tile((bm, bu), by_mn)],
        lambda acc, u: (acc * (2.0 * jnp.maximum(u.astype(F32), 0.0)),), swap=(gp_down, None))
    dxn2, sums_down = _matmul(
        "d_xn2", (D // bn, T // bm, FF // k_ff), NT, [dup, wup_g],
        [tile((bm, k_ff), by_mk), tile((bn, k_ff), by_nk)], f32_td, [tile((bm, bn), by_mn)], _ident,
        side=_add_job(gp_down, got_down, c_idx, (D // bn) * (T // bm) * (FF // k_ff)))
    sums_down = sums_down.reshape(N_CHIP, n_up, D)
    four_down = _scatter_to_chips(sums_down, "rs_chips_w_down", 4)
    gp_up = _matmul(
        "gw_up", (FF // bu, D // bg, 1), TN, [xn2, _after(dup, sums_down)],
        [tile((T, bg), by_0m), tile((T, bu), by_0n)],
        [jax.ShapeDtypeStruct((D, FF), BF16)], [tile((bg, bu), by_mn)], _ident)[0]
    dh1, dh1b, gg_mlp = _rms_bwd(_after(dxn2, gp_up), h1, g_mlp, dh2, "norm_mlp_bwd", True)

    dy, got_up = _matmul(
        "d_y", (D // bn, T // bm, 1), NT, [dh1b, wout_g], [tile((bm, D), by_m0), tile((bn, D), by_n0)],
        f32_td, [tile((bm, bn), by_mn)], _ident, swap=(gp_up, n_up))
    gp_out, sums_up = _matmul(
        "gw_out", (D // bn, D // bw, 1), TN, [y, _after(dh1b, dy)], [tile((T, bw), by_0m), tile((T, bn), by_0n)],
        [jax.ShapeDtypeStruct((D, D), BF16)], [tile((bw, bn), by_mn)], _ident,
        side=_add_job(gp_up, got_up, c_idx, (D // bn) * (D // bw)))
    gp_out = gp_out.reshape(N_DEV, n_out, D)
    sums_up = sums_up.reshape(N_CHIP, D, n_up)
    four_up = _scatter_to_chips(sums_up, "rs_chips_w_up", 5)
    dproj, gl_conv, gl_a, gl_b, gl_sw, gl_sbt = _mixer_bwd(
        proj, _after(dy, sums_up), conv_full, sw, sbt, g_a, g_b, n_seq, seq)
    gp_in, got_out = _matmul(
        "gw_in", (N_DEV, D // bw, 1), TN, [xn, dproj], [tile((T, bw), by_0m), tile((T, n_in), by_0n)],
        [jax.ShapeDtypeStruct((D, IN), BF16)], [tile((bw, n_in), by_mn)], _ident, swap=(gp_out, None))
    sums_out = _add_sibling(gp_out, got_out, c_idx, "rs_add_w_out")
    four_out = _scatter_to_chips(sums_out, "rs_chips_w_out", 6)
    bh = _tile(T // 2, bm)
    hm = (T // 2) // bh
    dproj = _after(dproj, sums_out)
    dxn, got_in = _matmul(
        "d_xn_top", (D // bn, hm, IN // k_in), NT, [dproj, win_g],
        [tile((bh, k_in), by_mk), tile((bn, k_in), by_nk)], f32_td, [tile((bh, bn), by_mn)], _ident,
        swap=(gp_in, n_in))
    sums_in = _add_sibling(gp_in, got_in, c_idx, "rs_add_w_in")
    four_in = _scatter_to_chips(sums_in, "rs_chips_w_in", 7)
    dxn = _matmul(
        "d_xn_bottom", (D // bn, hm, IN // k_in), NT, [_after(dproj, sums_in), win_g],
        [tile((bh, k_in), lambda n, m, k: (m + hm, k)), tile((bn, k_in), by_nk)], f32_td,
        [tile((bh, bn), lambda n, m, k: (m + hm, n))], _ident, fill=dxn)[0]
    grad_x, gg_mix = _rms_bwd(_after(dxn, sums_in), xf, g_mix, dh1, "norm_mix_bwd", False)

    outs = {}
    done = grad_x
    for tag, four, w, m, v in (("w_down", four_down, w_down, m_w_down, v_w_down),
                               ("w_up", four_up, w_up, m_w_up, v_w_up),
                               ("w_out", four_out, w_out, m_w_out, v_w_out),
                               ("w_in", four_in, w_in, m_w_in, v_w_in)):
        res = _sum_adamw(_after(four, done), w[0], m[0], v[0], "adamw_" + tag)
        done = res[0]
        outs[tag] = [a[None] for a in res]

    small = [("mix_norm_g", gg_mix, mix_norm_g, m_mix_norm_g, v_mix_norm_g),
             ("conv_w", gl_conv, None, None, None),
             ("spatial_w", gl_sw, spatial_w, m_spatial_w, v_spatial_w),
             ("spatial_b", gl_sbt.T, spatial_b, m_spatial_b, v_spatial_b),
             ("conv_out_norm_g", gl_a, conv_out_norm_g, m_conv_out_norm_g, v_conv_out_norm_g),
             ("gmlp_out_norm_g", gl_b, gmlp_out_norm_g, m_gmlp_out_norm_g, v_gmlp_out_norm_g),
             ("mlp_norm_g", gg_mlp, mlp_norm_g, m_mlp_norm_g, v_mlp_norm_g),
             ("final_norm_g", gg_fin, final_norm_g, m_final_norm_g, v_final_norm_g)]
    packed_g = jnp.concatenate([_rows128(g) for _, g, _, _, _ in small], axis=0)
    zeros_cw = jnp.zeros((CONV_K * wc // 128, 128), F32)
    pack = lambda idx: jnp.concatenate(
        [zeros_cw if item[2] is None else _rows128(item[idx]) for item in small], axis=0)
    all_g = _all_gather_async(packed_g, "ag_small_grads", 8)
    sg, sd, sm, sv = _sum_adamw(_after(all_g, done), pack(2), pack(3), pack(4), "adamw_small")
    row = 0
    for name, g, w, _, _ in small:
        n_rows = g.size // 128
        if w is not None:
            outs[name] = [a[row:row + n_rows].reshape(w.shape) for a in (sg, sd, sm, sv)]
        else:
            conv_grad_full = sg[row:row + n_rows].reshape(CONV_K, wc)
        row += n_rows
    cpd = wc // N_DEV
    conv_grad = lax.dynamic_slice(conv_grad_full, (0, my_dev * cpd), (CONV_K, cpd))
    pad8 = lambda a: jnp.pad(a, ((0, HALO - CONV_K), (0, 0)))
    outs["conv_w"] = [a[:CONV_K][None] for a in _sum_adamw(
        pad8(conv_grad)[None], pad8(conv_w[0]), pad8(m_conv_w[0]), pad8(v_conv_w[0]), "adamw_conv_w")]

    order = ["mix_norm_g", "w_in", "conv_w", "spatial_w", "spatial_b", "conv_out_norm_g", "gmlp_out_norm_g",
             "w_out", "mlp_norm_g", "w_up", "w_down", "final_norm_g"]
    result = [loss, grad_x.reshape(n_seq, seq, D)]
    for k in range(4):
        result += [outs[n][k] for n in order]
    return tuple(result)
```

```python
import functools
import math
from typing import Callable, NamedTuple

import jax
import jax.numpy as jnp
from jax import lax
from jax.experimental import pallas as pl
from jax.experimental.pallas import tpu as pltpu
from jax.experimental.pallas import tpu_sc as plsc

F32 = jnp.float32
BF16 = jnp.bfloat16
MESH = pl.DeviceIdType.MESH
HBM = pltpu.HBM

EPS = 1e-5
HEAD = 128
CONV_K = 3
N_DEV = 8
N_CHIP = 4
VMEM_LIMIT_BYTES = 62 * 1024 * 1024

ADAM_LR = 0.001
ADAM_B1 = 0.9
ADAM_B2 = 0.999
ADAM_EPS = 1e-08
ADAM_WD = 0.01
ADAM_STEP = 10

GELU_K0 = math.sqrt(2.0 / math.pi)
GELU_K1 = 0.044715

NN = (((1,), (0,)), ((), ()))
NT = (((1,), (1,)), ((), ()))
TN = (((0,), (0,)), ((), ()))


def _params(semantics):
    return pltpu.CompilerParams(dimension_semantics=semantics, vmem_limit_bytes=VMEM_LIMIT_BYTES)


def _tile(dim, want):
    if dim <= want:
        return dim
    for t in range(want - want % 8, 0, -8):
        if dim % t == 0:
            return t
    raise ValueError((dim, want))


def _owner_block(ref, j, cols):
    if cols is None:
        return ref.at[j]
    return ref.at[:, pl.ds(pl.multiple_of(j * cols, 128), cols)]


def _sibling_copies(p_ref, got_ref, send_sems, recv_sems, cols):
    x, y, c = _place()
    return [pltpu.make_async_remote_copy(
        src_ref=_owner_block(p_ref, 2 * k + (1 - c), cols), dst_ref=got_ref.at[k], send_sem=send_sems.at[k],
        recv_sem=recv_sems.at[k], device_id=(x, y, 1 - c), device_id_type=MESH) for k in range(N_CHIP)]


class _SideJob(NamedTuple):
    c_idx: jax.Array
    arrays: list
    in_tiles: list
    out_shapes: list
    out_tiles: list
    fn: Callable


def _matmul(name, grid, dims, operands, in_specs, out_shapes, out_specs, epilogue, swap=None, side=None, fill=None):
    n_in = len(operands)
    n_out = len(out_shapes)
    nk = grid[2]
    n_host = 0 if swap is None else 1
    n_si = 0 if side is None else len(side.arrays)
    n_so = 0 if side is None else len(side.out_shapes)
    n_fill = 0 if fill is None else 1
    assert nk == 1 or (n_out == 1 and epilogue is _ident and out_shapes[0].dtype == F32)
    assert fill is None or side is None

    def body(*refs):
        if side is not None:
            refs = refs[1:]
        a_ref, b_ref = refs[0], refs[1]
        extra = refs[2:n_in]
        first_out = n_in + n_si + n_host + n_fill
        outs = refs[first_out:first_out + n_out]
        if side is not None:
            res = side.fn(*[r[...] for r in refs[n_in:n_in + n_si]])
            for o, r in zip(refs[first_out + n_out:first_out + n_out + n_so], res):
                o[...] = r
        if swap is not None:
            ids = [pl.program_id(d) for d in range(3)]
            copies = _sibling_copies(refs[n_in + n_si], refs[first_out + n_out + n_so], refs[-2], refs[-1], swap[1])

            @pl.when(functools.reduce(jnp.logical_and, [i == 0 for i in ids]))
            def _():
                for cp in copies:
                    cp.start()

        part = lax.dot_general(a_ref[...], b_ref[...], dims, preferred_element_type=F32)

        def finish(acc):
            res = epilogue(acc, *[e[...] for e in extra])
            for o, r in zip(outs, res):
                o[...] = r.astype(o.dtype)

        if nk == 1:
            finish(part)
        else:
            k = pl.program_id(2)

            @pl.when(k == 0)
            def _():
                outs[0][...] = part

            @pl.when(k > 0)
            def _():
                outs[0][...] += part

        if swap is not None:
            @pl.when(functools.reduce(jnp.logical_and, [i == n - 1 for i, n in zip(ids, grid)]))
            def _():
                for cp in copies:
                    cp.wait()

    scratch = []
    semantics = ("parallel", "parallel", "arbitrary")
    operands, in_specs = list(operands), list(in_specs)
    out_shapes, out_specs = list(out_shapes), list(out_specs)
    if side is not None:
        def with_c(spec):
            return pl.BlockSpec(spec.block_shape, lambda n, m, k, c_ref, f=spec.index_map: f(n, m, k))

        def side_tile(shape, index):
            return pl.BlockSpec(shape, lambda n, m, k, c_ref: index((n * grid[1] + m) * grid[2] + k, c_ref[0]))

        in_specs = [with_c(s) for s in in_specs] + [side_tile(*t) for t in side.in_tiles]
        out_specs = [with_c(s) for s in out_specs] + [side_tile(*t) for t in side.out_tiles]
        operands += side.arrays
        out_shapes += side.out_shapes
    if swap is not None:
        parts, cols = swap
        got_shape = parts.shape[1:] if cols is None else (parts.shape[0], cols)
        hbm = pl.BlockSpec(memory_space=HBM)
        operands, in_specs = operands + [parts], in_specs + [hbm]
        out_shapes = out_shapes + [jax.ShapeDtypeStruct((N_CHIP,) + got_shape, parts.dtype)]
        out_specs = out_specs + [hbm]
        scratch += [pltpu.SemaphoreType.DMA((N_CHIP,)), pltpu.SemaphoreType.DMA((N_CHIP,))]
        semantics = ("arbitrary", "arbitrary", "arbitrary")
    aliases = {}
    if fill is not None:
        aliases = {len(operands): 0}
        operands, in_specs = operands + [fill], in_specs + [pl.BlockSpec(memory_space=pl.ANY)]
    if side is None:
        return pl.pallas_call(
            body, name=name, grid=grid, in_specs=in_specs, out_specs=out_specs, out_shape=out_shapes,
            scratch_shapes=scratch, input_output_aliases=aliases, compiler_params=_params(semantics),
        )(*operands)
    grid_spec = pltpu.PrefetchScalarGridSpec(
        num_scalar_prefetch=1, grid=grid, in_specs=in_specs, out_specs=out_specs, scratch_shapes=scratch)
    return pl.pallas_call(
        body, name=name, grid_spec=grid_spec, out_shape=out_shapes, compiler_params=_params(semantics),
    )(side.c_idx, *operands)


def _add_job(parts, got, c_idx, steps):
    _, R, C = got.shape
    rows = N_CHIP * R
    tr = rows // steps
    assert rows % steps == 0 and R % tr == 0 and tr % 16 == 0, (rows, steps)
    per = R // tr
    if parts.ndim == 3:
        flat = parts.reshape(N_DEV * R, C)
        parts_index = lambda s, c: ((2 * (s // per) + c) * per + s % per, 0)
    else:
        flat = parts
        parts_index = lambda s, c: (s % per, 2 * (s // per) + c)
    by_step = lambda s, c: (s, 0)
    return _SideJob(
        c_idx, [flat, got.reshape(rows, C)], [((tr, C), parts_index), ((tr, C), by_step)],
        [jax.ShapeDtypeStruct((rows, C), parts.dtype)], [((tr, C), by_step)],
        lambda p, g: ((p.astype(F32) + g.astype(F32)).astype(p.dtype),))


def _ident(acc):
    return (acc,)


ROW_TILE = 256


def _rms_fwd(x, g, name):
    T, D = x.shape
    tr = _tile(T, ROW_TILE)

    def body(x_ref, g_ref, o_ref):
        xv = x_ref[...]
        inv = lax.rsqrt(jnp.mean(xv * xv, axis=-1, keepdims=True) + EPS)
        o_ref[...] = (xv * inv * g_ref[...]).astype(o_ref.dtype)

    return pl.pallas_call(
        body, name=name, grid=(T // tr,),
        in_specs=[pl.BlockSpec((tr, D), lambda i: (i, 0)), pl.BlockSpec((1, D), lambda i: (0, 0))],
        out_specs=pl.BlockSpec((tr, D), lambda i: (i, 0)),
        out_shape=jax.ShapeDtypeStruct((T, D), BF16),
        compiler_params=_params(("parallel",)),
    )(x, g)


def _rms_bwd(dy, x, g, dres, name, want_bf16):
    T, D = x.shape
    tr = _tile(T, ROW_TILE)

    def body(dy_ref, x_ref, g_ref, dres_ref, *outs):
        dx_ref, gg_ref = outs[0], outs[-1]
        i = pl.program_id(0)
        xv = x_ref[...]
        dyv = dy_ref[...]
        inv = lax.rsqrt(jnp.mean(xv * xv, axis=-1, keepdims=True) + EPS)
        gd = dyv * g_ref[...]
        dot = jnp.mean(gd * xv, axis=-1, keepdims=True)
        dx = dres_ref[...] + (inv * gd - xv * (inv * inv * inv * dot))
        dx_ref[...] = dx
        if want_bf16:
            outs[1][...] = dx.astype(BF16)
        part = jnp.sum(dyv * xv * inv, axis=0, keepdims=True)

        @pl.when(i == 0)
        def _():
            gg_ref[...] = part

        @pl.when(i > 0)
        def _():
            gg_ref[...] += part

    row = pl.BlockSpec((tr, D), lambda i: (i, 0))
    vec = pl.BlockSpec((1, D), lambda i: (0, 0))
    out_shape = [jax.ShapeDtypeStruct((T, D), F32)]
    out_specs = [row]
    if want_bf16:
        out_shape.append(jax.ShapeDtypeStruct((T, D), BF16))
        out_specs.append(row)
    out_shape.append(jax.ShapeDtypeStruct((1, D), F32))
    out_specs.append(vec)
    return pl.pallas_call(
        body, name=name, grid=(T // tr,), in_specs=[row, row, vec, row],
        out_specs=out_specs, out_shape=out_shape, compiler_params=_params(("arbitrary",)),
    )(dy, x, g, dres)


def _loss_head(h1, mlp, target, g):
    T, D = h1.shape
    tr = _tile(T, ROW_TILE)

    def body(h1_ref, mlp_ref, t_ref, g_ref, dh_ref, dhb_ref, gg_ref, loss_ref):
        i = pl.program_id(0)
        hv = h1_ref[...] + mlp_ref[...]
        gv = g_ref[...]
        inv = lax.rsqrt(jnp.mean(hv * hv, axis=-1, keepdims=True) + EPS)
        diff = hv * inv * gv - t_ref[...]
        lpart = 0.5 * jnp.sum(jnp.mean(diff * diff, axis=-1, keepdims=True), axis=0, keepdims=True)
        dout = diff * (1.0 / D)
        gd = dout * gv
        dot = jnp.mean(gd * hv, axis=-1, keepdims=True)
        dh = inv * gd - hv * (inv * inv * inv * dot)
        dh_ref[...] = dh
        dhb_ref[...] = dh.astype(BF16)
        part = jnp.sum(dout * hv * inv, axis=0, keepdims=True)
        lrow = jnp.broadcast_to(lpart, (1, 128))

        @pl.when(i == 0)
        def _():
            gg_ref[...] = part
            loss_ref[...] = lrow

        @pl.when(i > 0)
        def _():
            gg_ref[...] += part
            loss_ref[...] += lrow

    row = pl.BlockSpec((tr, D), lambda i: (i, 0))
    vec = pl.BlockSpec((1, D), lambda i: (0, 0))
    return pl.pallas_call(
        body, name="loss_head", grid=(T // tr,), in_specs=[row, row, row, vec],
        out_specs=[row, row, vec, pl.BlockSpec((1, 128), lambda i: (0, 0))],
        out_shape=[jax.ShapeDtypeStruct((T, D), F32), jax.ShapeDtypeStruct((T, D), BF16),
                   jax.ShapeDtypeStruct((1, D), F32), jax.ShapeDtypeStruct((1, 128), F32)],
        compiler_params=_params(("arbitrary",)),
    )(h1, mlp, target, g)


HALO = 8


def _gelu_parts(x):
    th = jnp.tanh(GELU_K0 * (x + GELU_K1 * (x * x * x)))
    return x * (0.5 * (1.0 + th)), th


def _gelu_grad(x, th):
    return 0.5 * (1.0 + th) + (0.5 * GELU_K0) * x * (1.0 - th * th) * (1.0 + (3.0 * GELU_K1) * (x * x))


def _conv_fwd(b_ref, c_ref, h_ref, ch_ref, hh_ref, w_ref, first):
    tt, wc = c_ref.shape
    c = c_ref[...]
    h = h_ref[...]
    hc = c * h
    prev1 = jnp.where(first, 0.0, ch_ref[HALO - 1:HALO, :] * hh_ref[HALO - 1:HALO, :])
    prev2 = jnp.where(first, 0.0, ch_ref[HALO - 2:HALO - 1, :] * hh_ref[HALO - 2:HALO - 1, :])
    row = lax.broadcasted_iota(jnp.int32, (tt, wc), 0)
    m1 = jnp.where(row == 0, prev1, pltpu.roll(hc, 1, 0))
    m2 = jnp.where(row == 0, prev2, jnp.where(row == 1, prev1, pltpu.roll(hc, 2, 0)))
    conv = w_ref[0:1, :] * m2 + w_ref[1:2, :] * m1 + w_ref[2:3, :] * hc
    return c, h, hc, m1, m2, conv, b_ref[...] * conv


def _tril():
    r = lax.broadcasted_iota(jnp.int32, (HEAD, HEAD), 0)
    s = lax.broadcasted_iota(jnp.int32, (HEAD, HEAD), 1)
    return r >= s


def _spatial_fwd(gvb, sw_ref, sbt_ref, s_scr):
    n_head = sw_ref.shape[0]
    tri = _tril()
    for hd in range(n_head):
        sl = slice(hd * HEAD, (hd + 1) * HEAD)
        wm = jnp.where(tri, sw_ref[hd], 0.0).astype(BF16)
        s_scr[:, sl] = jnp.dot(wm, gvb[:, sl], preferred_element_type=F32) + sbt_ref[:, hd:hd + 1]


def _mixer_specs(n_tiles, wc, row_of, n_grid):
    def grp(g):
        return pl.BlockSpec((HEAD, wc), lambda *ids: (row_of(*ids), g))

    def halo(g):
        return pl.BlockSpec((HALO, wc), lambda *ids: (jnp.maximum(row_of(*ids) * (HEAD // HALO) - 1, 0), g))

    return grp, halo


def _mixer_fwd(proj, conv_w, sw, sbt, g_a, g_b, n_seq, seq):
    T, w5 = proj.shape
    wc = w5 // 5
    n_head = wc // HEAD
    nt = seq // HEAD

    def row_of(s, i):
        return s * nt + i

    grp, halo = _mixer_specs(nt, wc, row_of, 2)

    def body(b_ref, c_ref, h_ref, u_ref, v_ref, ch_ref, hh_ref, w_ref, sw_ref, sbt_ref, ga_ref, gb_ref,
             y_ref, s_scr):
        i = pl.program_id(1)
        ya = _conv_fwd(b_ref, c_ref, h_ref, ch_ref, hh_ref, w_ref, i == 0)[-1]
        inv = lax.rsqrt(jnp.mean(ya * ya, axis=-1, keepdims=True) + EPS)
        y_ref[:, :wc] = (ya * inv * ga_ref[...]).astype(BF16)
        gu, _ = _gelu_parts(u_ref[...])
        gv, _ = _gelu_parts(v_ref[...])
        _spatial_fwd(gv.astype(BF16), sw_ref, sbt_ref, s_scr)
        yb = gu * s_scr[...]
        inv = lax.rsqrt(jnp.mean(yb * yb, axis=-1, keepdims=True) + EPS)
        y_ref[:, wc:] = (yb * inv * gb_ref[...]).astype(BF16)

    const2 = lambda shape: pl.BlockSpec(shape, lambda s, i: (0, 0))
    return pl.pallas_call(
        body, name="mixer_fwd", grid=(n_seq, nt),
        in_specs=[grp(0), grp(1), grp(2), grp(3), grp(4), halo(1), halo(2),
                  const2((CONV_K, wc)), pl.BlockSpec((n_head, HEAD, HEAD), lambda s, i: (0, 0, 0)),
                  const2((HEAD, n_head)), const2((1, wc)), const2((1, wc))],
        out_specs=pl.BlockSpec((HEAD, 2 * wc), lambda s, i: (s * nt + i, 0)),
        out_shape=jax.ShapeDtypeStruct((T, 2 * wc), BF16),
        scratch_shapes=[pltpu.VMEM((HEAD, wc), F32)],
        compiler_params=_params(("parallel", "parallel")),
    )(proj, proj, proj, proj, proj, proj, proj, conv_w, sw, sbt, g_a, g_b)


def _mixer_bwd(proj, dy, conv_w, sw, sbt, g_a, g_b, n_seq, seq):
    T, w5 = proj.shape
    wc = w5 // 5
    n_head = wc // HEAD
    nt = seq // HEAD
    tt = HEAD

    def row_of(s, ir):
        return s * nt + (nt - 1 - ir)

    grp, halo = _mixer_specs(nt, wc, row_of, 2)

    def body(b_ref, c_ref, h_ref, u_ref, v_ref, ch_ref, hh_ref, dya_ref, dyb_ref, w_ref, sw_ref, sbt_ref,
             ga_ref, gb_ref, dp_ref, gw_ref, gga_ref, ggb_ref, gsw_ref, gsb_ref,
             carry_scr, s_scr, t_scr, dsum_scr):
        s_id = pl.program_id(0)
        ir = pl.program_id(1)
        first_tile = jnp.logical_and(s_id == 0, ir == 0)
        last_tile = jnp.logical_and(s_id == n_seq - 1, ir == nt - 1)

        def conv_part():
            @pl.when(ir == 0)
            def _():
                carry_scr[...] = jnp.zeros_like(carry_scr)

            c, h, hc, m1, m2, conv, ya = _conv_fwd(b_ref, c_ref, h_ref, ch_ref, hh_ref, w_ref, ir == nt - 1)
            inv = lax.rsqrt(jnp.mean(ya * ya, axis=-1, keepdims=True) + EPS)
            dyn = dya_ref[...]
            gd = dyn * ga_ref[...]
            dot = jnp.mean(gd * ya, axis=-1, keepdims=True)
            dya = inv * gd - ya * (inv * inv * inv * dot)
            gg = jnp.sum(dyn * ya * inv, axis=0, keepdims=True)
            dconv = dya * b_ref[...]
            nxt0 = carry_scr[0:1, :]
            nxt1 = carry_scr[1:2, :]
            row = lax.broadcasted_iota(jnp.int32, (tt, wc), 0)
            p1 = jnp.where(row == tt - 1, nxt0, pltpu.roll(dconv, tt - 1, 0))
            p2 = jnp.where(row == tt - 2, nxt0, jnp.where(row == tt - 1, nxt1, pltpu.roll(dconv, tt - 2, 0)))
            dhc = w_ref[2:3, :] * dconv + w_ref[1:2, :] * p1 + w_ref[0:1, :] * p2
            carry_scr[...] = dconv[0:HALO, :]
            dp_ref[:, 0:wc] = (dya * conv).astype(BF16)
            dp_ref[:, wc:2 * wc] = (dhc * h).astype(BF16)
            dp_ref[:, 2 * wc:3 * wc] = (dhc * c).astype(BF16)
            gw0 = jnp.sum(dconv * m2, axis=0, keepdims=True)
            gw1 = jnp.sum(dconv * m1, axis=0, keepdims=True)
            gw2 = jnp.sum(dconv * hc, axis=0, keepdims=True)

            @pl.when(first_tile)
            def _():
                gw_ref[0:1, :] = gw0
                gw_ref[1:2, :] = gw1
                gw_ref[2:3, :] = gw2
                gga_ref[...] = gg

            @pl.when(jnp.logical_not(first_tile))
            def _():
                gw_ref[0:1, :] += gw0
                gw_ref[1:2, :] += gw1
                gw_ref[2:3, :] += gw2
                gga_ref[...] += gg

        def gate_part():
            u = u_ref[...]
            v = v_ref[...]
            gu, thu = _gelu_parts(u)
            gv, thv = _gelu_parts(v)
            gvb = gv.astype(BF16)
            _spatial_fwd(gvb, sw_ref, sbt_ref, s_scr)
            sv = s_scr[...]
            yb = gu * sv
            inv = lax.rsqrt(jnp.mean(yb * yb, axis=-1, keepdims=True) + EPS)
            dyn = dyb_ref[...]
            gd = dyn * gb_ref[...]
            dot = jnp.mean(gd * yb, axis=-1, keepdims=True)
            dyb = inv * gd - yb * (inv * inv * inv * dot)
            gg = jnp.sum(dyn * yb * inv, axis=0, keepdims=True)
            ds = dyb * gu
            dsb = ds.astype(BF16)
            tri = _tril()

            @pl.when(first_tile)
            def _():
                ggb_ref[...] = gg
                dsum_scr[...] = ds
                gsw_ref[...] = jnp.zeros_like(gsw_ref)

            @pl.when(jnp.logical_not(first_tile))
            def _():
                ggb_ref[...] += gg
                dsum_scr[...] += ds

            for hd in range(n_head):
                sl = slice(hd * HEAD, (hd + 1) * HEAD)
                wm = jnp.where(tri, sw_ref[hd], 0.0).astype(BF16)
                t_scr[:, sl] = lax.dot_general(wm, dsb[:, sl], TN, preferred_element_type=F32)
                gsw_ref[hd] += lax.dot_general(dsb[:, sl], gvb[:, sl], NT, preferred_element_type=F32)
            dp_ref[:, 3 * wc:4 * wc] = (dyb * sv * _gelu_grad(u, thu)).astype(BF16)
            dp_ref[:, 4 * wc:5 * wc] = (t_scr[...] * _gelu_grad(v, thv)).astype(BF16)

            @pl.when(last_tile)
            def _():
                for hd in range(n_head):
                    sl = slice(hd * HEAD, (hd + 1) * HEAD)
                    gsw_ref[hd] = jnp.where(tri, gsw_ref[hd], 0.0)
                    gsb_ref[:, hd:hd + 1] = jnp.sum(dsum_scr[:, sl], axis=1, keepdims=True)

        conv_part()
        gate_part()

    const2 = lambda shape: pl.BlockSpec(shape, lambda s, i: (0, 0))
    const3 = pl.BlockSpec((n_head, HEAD, HEAD), lambda s, i: (0, 0, 0))
    dy_spec = lambda col: pl.BlockSpec((tt, wc), lambda s, ir: (row_of(s, ir), col))
    return pl.pallas_call(
        body, name="mixer_bwd", grid=(n_seq, nt),
        in_specs=[grp(0), grp(1), grp(2), grp(3), grp(4), halo(1), halo(2), dy_spec(0), dy_spec(1),
                  const2((CONV_K, wc)), const3, const2((HEAD, n_head)), const2((1, wc)), const2((1, wc))],
        out_specs=[pl.BlockSpec((tt, 5 * wc), lambda s, ir: (row_of(s, ir), 0)),
                   const2((CONV_K, wc)), const2((1, wc)), const2((1, wc)), const3, const2((HEAD, n_head))],
        out_shape=[jax.ShapeDtypeStruct((T, 5 * wc), BF16), jax.ShapeDtypeStruct((CONV_K, wc), F32),
                   jax.ShapeDtypeStruct((1, wc), F32), jax.ShapeDtypeStruct((1, wc), F32),
                   jax.ShapeDtypeStruct((n_head, HEAD, HEAD), F32), jax.ShapeDtypeStruct((HEAD, n_head), F32)],
        scratch_shapes=[pltpu.VMEM((HALO, wc), F32), pltpu.VMEM((tt, wc), F32), pltpu.VMEM((tt, wc), F32),
                        pltpu.VMEM((tt, wc), F32)],
        compiler_params=_params(("arbitrary", "arbitrary")),
    )(proj, proj, proj, proj, proj, proj, proj, dy, dy, conv_w, sw, sbt, g_a, g_b)


def _place():
    return lax.axis_index("x"), lax.axis_index("y"), lax.axis_index("c")


def _other_chips(x, y):
    return [(1 - x, y), (x, 1 - y), (1 - x, 1 - y)]


def _all_gather_async(blk, name, collective_id, side_by_side=False):
    cols = blk.shape[1] if side_by_side else None
    out_shape = (blk.shape[0], N_DEV * cols) if side_by_side else (N_DEV,) + blk.shape

    def body(x_ref, out_ref, send_sems, recv_sems, local_sem):
        x, y, c = _place()
        me, sibling = (x, y, c), (x, y, 1 - c)
        x_nbr, y_nbr, diagonal = (1 - x, y, c), (x, 1 - y, c), (1 - x, 1 - y, c)
        near = (c * x + (1 - c) * (1 - x), c * (1 - y) + (1 - c) * y, c)
        far = ((1 - c) * x + c * (1 - x), (1 - c) * (1 - y) + c * y, c)
        _handshake([sibling, x_nbr, y_nbr])

        def slot(px, py, pc):
            return _owner_block(out_ref, 4 * px + 2 * py + pc, cols)

        def copy(k, block, to, src=None):
            return pltpu.make_async_remote_copy(
                src_ref=slot(*block) if src is None else src, dst_ref=slot(*block),
                send_sem=send_sems.at[k], recv_sem=recv_sems.at[k], device_id=to, device_id_type=MESH)

        mine = pltpu.make_async_copy(x_ref, slot(*me), local_sem)
        mine.start()
        sent = [copy(0, me, sibling, src=x_ref), copy(1, me, x_nbr, src=x_ref), copy(2, me, y_nbr, src=x_ref)]
        for cp in sent:
            cp.start()
        copy(1 + c, near, me).wait_recv()
        sent += [copy(3, near, far), copy(4, near, sibling)]
        sent[-2].start()
        sent[-1].start()
        copy(2 - c, far, me).wait_recv()
        sent.append(copy(5, far, sibling))
        sent[-1].start()
        copy(3, diagonal, me).wait_recv()
        sent.append(copy(6, diagonal, sibling))
        sent[-1].start()
        for k in (0, 4, 5, 6):
            copy(k, sibling, me).wait_recv()
        for cp in sent:
            cp.wait_send()
        mine.wait()

    return _sequencer_call(
        body, name, collective_id, jax.ShapeDtypeStruct(out_shape, blk.dtype),
        [pltpu.SemaphoreType.DMA((7,)), pltpu.SemaphoreType.DMA((7,)), pltpu.SemaphoreType.DMA], blk)


def _sequencer_call(body, name, collective_id, out_type, scratch_types, operand):
    return pl.kernel(
        body, name=name, out_type=out_type, mesh=plsc.ScalarSubcoreMesh(axis_name="seq_core", num_cores=1),
        scratch_types=scratch_types, compiler_params=pltpu.CompilerParams(collective_id=collective_id),
    )(operand)


def _handshake(peers):
    barrier = pltpu.get_barrier_semaphore()
    for peer in peers:
        pl.semaphore_signal(barrier, inc=1, device_id=peer, device_id_type=MESH)
    pl.semaphore_wait(barrier, len(peers))


def _add_sibling(parts, got, c_idx, name):
    _, R, C = got.shape
    tr, tc = _tile(R, 1024), _tile(C, 2048)

    def body(c_ref, p_ref, g_ref, o_ref):
        o_ref[...] = (p_ref[...].astype(F32) + g_ref[...].astype(F32)).astype(o_ref.dtype)

    if parts.ndim == 3:
        parts_spec = pl.BlockSpec((None, tr, tc), lambda k, i, j, c_ref: (2 * k + c_ref[0], i, j))
    else:
        parts_spec = pl.BlockSpec((tr, tc), lambda k, i, j, c_ref: (i, (2 * k + c_ref[0]) * (C // tc) + j))
    grid_spec = pltpu.PrefetchScalarGridSpec(
        num_scalar_prefetch=1, grid=(N_CHIP, R // tr, C // tc),
        in_specs=[parts_spec, pl.BlockSpec((None, tr, tc), lambda k, i, j, c_ref: (k, i, j))],
        out_specs=pl.BlockSpec((None, tr, tc), lambda k, i, j, c_ref: (k, i, j)))
    return pl.pallas_call(
        body, name=name, grid_spec=grid_spec, out_shape=jax.ShapeDtypeStruct((N_CHIP, R, C), parts.dtype),
        compiler_params=_params(("parallel", "parallel", "parallel")),
    )(c_idx, parts, got)


def _scatter_to_chips(sums, name, collective_id):
    _, R, C = sums.shape

    def body(q_ref, got_ref, send_sems, recv_sems, local_sem):
        x, y, c = _place()
        _handshake([(*chip, c) for chip in _other_chips(x, y)])
        my_chip = 2 * x + y
        mine = pltpu.make_async_copy(q_ref.at[my_chip], got_ref.at[my_chip], local_sem)
        mine.start()
        copies = [pltpu.make_async_remote_copy(
            src_ref=q_ref.at[2 * px + py], dst_ref=got_ref.at[my_chip], send_sem=send_sems.at[j],
            recv_sem=recv_sems.at[j], device_id=(px, py, c), device_id_type=MESH)
            for j, (px, py) in enumerate(_other_chips(x, y))]
        for cp in copies:
            cp.start()
        for cp in copies:
            cp.wait()
        mine.wait()

    return _sequencer_call(
        body, name, collective_id, jax.ShapeDtypeStruct((N_CHIP, R, C), sums.dtype),
        [pltpu.SemaphoreType.DMA((3,)), pltpu.SemaphoreType.DMA((3,)), pltpu.SemaphoreType.DMA], sums)


def _adamw_math(w, g, m, v):
    m = ADAM_B1 * m + (1.0 - ADAM_B1) * g
    v = ADAM_B2 * v + (1.0 - ADAM_B2) * (g * g)
    m_hat = m / (1.0 - ADAM_B1 ** ADAM_STEP)
    v_hat = v / (1.0 - ADAM_B2 ** ADAM_STEP)
    delta = -ADAM_LR * (m_hat / (jnp.sqrt(v_hat) + ADAM_EPS) + ADAM_WD * w)
    return delta, m, v


def _sum_adamw_tiles(p, w, m, v):
    g = p[0].astype(F32)
    for k in range(1, p.shape[0]):
        g = g + p[k].astype(F32)
    delta, mn, vn = _adamw_math(w, g, m, v)
    return g, delta, mn, vn


def _sum_adamw(parts, w, m, v, name):
    n_parts, R, C = parts.shape
    tr, tc = _tile(R, 512), _tile(C, 1024)

    def body(p_ref, w_ref, m_ref, v_ref, g_out, d_out, m_out, v_out):
        res = _sum_adamw_tiles(p_ref[...], w_ref[...], m_ref[...], v_ref[...])
        for o, r in zip((g_out, d_out, m_out, v_out), res):
            o[...] = r

    blk = pl.BlockSpec((tr, tc), lambda i, j: (i, j))
    shp = jax.ShapeDtypeStruct((R, C), F32)
    return pl.pallas_call(
        body, name=name, grid=(R // tr, C // tc),
        in_specs=[pl.BlockSpec((n_parts, tr, tc), lambda i, j: (0, i, j)), blk, blk, blk],
        out_specs=[blk, blk, blk, blk], out_shape=[shp, shp, shp, shp],
        compiler_params=_params(("parallel", "parallel")),
    )(parts, w, m, v)


def _after(value, dep):
    return lax.optimization_barrier((value, dep))[0]


def _rows128(a):
    return a.reshape(-1, 128)


def kernel(x, mix_norm_g, w_in, conv_w, spatial_w, spatial_b, conv_out_norm_g, gmlp_out_norm_g, w_out, mlp_norm_g, w_up, w_down, final_norm_g, loss_target, m_mix_norm_g, m_w_in, m_conv_w, m_spatial_w, m_spatial_b, m_conv_out_norm_g, m_gmlp_out_norm_g, m_w_out, m_mlp_norm_g, m_w_up, m_w_down, m_final_norm_g, v_mix_norm_g, v_w_in, v_conv_w, v_spatial_w, v_spatial_b, v_conv_out_norm_g, v_gmlp_out_norm_g, v_w_out, v_mlp_norm_g, v_w_up, v_w_down, v_final_norm_g):
    n_seq, seq, D = x.shape
    T = n_seq * seq
    n_in = w_in.shape[2]
    n_out = w_out.shape[1]
    n_up = w_up.shape[2]
    wc = conv_w.shape[2] * N_DEV
    n_head = wc // HEAD
    FF = n_up * N_DEV
    assert N_DEV * n_in == 5 * wc and seq % HEAD == 0 and D == 2 * wc

    c_idx = lax.axis_index("c").astype(jnp.int32).reshape(1)
    my_dev = 4 * lax.axis_index("x") + 2 * lax.axis_index("y") + lax.axis_index("c")

    xf = x.reshape(T, D)
    tgt = loss_target.reshape(T, D)

    cast = lambda w: w[0].astype(BF16)
    win_g = _all_gather_async(cast(w_in), "ag_w_in", 0, side_by_side=True)
    cw_pad = jnp.pad(conv_w[0], ((0, HALO - CONV_K), (0, 0)))
    cw_g = _all_gather_async(cw_pad, "ag_conv_w", 9)
    conv_full = jnp.transpose(cw_g[:, :CONV_K, :], (1, 0, 2)).reshape(CONV_K, wc)
    wout_g = _all_gather_async(cast(w_out), "ag_w_out", 1).reshape(D, D)
    wup_g = _all_gather_async(cast(w_up), "ag_w_up", 2, side_by_side=True)
    wdown_g = _all_gather_async(cast(w_down), "ag_w_down", 3).reshape(FF, D)

    sw = spatial_w[0]
    sbt = spatial_b[0].T
    g_mix, g_a, g_b, g_mlp = mix_norm_g, conv_out_norm_g, gmlp_out_norm_g, mlp_norm_g
    g_fin = final_norm_g.reshape(1, D)

    IN = N_DEV * n_in
    bp = _tile(T, 1024)
    bm = _tile(T, 1024)
    bn = _tile(D, 1024)
    bu = _tile(FF, 1024)
    bw = _tile(D, 512)
    bg = _tile(D, 1024)
    k_ff = _tile(FF, 4096)
    k_in = _tile(IN, 5120)

    def tile(shape, index):
        return pl.BlockSpec(shape, index)

    by_m0 = lambda n, m, k: (m, 0)
    by_0n = lambda n, m, k: (0, n)
    by_n0 = lambda n, m, k: (n, 0)
    by_mn = lambda n, m, k: (m, n)
    by_mk = lambda n, m, k: (m, k)
    by_kn = lambda n, m, k: (k, n)
    by_nk = lambda n, m, k: (n, k)
    by_0m = lambda n, m, k: (0, m)
    f32_td = [jax.ShapeDtypeStruct((T, D), F32)]

    xn = _rms_fwd(xf, g_mix, "norm_mix")
    proj = _matmul(
        "proj", (N_DEV, T // bp, 1), NN, [xn, win_g], [tile((bp, D), by_m0), tile((D, n_in), by_0n)],
        [jax.ShapeDtypeStruct((T, IN), F32)], [tile((bp, n_in), by_mn)], _ident)[0]
    y = _mixer_fwd(proj, conv_full, sw, sbt, g_a, g_b, n_seq, seq)
    h1 = _matmul(
        "out_proj", (D // bn, T // bp, 1), NN, [y, wout_g, xf],
        [tile((bp, D), by_m0), tile((D, bn), by_0n), tile((bp, bn), by_mn)],
        f32_td, [tile((bp, bn), by_mn)], lambda acc, res: (res + acc,))[0]
    xn2 = _rms_fwd(h1, g_mlp, "norm_mlp")

    def up_epilogue(acc):
        r = jnp.maximum(acc, 0.0)
        return acc, r * r

    up, act = _matmul(
        "up_proj", (FF // bu, T // bm, 1), NN, [xn2, wup_g], [tile((bm, D), by_m0), tile((D, bu), by_0n)],
        [jax.ShapeDtypeStruct((T, FF), BF16)] * 2, [tile((bm, bu), by_mn)] * 2, up_epilogue)
    mlp = _matmul(
        "down_proj", (D // bn, T // bm, FF // k_ff), NN, [act, wdown_g],
        [tile((bm, k_ff), by_mk), tile((k_ff, bn), by_kn)], f32_td, [tile((bm, bn), by_mn)], _ident)[0]

    dh2, dh2b, gg_fin, loss_row = _loss_head(h1, mlp, tgt, g_fin)
    loss = lax.psum(loss_row[0, 0], ("x", "y", "c"))
    dh2b = _after(dh2b, loss)

    gp_down = _matmul(
        "gw_down", (D // bn, FF // bg, 1), TN, [act, dh2b], [tile((T, bg), by_0m), tile((T, bn), by_0n)],
        [jax.ShapeDtypeStruct((FF, D), BF16)], [tile((bg, bn), by_mn)], _ident)[0].reshape(N_DEV, n_up, D)
    dup, got_down = _matmul(
        "d_act", (FF // bu, T // bm, 1), NT, [dh2b, wdown_g, up],
        [tile((bm, D), by_m0), tile((bu, D), by_n0), tile((bm, bu), by_mn)],
        [jax.ShapeDtypeStruct((T, FF), BF16)], [tile((bm, bu), by_mn)],
        lambda acc, u: (acc * (2.0 * jnp.maximum(u.astype(F32), 0.0)),), swap=(gp_down, None))
    dxn2, sums_down = _matmul(
        "d_xn2", (D // bn, T // bm, FF // k_ff), NT, [dup, wup_g],
        [tile((bm, k_ff), by_mk), tile((bn, k_ff), by_nk)], f32_td, [tile((bm, bn), by_mn)], _ident,
        side=_add_job(gp_down, got_down, c_idx, (D // bn) * (T // bm) * (FF // k_ff)))
    sums_down = sums_down.reshape(N_CHIP, n_up, D)
    four_down = _scatter_to_chips(sums_down, "rs_chips_w_down", 4)
    gp_up = _matmul(
        "gw_up", (FF // bu, D // bg, 1), TN, [xn2, _after(dup, sums_down)],
        [tile((T, bg), by_0m), tile((T, bu), by_0n)],
        [jax.ShapeDtypeStruct((D, FF), BF16)], [tile((bg, bu), by_mn)], _ident)[0]
    dh1, dh1b, gg_mlp = _rms_bwd(_after(dxn2, gp_up), h1, g_mlp, dh2, "norm_mlp_bwd", True)

    dy, got_up = _matmul(
        "d_y", (D // bn, T // bm, 1), NT, [dh1b, wout_g], [tile((bm, D), by_m0), tile((bn, D), by_n0)],
        f32_td, [tile((bm, bn), by_mn)], _ident, swap=(gp_up, n_up))
    gp_out, sums_up = _matmul(
        "gw_out", (D // bn, D // bw, 1), TN, [y, _after(dh1b, dy)], [tile((T, bw), by_0m), tile((T, bn), by_0n)],
        [jax.ShapeDtypeStruct((D, D), BF16)], [tile((bw, bn), by_mn)], _ident,
        side=_add_job(gp_up, got_up, c_idx, (D // bn) * (D // bw)))
    gp_out = gp_out.reshape(N_DEV, n_out, D)
    sums_up = sums_up.reshape(N_CHIP, D, n_up)
    four_up = _scatter_to_chips(sums_up, "rs_chips_w_up", 5)
    dproj, gl_conv, gl_a, gl_b, gl_sw, gl_sbt = _mixer_bwd(
        proj, _after(dy, sums_up), conv_full, sw, sbt, g_a, g_b, n_seq, seq)
    gp_in, got_out = _matmul(
        "gw_in", (N_DEV, D // bg, 1), TN, [xn, dproj], [tile((T, bg), by_0m), tile((T, n_in), by_0n)],
        [jax.ShapeDtypeStruct((D, IN), BF16)], [tile((bg, n_in), by_mn)], _ident, swap=(gp_out, None))
    sums_out = _add_sibling(gp_out, got_out, c_idx, "rs_add_w_out")
    four_out = _scatter_to_chips(sums_out, "rs_chips_w_out", 6)
    bh = _tile(T // 2, bm)
    hm = (T // 2) // bh
    dproj = _after(dproj, sums_out)
    dxn, got_in = _matmul(
        "d_xn_top", (D // bn, hm, IN // k_in), NT, [dproj, win_g],
        [tile((bh, k_in), by_mk), tile((bn, k_in), by_nk)], f32_td, [tile((bh, bn), by_mn)], _ident,
        swap=(gp_in, n_in))
    sums_in = _add_sibling(gp_in, got_in, c_idx, "rs_add_w_in")
    four_in = _scatter_to_chips(sums_in, "rs_chips_w_in", 7)
    dxn = _matmul(
        "d_xn_bottom", (D // bn, hm, IN // k_in), NT, [_after(dproj, sums_in), win_g],
        [tile((bh, k_in), lambda n, m, k: (m + hm, k)), tile((bn, k_in), by_nk)], f32_td,
        [tile((bh, bn), lambda n, m, k: (m + hm, n))], _ident, fill=dxn)[0]
    grad_x, gg_mix = _rms_bwd(_after(dxn, sums_in), xf, g_mix, dh1, "norm_mix_bwd", False)

    outs = {}
    done = grad_x
    for tag, four, w, m, v in (("w_down", four_down, w_down, m_w_down, v_w_down),
                               ("w_up", four_up, w_up, m_w_up, v_w_up),
                               ("w_out", four_out, w_out, m_w_out, v_w_out),
                               ("w_in", four_in, w_in, m_w_in, v_w_in)):
        res = _sum_adamw(_after(four, done), w[0], m[0], v[0], "adamw_" + tag)
        done = res[0]
        outs[tag] = [a[None] for a in res]

    small = [("mix_norm_g", gg_mix, mix_norm_g, m_mix_norm_g, v_mix_norm_g),
             ("conv_w", gl_conv, None, None, None),
             ("spatial_w", gl_sw, spatial_w, m_spatial_w, v_spatial_w),
             ("spatial_b", gl_sbt.T, spatial_b, m_spatial_b, v_spatial_b),
             ("conv_out_norm_g", gl_a, conv_out_norm_g, m_conv_out_norm_g, v_conv_out_norm_g),
             ("gmlp_out_norm_g", gl_b, gmlp_out_norm_g, m_gmlp_out_norm_g, v_gmlp_out_norm_g),
             ("mlp_norm_g", gg_mlp, mlp_norm_g, m_mlp_norm_g, v_mlp_norm_g),
             ("final_norm_g", gg_fin, final_norm_g, m_final_norm_g, v_final_norm_g)]
    packed_g = jnp.concatenate([_rows128(g) for _, g, _, _, _ in small], axis=0)
    zeros_cw = jnp.zeros((CONV_K * wc // 128, 128), F32)
    pack = lambda idx: jnp.concatenate(
        [zeros_cw if item[2] is None else _rows128(item[idx]) for item in small], axis=0)
    all_g = _all_gather_async(packed_g, "ag_small_grads", 8)
    sg, sd, sm, sv = _sum_adamw(_after(all_g, done), pack(2), pack(3), pack(4), "adamw_small")
    row = 0
    for name, g, w, _, _ in small:
        n_rows = g.size // 128
        if w is not None:
            outs[name] = [a[row:row + n_rows].reshape(w.shape) for a in (sg, sd, sm, sv)]
        else:
            conv_grad_full = sg[row:row + n_rows].reshape(CONV_K, wc)
        row += n_rows
    cpd = wc // N_DEV
    conv_grad = lax.dynamic_slice(conv_grad_full, (0, my_dev * cpd), (CONV_K, cpd))
    pad8 = lambda a: jnp.pad(a, ((0, HALO - CONV_K), (0, 0)))
    outs["conv_w"] = [a[:CONV_K][None] for a in _sum_adamw(
        pad8(conv_grad)[None], pad8(conv_w[0]), pad8(m_conv_w[0]), pad8(v_conv_w[0]), "adamw_conv_w")]

    order = ["mix_norm_g", "w_in", "conv_w", "spatial_w", "spatial_b", "conv_out_norm_g", "gmlp_out_norm_g",
             "w_out", "mlp_norm_g", "w_up", "w_down", "final_norm_g"]
    result = [loss, grad_x.reshape(n_seq, seq, D)]
    for k in range(4):
        result += [outs[n][k] for n in order]
    return tuple(result)
```

```python
import functools
import math
from typing import Callable, NamedTuple

import jax
import jax.numpy as jnp
from jax import lax
from jax.experimental import pallas as pl
from jax.experimental.pallas import tpu as pltpu
from jax.experimental.pallas import tpu_sc as plsc

F32 = jnp.float32
BF16 = jnp.bfloat16
MESH = pl.DeviceIdType.MESH
HBM = pltpu.HBM

EPS = 1e-5
HEAD = 128
CONV_K = 3
N_DEV = 8
N_CHIP = 4
VMEM_LIMIT_BYTES = 62 * 1024 * 1024

ADAM_LR = 0.001
ADAM_B1 = 0.9
ADAM_B2 = 0.999
ADAM_EPS = 1e-08
ADAM_WD = 0.01
ADAM_STEP = 10

GELU_K0 = math.sqrt(2.0 / math.pi)
GELU_K1 = 0.044715

NN = (((1,), (0,)), ((), ()))
NT = (((1,), (1,)), ((), ()))
TN = (((0,), (0,)), ((), ()))


def _params(semantics):
    return pltpu.CompilerParams(dimension_semantics=semantics, vmem_limit_bytes=VMEM_LIMIT_BYTES)


def _tile(dim, want):
    if dim <= want:
        return dim
    for t in range(want - want % 8, 0, -8):
        if dim % t == 0:
            return t
    raise ValueError((dim, want))


def _owner_block(ref, j, cols):
    if cols is None:
        return ref.at[j]
    return ref.at[:, pl.ds(pl.multiple_of(j * cols, 128), cols)]


def _sibling_copies(p_ref, got_ref, send_sems, recv_sems, cols):
    x, y, c = _place()
    return [pltpu.make_async_remote_copy(
        src_ref=_owner_block(p_ref, 2 * k + (1 - c), cols), dst_ref=got_ref.at[k], send_sem=send_sems.at[k],
        recv_sem=recv_sems.at[k], device_id=(x, y, 1 - c), device_id_type=MESH) for k in range(N_CHIP)]


class _SideJob(NamedTuple):
    c_idx: jax.Array
    arrays: list
    in_tiles: list
    out_shapes: list
    out_tiles: list
    fn: Callable


def _matmul(name, grid, dims, operands, in_specs, out_shapes, out_specs, epilogue, swap=None, side=None, fill=None):
    n_in = len(operands)
    n_out = len(out_shapes)
    nk = grid[2]
    n_host = 0 if swap is None else 1
    n_si = 0 if side is None else len(side.arrays)
    n_so = 0 if side is None else len(side.out_shapes)
    n_fill = 0 if fill is None else 1
    assert nk == 1 or (n_out == 1 and epilogue is _ident and out_shapes[0].dtype == F32)
    assert fill is None or side is None

    def body(*refs):
        if side is not None:
            refs = refs[1:]
        a_ref, b_ref = refs[0], refs[1]
        extra = refs[2:n_in]
        first_out = n_in + n_si + n_host + n_fill
        outs = refs[first_out:first_out + n_out]
        if side is not None:
            step = (pl.program_id(0) * grid[1] + pl.program_id(1)) * grid[2] + pl.program_id(2)
            side.fn(step, refs[n_in:n_in + n_si], refs[first_out + n_out:first_out + n_out + n_so])
        if swap is not None:
            ids = [pl.program_id(d) for d in range(3)]
            copies = _sibling_copies(refs[n_in + n_si], refs[first_out + n_out + n_so], refs[-2], refs[-1], swap[1])

            @pl.when(functools.reduce(jnp.logical_and, [i == 0 for i in ids]))
            def _():
                for cp in copies:
                    cp.start()

        part = lax.dot_general(a_ref[...], b_ref[...], dims, preferred_element_type=F32)

        def finish(acc):
            res = epilogue(acc, *[e[...] for e in extra])
            for o, r in zip(outs, res):
                o[...] = r.astype(o.dtype)

        if nk == 1:
            finish(part)
        else:
            k = pl.program_id(2)

            @pl.when(k == 0)
            def _():
                outs[0][...] = part

            @pl.when(k > 0)
            def _():
                outs[0][...] += part

        if swap is not None:
            @pl.when(functools.reduce(jnp.logical_and, [i == n - 1 for i, n in zip(ids, grid)]))
            def _():
                for cp in copies:
                    cp.wait()

    scratch = []
    semantics = ("parallel", "parallel", "arbitrary")
    operands, in_specs = list(operands), list(in_specs)
    out_shapes, out_specs = list(out_shapes), list(out_specs)
    if side is not None:
        def with_c(spec):
            return pl.BlockSpec(spec.block_shape, lambda n, m, k, c_ref, f=spec.index_map: f(n, m, k))

        def side_tile(shape, index):
            return pl.BlockSpec(shape, lambda n, m, k, c_ref: index((n * grid[1] + m) * grid[2] + k, c_ref[0]))

        in_specs = [with_c(s) for s in in_specs] + [side_tile(*t) for t in side.in_tiles]
        out_specs = [with_c(s) for s in out_specs] + [side_tile(*t) for t in side.out_tiles]
        operands += side.arrays
        out_shapes += side.out_shapes
        semantics = ("arbitrary", "arbitrary", "arbitrary")
    if swap is not None:
        parts, cols = swap
        got_shape = parts.shape[1:] if cols is None else (parts.shape[0], cols)
        hbm = pl.BlockSpec(memory_space=HBM)
        operands, in_specs = operands + [parts], in_specs + [hbm]
        out_shapes = out_shapes + [jax.ShapeDtypeStruct((N_CHIP,) + got_shape, parts.dtype)]
        out_specs = out_specs + [hbm]
        scratch += [pltpu.SemaphoreType.DMA((N_CHIP,)), pltpu.SemaphoreType.DMA((N_CHIP,))]
        semantics = ("arbitrary", "arbitrary", "arbitrary")
    aliases = {}
    if fill is not None:
        aliases = {len(operands): 0}
        operands, in_specs = operands + [fill], in_specs + [pl.BlockSpec(memory_space=pl.ANY)]
    if side is None:
        return pl.pallas_call(
            body, name=name, grid=grid, in_specs=in_specs, out_specs=out_specs, out_shape=out_shapes,
            scratch_shapes=scratch, input_output_aliases=aliases, compiler_params=_params(semantics),
        )(*operands)
    grid_spec = pltpu.PrefetchScalarGridSpec(
        num_scalar_prefetch=1, grid=grid, in_specs=in_specs, out_specs=out_specs, scratch_shapes=scratch)
    return pl.pallas_call(
        body, name=name, grid_spec=grid_spec, out_shape=out_shapes, compiler_params=_params(semantics),
    )(side.c_idx, *operands)


def _add_job(parts, got, c_idx, steps):
    _, R, C = got.shape
    rows = N_CHIP * R
    tr = rows // steps
    assert rows % steps == 0 and R % tr == 0 and tr % 16 == 0, (rows, steps)
    per = R // tr
    if parts.ndim == 3:
        flat = parts.reshape(N_DEV * R, C)
        parts_index = lambda s, c: ((2 * (s // per) + c) * per + s % per, 0)
    else:
        flat = parts
        parts_index = lambda s, c: (s % per, 2 * (s // per) + c)
    by_step = lambda s, c: (s, 0)

    def add(step, ins, outs):
        outs[0][...] = (ins[0][...].astype(F32) + ins[1][...].astype(F32)).astype(outs[0].dtype)

    return _SideJob(
        c_idx, [flat, got.reshape(rows, C)], [((tr, C), parts_index), ((tr, C), by_step)],
        [jax.ShapeDtypeStruct((rows, C), parts.dtype)], [((tr, C), by_step)], add)


def _ident(acc):
    return (acc,)


ROW_TILE = 256


def _rms_fwd(x, g, name):
    T, D = x.shape
    tr = _tile(T, ROW_TILE)

    def body(x_ref, g_ref, o_ref):
        xv = x_ref[...]
        inv = lax.rsqrt(jnp.mean(xv * xv, axis=-1, keepdims=True) + EPS)
        o_ref[...] = (xv * inv * g_ref[...]).astype(o_ref.dtype)

    return pl.pallas_call(
        body, name=name, grid=(T // tr,),
        in_specs=[pl.BlockSpec((tr, D), lambda i: (i, 0)), pl.BlockSpec((1, D), lambda i: (0, 0))],
        out_specs=pl.BlockSpec((tr, D), lambda i: (i, 0)),
        out_shape=jax.ShapeDtypeStruct((T, D), BF16),
        compiler_params=_params(("parallel",)),
    )(x, g)


def _rms_bwd_rows(i, ins, outs):
    dy_ref, x_ref, g_ref, dres_ref = ins
    dx_ref, gg_ref = outs[0], outs[-1]
    xv = x_ref[...]
    dyv = dy_ref[...]
    inv = lax.rsqrt(jnp.mean(xv * xv, axis=-1, keepdims=True) + EPS)
    gd = dyv * g_ref[...]
    dot = jnp.mean(gd * xv, axis=-1, keepdims=True)
    dx = dres_ref[...] + (inv * gd - xv * (inv * inv * inv * dot))
    dx_ref[...] = dx
    if len(outs) == 3:
        outs[1][...] = dx.astype(BF16)
    part = jnp.sum(dyv * xv * inv, axis=0, keepdims=True)

    @pl.when(i == 0)
    def _():
        gg_ref[...] = part

    @pl.when(i > 0)
    def _():
        gg_ref[...] += part


def _rms_bwd_job(dy, x, g, dres, c_idx, steps):
    T, D = x.shape
    tr = T // steps
    assert T % steps == 0 and tr % 16 == 0, (T, steps)
    rows = ((tr, D), lambda s, c: (s, 0))
    vec = ((1, D), lambda s, c: (0, 0))
    return _SideJob(
        c_idx, [dy, x, g, dres], [rows, rows, vec, rows],
        [jax.ShapeDtypeStruct((T, D), F32), jax.ShapeDtypeStruct((T, D), BF16), jax.ShapeDtypeStruct((1, D), F32)],
        [rows, rows, vec], _rms_bwd_rows)


def _rms_bwd(dy, x, g, dres, name, want_bf16):
    T, D = x.shape
    tr = _tile(T, ROW_TILE)

    def body(dy_ref, x_ref, g_ref, dres_ref, *outs):
        _rms_bwd_rows(pl.program_id(0), (dy_ref, x_ref, g_ref, dres_ref), outs)

    row = pl.BlockSpec((tr, D), lambda i: (i, 0))
    vec = pl.BlockSpec((1, D), lambda i: (0, 0))
    out_shape = [jax.ShapeDtypeStruct((T, D), F32)]
    out_specs = [row]
    if want_bf16:
        out_shape.append(jax.ShapeDtypeStruct((T, D), BF16))
        out_specs.append(row)
    out_shape.append(jax.ShapeDtypeStruct((1, D), F32))
    out_specs.append(vec)
    return pl.pallas_call(
        body, name=name, grid=(T // tr,), in_specs=[row, row, vec, row],
        out_specs=out_specs, out_shape=out_shape, compiler_params=_params(("arbitrary",)),
    )(dy, x, g, dres)


def _loss_head(h1, mlp, target, g):
    T, D = h1.shape
    tr = _tile(T, ROW_TILE)

    def body(h1_ref, mlp_ref, t_ref, g_ref, dh_ref, dhb_ref, gg_ref, loss_ref):
        i = pl.program_id(0)
        hv = h1_ref[...] + mlp_ref[...]
        gv = g_ref[...]
        inv = lax.rsqrt(jnp.mean(hv * hv, axis=-1, keepdims=True) + EPS)
        diff = hv * inv * gv - t_ref[...]
        lpart = 0.5 * jnp.sum(jnp.mean(diff * diff, axis=-1, keepdims=True), axis=0, keepdims=True)
        dout = diff * (1.0 / D)
        gd = dout * gv
        dot = jnp.mean(gd * hv, axis=-1, keepdims=True)
        dh = inv * gd - hv * (inv * inv * inv * dot)
        dh_ref[...] = dh
        dhb_ref[...] = dh.astype(BF16)
        part = jnp.sum(dout * hv * inv, axis=0, keepdims=True)
        lrow = jnp.broadcast_to(lpart, (1, 128))

        @pl.when(i == 0)
        def _():
            gg_ref[...] = part
            loss_ref[...] = lrow

        @pl.when(i > 0)
        def _():
            gg_ref[...] += part
            loss_ref[...] += lrow

    row = pl.BlockSpec((tr, D), lambda i: (i, 0))
    vec = pl.BlockSpec((1, D), lambda i: (0, 0))
    return pl.pallas_call(
        body, name="loss_head", grid=(T // tr,), in_specs=[row, row, row, vec],
        out_specs=[row, row, vec, pl.BlockSpec((1, 128), lambda i: (0, 0))],
        out_shape=[jax.ShapeDtypeStruct((T, D), F32), jax.ShapeDtypeStruct((T, D), BF16),
                   jax.ShapeDtypeStruct((1, D), F32), jax.ShapeDtypeStruct((1, 128), F32)],
        compiler_params=_params(("arbitrary",)),
    )(h1, mlp, target, g)


HALO = 8


def _gelu_parts(x):
    th = jnp.tanh(GELU_K0 * (x + GELU_K1 * (x * x * x)))
    return x * (0.5 * (1.0 + th)), th


def _gelu_grad(x, th):
    return 0.5 * (1.0 + th) + (0.5 * GELU_K0) * x * (1.0 - th * th) * (1.0 + (3.0 * GELU_K1) * (x * x))


def _conv_fwd(b_ref, c_ref, h_ref, ch_ref, hh_ref, w_ref, first):
    tt, wc = c_ref.shape
    c = c_ref[...]
    h = h_ref[...]
    hc = c * h
    prev1 = jnp.where(first, 0.0, ch_ref[HALO - 1:HALO, :] * hh_ref[HALO - 1:HALO, :])
    prev2 = jnp.where(first, 0.0, ch_ref[HALO - 2:HALO - 1, :] * hh_ref[HALO - 2:HALO - 1, :])
    row = lax.broadcasted_iota(jnp.int32, (tt, wc), 0)
    m1 = jnp.where(row == 0, prev1, pltpu.roll(hc, 1, 0))
    m2 = jnp.where(row == 0, prev2, jnp.where(row == 1, prev1, pltpu.roll(hc, 2, 0)))
    conv = w_ref[0:1, :] * m2 + w_ref[1:2, :] * m1 + w_ref[2:3, :] * hc
    return c, h, hc, m1, m2, conv, b_ref[...] * conv


def _tril():
    r = lax.broadcasted_iota(jnp.int32, (HEAD, HEAD), 0)
    s = lax.broadcasted_iota(jnp.int32, (HEAD, HEAD), 1)
    return r >= s


def _spatial_fwd(gvb, sw_ref, sbt_ref, s_scr):
    n_head = sw_ref.shape[0]
    tri = _tril()
    for hd in range(n_head):
        sl = slice(hd * HEAD, (hd + 1) * HEAD)
        wm = jnp.where(tri, sw_ref[hd], 0.0).astype(BF16)
        s_scr[:, sl] = jnp.dot(wm, gvb[:, sl], preferred_element_type=F32) + sbt_ref[:, hd:hd + 1]


def _mixer_specs(n_tiles, wc, row_of, n_grid):
    def grp(g):
        return pl.BlockSpec((HEAD, wc), lambda *ids: (row_of(*ids), g))

    def halo(g):
        return pl.BlockSpec((HALO, wc), lambda *ids: (jnp.maximum(row_of(*ids) * (HEAD // HALO) - 1, 0), g))

    return grp, halo


def _mixer_fwd(proj, conv_w, sw, sbt, g_a, g_b, n_seq, seq):
    T, w5 = proj.shape
    wc = w5 // 5
    n_head = wc // HEAD
    nt = seq // HEAD

    def row_of(s, i):
        return s * nt + i

    grp, halo = _mixer_specs(nt, wc, row_of, 2)

    def body(b_ref, c_ref, h_ref, u_ref, v_ref, ch_ref, hh_ref, w_ref, sw_ref, sbt_ref, ga_ref, gb_ref,
             y_ref, s_scr):
        i = pl.program_id(1)
        ya = _conv_fwd(b_ref, c_ref, h_ref, ch_ref, hh_ref, w_ref, i == 0)[-1]
        inv = lax.rsqrt(jnp.mean(ya * ya, axis=-1, keepdims=True) + EPS)
        y_ref[:, :wc] = (ya * inv * ga_ref[...]).astype(BF16)
        gu, _ = _gelu_parts(u_ref[...])
        gv, _ = _gelu_parts(v_ref[...])
        _spatial_fwd(gv.astype(BF16), sw_ref, sbt_ref, s_scr)
        yb = gu * s_scr[...]
        inv = lax.rsqrt(jnp.mean(yb * yb, axis=-1, keepdims=True) + EPS)
        y_ref[:, wc:] = (yb * inv * gb_ref[...]).astype(BF16)

    const2 = lambda shape: pl.BlockSpec(shape, lambda s, i: (0, 0))
    return pl.pallas_call(
        body, name="mixer_fwd", grid=(n_seq, nt),
        in_specs=[grp(0), grp(1), grp(2), grp(3), grp(4), halo(1), halo(2),
                  const2((CONV_K, wc)), pl.BlockSpec((n_head, HEAD, HEAD), lambda s, i: (0, 0, 0)),
                  const2((HEAD, n_head)), const2((1, wc)), const2((1, wc))],
        out_specs=pl.BlockSpec((HEAD, 2 * wc), lambda s, i: (s * nt + i, 0)),
        out_shape=jax.ShapeDtypeStruct((T, 2 * wc), BF16),
        scratch_shapes=[pltpu.VMEM((HEAD, wc), F32)],
        compiler_params=_params(("parallel", "parallel")),
    )(proj, proj, proj, proj, proj, proj, proj, conv_w, sw, sbt, g_a, g_b)


def _mixer_bwd(proj, dy, conv_w, sw, sbt, g_a, g_b, n_seq, seq):
    T, w5 = proj.shape
    wc = w5 // 5
    n_head = wc // HEAD
    nt = seq // HEAD
    tt = HEAD

    def row_of(s, ir):
        return s * nt + (nt - 1 - ir)

    grp, halo = _mixer_specs(nt, wc, row_of, 2)

    def body(b_ref, c_ref, h_ref, u_ref, v_ref, ch_ref, hh_ref, dya_ref, dyb_ref, w_ref, sw_ref, sbt_ref,
             ga_ref, gb_ref, dp_ref, gw_ref, gga_ref, ggb_ref, gsw_ref, gsb_ref,
             carry_scr, s_scr, t_scr, dsum_scr):
        s_id = pl.program_id(0)
        ir = pl.program_id(1)
        first_tile = jnp.logical_and(s_id == 0, ir == 0)
        last_tile = jnp.logical_and(s_id == n_seq - 1, ir == nt - 1)

        def conv_part():
            @pl.when(ir == 0)
            def _():
                carry_scr[...] = jnp.zeros_like(carry_scr)

            c, h, hc, m1, m2, conv, ya = _conv_fwd(b_ref, c_ref, h_ref, ch_ref, hh_ref, w_ref, ir == nt - 1)
            inv = lax.rsqrt(jnp.mean(ya * ya, axis=-1, keepdims=True) + EPS)
            dyn = dya_ref[...]
            gd = dyn * ga_ref[...]
            dot = jnp.mean(gd * ya, axis=-1, keepdims=True)
            dya = inv * gd - ya * (inv * inv * inv * dot)
            gg = jnp.sum(dyn * ya * inv, axis=0, keepdims=True)
            dconv = dya * b_ref[...]
            nxt0 = carry_scr[0:1, :]
            nxt1 = carry_scr[1:2, :]
            row = lax.broadcasted_iota(jnp.int32, (tt, wc), 0)
            p1 = jnp.where(row == tt - 1, nxt0, pltpu.roll(dconv, tt - 1, 0))
            p2 = jnp.where(row == tt - 2, nxt0, jnp.where(row == tt - 1, nxt1, pltpu.roll(dconv, tt - 2, 0)))
            dhc = w_ref[2:3, :] * dconv + w_ref[1:2, :] * p1 + w_ref[0:1, :] * p2
            carry_scr[...] = dconv[0:HALO, :]
            dp_ref[:, 0:wc] = (dya * conv).astype(BF16)
            dp_ref[:, wc:2 * wc] = (dhc * h).astype(BF16)
            dp_ref[:, 2 * wc:3 * wc] = (dhc * c).astype(BF16)
            gw0 = jnp.sum(dconv * m2, axis=0, keepdims=True)
            gw1 = jnp.sum(dconv * m1, axis=0, keepdims=True)
            gw2 = jnp.sum(dconv * hc, axis=0, keepdims=True)

            @pl.when(first_tile)
            def _():
                gw_ref[0:1, :] = gw0
                gw_ref[1:2, :] = gw1
                gw_ref[2:3, :] = gw2
                gga_ref[...] = gg

            @pl.when(jnp.logical_not(first_tile))
            def _():
                gw_ref[0:1, :] += gw0
                gw_ref[1:2, :] += gw1
                gw_ref[2:3, :] += gw2
                gga_ref[...] += gg

        def gate_part():
            u = u_ref[...]
            v = v_ref[...]
            gu, thu = _gelu_parts(u)
            gv, thv = _gelu_parts(v)
            gvb = gv.astype(BF16)
            _spatial_fwd(gvb, sw_ref, sbt_ref, s_scr)
            sv = s_scr[...]
            yb = gu * sv
            inv = lax.rsqrt(jnp.mean(yb * yb, axis=-1, keepdims=True) + EPS)
            dyn = dyb_ref[...]
            gd = dyn * gb_ref[...]
            dot = jnp.mean(gd * yb, axis=-1, keepdims=True)
            dyb = inv * gd - yb * (inv * inv * inv * dot)
            gg = jnp.sum(dyn * yb * inv, axis=0, keepdims=True)
            ds = dyb * gu
            dsb = ds.astype(BF16)
            tri = _tril()

            @pl.when(first_tile)
            def _():
                ggb_ref[...] = gg
                dsum_scr[...] = ds
                gsw_ref[...] = jnp.zeros_like(gsw_ref)

            @pl.when(jnp.logical_not(first_tile))
            def _():
                ggb_ref[...] += gg
                dsum_scr[...] += ds

            for hd in range(n_head):
                sl = slice(hd * HEAD, (hd + 1) * HEAD)
                wm = jnp.where(tri, sw_ref[hd], 0.0).astype(BF16)
                t_scr[:, sl] = lax.dot_general(wm, dsb[:, sl], TN, preferred_element_type=F32)
                gsw_ref[hd] += lax.dot_general(dsb[:, sl], gvb[:, sl], NT, preferred_element_type=F32)
            dp_ref[:, 3 * wc:4 * wc] = (dyb * sv * _gelu_grad(u, thu)).astype(BF16)
            dp_ref[:, 4 * wc:5 * wc] = (t_scr[...] * _gelu_grad(v, thv)).astype(BF16)

            @pl.when(last_tile)
            def _():
                for hd in range(n_head):
                    sl = slice(hd * HEAD, (hd + 1) * HEAD)
                    gsw_ref[hd] = jnp.where(tri, gsw_ref[hd], 0.0)
                    gsb_ref[:, hd:hd + 1] = jnp.sum(dsum_scr[:, sl], axis=1, keepdims=True)

        conv_part()
        gate_part()

    const2 = lambda shape: pl.BlockSpec(shape, lambda s, i: (0, 0))
    const3 = pl.BlockSpec((n_head, HEAD, HEAD), lambda s, i: (0, 0, 0))
    dy_spec = lambda col: pl.BlockSpec((tt, wc), lambda s, ir: (row_of(s, ir), col))
    return pl.pallas_call(
        body, name="mixer_bwd", grid=(n_seq, nt),
        in_specs=[grp(0), grp(1), grp(2), grp(3), grp(4), halo(1), halo(2), dy_spec(0), dy_spec(1),
                  const2((CONV_K, wc)), const3, const2((HEAD, n_head)), const2((1, wc)), const2((1, wc))],
        out_specs=[pl.BlockSpec((tt, 5 * wc), lambda s, ir: (row_of(s, ir), 0)),
                   const2((CONV_K, wc)), const2((1, wc)), const2((1, wc)), const3, const2((HEAD, n_head))],
        out_shape=[jax.ShapeDtypeStruct((T, 5 * wc), BF16), jax.ShapeDtypeStruct((CONV_K, wc), F32),
                   jax.ShapeDtypeStruct((1, wc), F32), jax.ShapeDtypeStruct((1, wc), F32),
                   jax.ShapeDtypeStruct((n_head, HEAD, HEAD), F32), jax.ShapeDtypeStruct((HEAD, n_head), F32)],
        scratch_shapes=[pltpu.VMEM((HALO, wc), F32), pltpu.VMEM((tt, wc), F32), pltpu.VMEM((tt, wc), F32),
                        pltpu.VMEM((tt, wc), F32)],
        compiler_params=_params(("arbitrary", "arbitrary")),
    )(proj, proj, proj, proj, proj, proj, proj, dy, dy, conv_w, sw, sbt, g_a, g_b)


def _place():
    return lax.axis_index("x"), lax.axis_index("y"), lax.axis_index("c")


def _other_chips(x, y):
    return [(1 - x, y), (x, 1 - y), (1 - x, 1 - y)]


def _all_gather_async(blk, name, collective_id, side_by_side=False):
    cols = blk.shape[1] if side_by_side else None
    out_shape = (blk.shape[0], N_DEV * cols) if side_by_side else (N_DEV,) + blk.shape

    def body(x_ref, out_ref, send_sems, recv_sems, local_sem):
        x, y, c = _place()
        me, sibling = (x, y, c), (x, y, 1 - c)
        x_nbr, y_nbr, diagonal = (1 - x, y, c), (x, 1 - y, c), (1 - x, 1 - y, c)
        near = (c * x + (1 - c) * (1 - x), c * (1 - y) + (1 - c) * y, c)
        far = ((1 - c) * x + c * (1 - x), (1 - c) * (1 - y) + c * y, c)
        _handshake([sibling, x_nbr, y_nbr])

        def slot(px, py, pc):
            return _owner_block(out_ref, 4 * px + 2 * py + pc, cols)

        def copy(k, block, to, src=None):
            return pltpu.make_async_remote_copy(
                src_ref=slot(*block) if src is None else src, dst_ref=slot(*block),
                send_sem=send_sems.at[k], recv_sem=recv_sems.at[k], device_id=to, device_id_type=MESH)

        mine = pltpu.make_async_copy(x_ref, slot(*me), local_sem)
        mine.start()
        sent = [copy(0, me, sibling, src=x_ref), copy(1, me, x_nbr, src=x_ref), copy(2, me, y_nbr, src=x_ref)]
        for cp in sent:
            cp.start()
        copy(1 + c, near, me).wait_recv()
        sent += [copy(3, near, far), copy(4, near, sibling)]
        sent[-2].start()
        sent[-1].start()
        copy(2 - c, far, me).wait_recv()
        sent.append(copy(5, far, sibling))
        sent[-1].start()
        copy(3, diagonal, me).wait_recv()
        sent.append(copy(6, diagonal, sibling))
        sent[-1].start()
        for k in (0, 4, 5, 6):
            copy(k, sibling, me).wait_recv()
        for cp in sent:
            cp.wait_send()
        mine.wait()

    return _sequencer_call(
        body, name, collective_id, jax.ShapeDtypeStruct(out_shape, blk.dtype),
        [pltpu.SemaphoreType.DMA((7,)), pltpu.SemaphoreType.DMA((7,)), pltpu.SemaphoreType.DMA], blk)


def _sequencer_call(body, name, collective_id, out_type, scratch_types, operand):
    return pl.kernel(
        body, name=name, out_type=out_type, mesh=plsc.ScalarSubcoreMesh(axis_name="seq_core", num_cores=1),
        scratch_types=scratch_types, compiler_params=pltpu.CompilerParams(collective_id=collective_id),
    )(operand)


def _handshake(peers):
    barrier = pltpu.get_barrier_semaphore()
    for peer in peers:
        pl.semaphore_signal(barrier, inc=1, device_id=peer, device_id_type=MESH)
    pl.semaphore_wait(barrier, len(peers))


def _add_sibling(parts, got, c_idx, name):
    _, R, C = got.shape
    tr, tc = _tile(R, 1024), _tile(C, 2048)

    def body(c_ref, p_ref, g_ref, o_ref):
        o_ref[...] = (p_ref[...].astype(F32) + g_ref[...].astype(F32)).astype(o_ref.dtype)

    if parts.ndim == 3:
        parts_spec = pl.BlockSpec((None, tr, tc), lambda k, i, j, c_ref: (2 * k + c_ref[0], i, j))
    else:
        parts_spec = pl.BlockSpec((tr, tc), lambda k, i, j, c_ref: (i, (2 * k + c_ref[0]) * (C // tc) + j))
    grid_spec = pltpu.PrefetchScalarGridSpec(
        num_scalar_prefetch=1, grid=(N_CHIP, R // tr, C // tc),
        in_specs=[parts_spec, pl.BlockSpec((None, tr, tc), lambda k, i, j, c_ref: (k, i, j))],
        out_specs=pl.BlockSpec((None, tr, tc), lambda k, i, j, c_ref: (k, i, j)))
    return pl.pallas_call(
        body, name=name, grid_spec=grid_spec, out_shape=jax.ShapeDtypeStruct((N_CHIP, R, C), parts.dtype),
        compiler_params=_params(("parallel", "parallel", "parallel")),
    )(c_idx, parts, got)


def _scatter_to_chips(sums, name, collective_id):
    _, R, C = sums.shape

    def body(q_ref, got_ref, send_sems, recv_sems, local_sem):
        x, y, c = _place()
        _handshake([(*chip, c) for chip in _other_chips(x, y)])
        my_chip = 2 * x + y
        mine = pltpu.make_async_copy(q_ref.at[my_chip], got_ref.at[my_chip], local_sem)
        mine.start()
        copies = [pltpu.make_async_remote_copy(
            src_ref=q_ref.at[2 * px + py], dst_ref=got_ref.at[my_chip], send_sem=send_sems.at[j],
            recv_sem=recv_sems.at[j], device_id=(px, py, c), device_id_type=MESH)
            for j, (px, py) in enumerate(_other_chips(x, y))]
        for cp in copies:
            cp.start()
        for cp in copies:
            cp.wait()
        mine.wait()

    return _sequencer_call(
        body, name, collective_id, jax.ShapeDtypeStruct((N_CHIP, R, C), sums.dtype),
        [pltpu.SemaphoreType.DMA((3,)), pltpu.SemaphoreType.DMA((3,)), pltpu.SemaphoreType.DMA], sums)


def _adamw_math(w, g, m, v):
    m = ADAM_B1 * m + (1.0 - ADAM_B1) * g
    v = ADAM_B2 * v + (1.0 - ADAM_B2) * (g * g)
    m_hat = m / (1.0 - ADAM_B1 ** ADAM_STEP)
    v_hat = v / (1.0 - ADAM_B2 ** ADAM_STEP)
    delta = -ADAM_LR * (m_hat / (jnp.sqrt(v_hat) + ADAM_EPS) + ADAM_WD * w)
    return delta, m, v


def _sum_adamw_tiles(p, w, m, v):
    g = p[0].astype(F32)
    for k in range(1, p.shape[0]):
        g = g + p[k].astype(F32)
    delta, mn, vn = _adamw_math(w, g, m, v)
    return g, delta, mn, vn


def _sum_adamw(parts, w, m, v, name):
    n_parts, R, C = parts.shape
    tr, tc = _tile(R, 512), _tile(C, 1024)

    def body(p_ref, w_ref, m_ref, v_ref, g_out, d_out, m_out, v_out):
        res = _sum_adamw_tiles(p_ref[...], w_ref[...], m_ref[...], v_ref[...])
        for o, r in zip((g_out, d_out, m_out, v_out), res):
            o[...] = r

    blk = pl.BlockSpec((tr, tc), lambda i, j: (i, j))
    shp = jax.ShapeDtypeStruct((R, C), F32)
    return pl.pallas_call(
        body, name=name, grid=(R // tr, C // tc),
        in_specs=[pl.BlockSpec((n_parts, tr, tc), lambda i, j: (0, i, j)), blk, blk, blk],
        out_specs=[blk, blk, blk, blk], out_shape=[shp, shp, shp, shp],
        compiler_params=_params(("parallel", "parallel")),
    )(parts, w, m, v)


def _after(value, dep):
    return lax.optimization_barrier((value, dep))[0]


def _rows128(a):
    return a.reshape(-1, 128)


def kernel(x, mix_norm_g, w_in, conv_w, spatial_w, spatial_b, conv_out_norm_g, gmlp_out_norm_g, w_out, mlp_norm_g, w_up, w_down, final_norm_g, loss_target, m_mix_norm_g, m_w_in, m_conv_w, m_spatial_w, m_spatial_b, m_conv_out_norm_g, m_gmlp_out_norm_g, m_w_out, m_mlp_norm_g, m_w_up, m_w_down, m_final_norm_g, v_mix_norm_g, v_w_in, v_conv_w, v_spatial_w, v_spatial_b, v_conv_out_norm_g, v_gmlp_out_norm_g, v_w_out, v_mlp_norm_g, v_w_up, v_w_down, v_final_norm_g):
    n_seq, seq, D = x.shape
    T = n_seq * seq
    n_in = w_in.shape[2]
    n_out = w_out.shape[1]
    n_up = w_up.shape[2]
    wc = conv_w.shape[2] * N_DEV
    n_head = wc // HEAD
    FF = n_up * N_DEV
    assert N_DEV * n_in == 5 * wc and seq % HEAD == 0 and D == 2 * wc

    c_idx = lax.axis_index("c").astype(jnp.int32).reshape(1)
    my_dev = 4 * lax.axis_index("x") + 2 * lax.axis_index("y") + lax.axis_index("c")

    xf = x.reshape(T, D)
    tgt = loss_target.reshape(T, D)

    cast = lambda w: w[0].astype(BF16)
    win_g = _all_gather_async(cast(w_in), "ag_w_in", 0, side_by_side=True)
    cw_pad = jnp.pad(conv_w[0], ((0, HALO - CONV_K), (0, 0)))
    cw_g = _all_gather_async(cw_pad, "ag_conv_w", 9)
    conv_full = jnp.transpose(cw_g[:, :CONV_K, :], (1, 0, 2)).reshape(CONV_K, wc)
    wout_g = _all_gather_async(cast(w_out), "ag_w_out", 1).reshape(D, D)
    wup_g = _all_gather_async(cast(w_up), "ag_w_up", 2, side_by_side=True)
    wdown_g = _all_gather_async(cast(w_down), "ag_w_down", 3).reshape(FF, D)

    sw = spatial_w[0]
    sbt = spatial_b[0].T
    g_mix, g_a, g_b, g_mlp = mix_norm_g, conv_out_norm_g, gmlp_out_norm_g, mlp_norm_g
    g_fin = final_norm_g.reshape(1, D)

    IN = N_DEV * n_in
    bp = _tile(T, 1024)
    bm = _tile(T, 1024)
    bn = _tile(D, 1024)
    bu = _tile(FF, 1024)
    bw = _tile(D, 512)
    bg = _tile(D, 1024)
    k_ff = _tile(FF, 4096)
    k_in = _tile(IN, 5120)

    def tile(shape, index):
        return pl.BlockSpec(shape, index)

    by_m0 = lambda n, m, k: (m, 0)
    by_0n = lambda n, m, k: (0, n)
    by_n0 = lambda n, m, k: (n, 0)
    by_mn = lambda n, m, k: (m, n)
    by_mk = lambda n, m, k: (m, k)
    by_kn = lambda n, m, k: (k, n)
    by_nk = lambda n, m, k: (n, k)
    by_0m = lambda n, m, k: (0, m)
    f32_td = [jax.ShapeDtypeStruct((T, D), F32)]

    xn = _rms_fwd(xf, g_mix, "norm_mix")
    proj = _matmul(
        "proj", (N_DEV, T // bp, 1), NN, [xn, win_g], [tile((bp, D), by_m0), tile((D, n_in), by_0n)],
        [jax.ShapeDtypeStruct((T, IN), F32)], [tile((bp, n_in), by_mn)], _ident)[0]
    y = _mixer_fwd(proj, conv_full, sw, sbt, g_a, g_b, n_seq, seq)
    h1 = _matmul(
        "out_proj", (D // bn, T // bp, 1), NN, [y, wout_g, xf],
        [tile((bp, D), by_m0), tile((D, bn), by_0n), tile((bp, bn), by_mn)],
        f32_td, [tile((bp, bn), by_mn)], lambda acc, res: (res + acc,))[0]
    xn2 = _rms_fwd(h1, g_mlp, "norm_mlp")

    def up_epilogue(acc):
        r = jnp.maximum(acc, 0.0)
        return acc, r * r

    up, act = _matmul(
        "up_proj", (FF // bu, T // bm, 1), NN, [xn2, wup_g], [tile((bm, D), by_m0), tile((D, bu), by_0n)],
        [jax.ShapeDtypeStruct((T, FF), BF16)] * 2, [tile((bm, bu), by_mn)] * 2, up_epilogue)
    mlp = _matmul(
        "down_proj", (D // bn, T // bm, FF // k_ff), NN, [act, wdown_g],
        [tile((bm, k_ff), by_mk), tile((k_ff, bn), by_kn)], f32_td, [tile((bm, bn), by_mn)], _ident)[0]

    dh2, dh2b, gg_fin, loss_row = _loss_head(h1, mlp, tgt, g_fin)
    loss = lax.psum(loss_row[0, 0], ("x", "y", "c"))
    dh2b = _after(dh2b, loss)

    gp_down = _matmul(
        "gw_down", (D // bn, FF // bg, 1), TN, [act, dh2b], [tile((T, bg), by_0m), tile((T, bn), by_0n)],
        [jax.ShapeDtypeStruct((FF, D), BF16)], [tile((bg, bn), by_mn)], _ident)[0].reshape(N_DEV, n_up, D)
    dup, got_down = _matmul(
        "d_act", (FF // bu, T // bm, 1), NT, [dh2b, wdown_g, up],
        [tile((bm, D), by_m0), tile((bu, D), by_n0), tile((bm, bu), by_mn)],
        [jax.ShapeDtypeStruct((T, FF), BF16)], [tile((bm, bu), by_mn)],
        lambda acc, u: (acc * (2.0 * jnp.maximum(u.astype(F32), 0.0)),), swap=(gp_down, None))
    dxn2, sums_down = _matmul(
        "d_xn2", (D // bn, T // bm, FF // k_ff), NT, [dup, wup_g],
        [tile((bm, k_ff), by_mk), tile((bn, k_ff), by_nk)], f32_td, [tile((bm, bn), by_mn)], _ident,
        side=_add_job(gp_down, got_down, c_idx, (D // bn) * (T // bm) * (FF // k_ff)))
    sums_down = sums_down.reshape(N_CHIP, n_up, D)
    four_down = _scatter_to_chips(sums_down, "rs_chips_w_down", 4)
    gp_up, dh1, dh1b, gg_mlp = _matmul(
        "gw_up", (FF // bu, D // bg, 1), TN, [xn2, _after(dup, sums_down)],
        [tile((T, bg), by_0m), tile((T, bu), by_0n)],
        [jax.ShapeDtypeStruct((D, FF), BF16)], [tile((bg, bu), by_mn)], _ident,
        side=_rms_bwd_job(dxn2, h1, g_mlp, dh2, c_idx, (FF // bu) * (D // bg)))

    dy, got_up = _matmul(
        "d_y", (D // bn, T // bm, 1), NT, [dh1b, wout_g], [tile((bm, D), by_m0), tile((bn, D), by_n0)],
        f32_td, [tile((bm, bn), by_mn)], _ident, swap=(gp_up, n_up))
    gp_out, sums_up = _matmul(
        "gw_out", (D // bn, D // bw, 1), TN, [y, _after(dh1b, dy)], [tile((T, bw), by_0m), tile((T, bn), by_0n)],
        [jax.ShapeDtypeStruct((D, D), BF16)], [tile((bw, bn), by_mn)], _ident,
        side=_add_job(gp_up, got_up, c_idx, (D // bn) * (D // bw)))
    gp_out = gp_out.reshape(N_DEV, n_out, D)
    sums_up = sums_up.reshape(N_CHIP, D, n_up)
    four_up = _scatter_to_chips(sums_up, "rs_chips_w_up", 5)
    dproj, gl_conv, gl_a, gl_b, gl_sw, gl_sbt = _mixer_bwd(
        proj, _after(dy, sums_up), conv_full, sw, sbt, g_a, g_b, n_seq, seq)
    gp_in, got_out = _matmul(
        "gw_in", (N_DEV, D // bg, 1), TN, [xn, dproj], [tile((T, bg), by_0m), tile((T, n_in), by_0n)],
        [jax.ShapeDtypeStruct((D, IN), BF16)], [tile((bg, n_in), by_mn)], _ident, swap=(gp_out, None))
    sums_out = _add_sibling(gp_out, got_out, c_idx, "rs_add_w_out")
    four_out = _scatter_to_chips(sums_out, "rs_chips_w_out", 6)
    bh = _tile(T // 2, bm)
    hm = (T // 2) // bh
    dproj = _after(dproj, sums_out)
    dxn, got_in = _matmul(
        "d_xn_top", (D // bn, hm, IN // k_in), NT, [dproj, win_g],
        [tile((bh, k_in), by_mk), tile((bn, k_in), by_nk)], f32_td, [tile((bh, bn), by_mn)], _ident,
        swap=(gp_in, n_in))
    sums_in = _add_sibling(gp_in, got_in, c_idx, "rs_add_w_in")
    four_in = _scatter_to_chips(sums_in, "rs_chips_w_in", 7)
    dxn = _matmul(
        "d_xn_bottom", (D // bn, hm, IN // k_in), NT, [_after(dproj, sums_in), win_g],
        [tile((bh, k_in), lambda n, m, k: (m + hm, k)), tile((bn, k_in), by_nk)], f32_td,
        [tile((bh, bn), lambda n, m, k: (m + hm, n))], _ident, fill=dxn)[0]
    grad_x, gg_mix = _rms_bwd(_after(dxn, sums_in), xf, g_mix, dh1, "norm_mix_bwd", False)

    outs = {}
    done = grad_x
    for tag, four, w, m, v in (("w_down", four_down, w_down, m_w_down, v_w_down),
                               ("w_up", four_up, w_up, m_w_up, v_w_up),
                               ("w_out", four_out, w_out, m_w_out, v_w_out),
                               ("w_in", four_in, w_in, m_w_in, v_w_in)):
        res = _sum_adamw(_after(four, done), w[0], m[0], v[0], "adamw_" + tag)
        done = res[0]
        outs[tag] = [a[None] for a in res]

    small = [("mix_norm_g", gg_mix, mix_norm_g, m_mix_norm_g, v_mix_norm_g),
             ("conv_w", gl_conv, None, None, None),
             ("spatial_w", gl_sw, spatial_w, m_spatial_w, v_spatial_w),
             ("spatial_b", gl_sbt.T, spatial_b, m_spatial_b, v_spatial_b),
             ("conv_out_norm_g", gl_a, conv_out_norm_g, m_conv_out_norm_g, v_conv_out_norm_g),
             ("gmlp_out_norm_g", gl_b, gmlp_out_norm_g, m_gmlp_out_norm_g, v_gmlp_out_norm_g),
             ("mlp_norm_g", gg_mlp, mlp_norm_g, m_mlp_norm_g, v_mlp_norm_g),
             ("final_norm_g", gg_fin, final_norm_g, m_final_norm_g, v_final_norm_g)]
    packed_g = jnp.concatenate([_rows128(g) for _, g, _, _, _ in small], axis=0)
    zeros_cw = jnp.zeros((CONV_K * wc // 128, 128), F32)
    pack = lambda idx: jnp.concatenate(
        [zeros_cw if item[2] is None else _rows128(item[idx]) for item in small], axis=0)
    all_g = _all_gather_async(packed_g, "ag_small_grads", 8)
    sg, sd, sm, sv = _sum_adamw(_after(all_g, done), pack(2), pack(3), pack(4), "adamw_small")
    row = 0
    for name, g, w, _, _ in small:
        n_rows = g.size // 128
        if w is not None:
            outs[name] = [a[row:row + n_rows].reshape(w.shape) for a in (sg, sd, sm, sv)]
        else:
            conv_grad_full = sg[row:row + n_rows].reshape(CONV_K, wc)
        row += n_rows
    cpd = wc // N_DEV
    conv_grad = lax.dynamic_slice(conv_grad_full, (0, my_dev * cpd), (CONV_K, cpd))
    pad8 = lambda a: jnp.pad(a, ((0, HALO - CONV_K), (0, 0)))
    outs["conv_w"] = [a[:CONV_K][None] for a in _sum_adamw(
        pad8(conv_grad)[None], pad8(conv_w[0]), pad8(m_conv_w[0]), pad8(v_conv_w[0]), "adamw_conv_w")]

    order = ["mix_norm_g", "w_in", "conv_w", "spatial_w", "spatial_b", "conv_out_norm_g", "gmlp_out_norm_g",
             "w_out", "mlp_norm_g", "w_up", "w_down", "final_norm_g"]
    result = [loss, grad_x.reshape(n_seq, seq, D)]
    for k in range(4):
        result += [outs[n][k] for n in order]
    return tuple(result)
```

```python
import functools
import math
from typing import Callable, NamedTuple

import jax
import jax.numpy as jnp
from jax import lax
from jax.experimental import pallas as pl
from jax.experimental.pallas import tpu as pltpu
from jax.experimental.pallas import tpu_sc as plsc

F32 = jnp.float32
BF16 = jnp.bfloat16
MESH = pl.DeviceIdType.MESH
HBM = pltpu.HBM

EPS = 1e-5
HEAD = 128
CONV_K = 3
N_DEV = 8
N_CHIP = 4
VMEM_LIMIT_BYTES = 62 * 1024 * 1024

ADAM_LR = 0.001
ADAM_B1 = 0.9
ADAM_B2 = 0.999
ADAM_EPS = 1e-08
ADAM_WD = 0.01
ADAM_STEP = 10

GELU_K0 = math.sqrt(2.0 / math.pi)
GELU_K1 = 0.044715

NN = (((1,), (0,)), ((), ()))
NT = (((1,), (1,)), ((), ()))
TN = (((0,), (0,)), ((), ()))


def _params(semantics):
    return pltpu.CompilerParams(dimension_semantics=semantics, vmem_limit_bytes=VMEM_LIMIT_BYTES)


def _tile(dim, want):
    if dim <= want:
        return dim
    for t in range(want - want % 8, 0, -8):
        if dim % t == 0:
            return t
    raise ValueError((dim, want))


def _owner_block(ref, j, cols):
    if cols is None:
        return ref.at[j]
    return ref.at[:, pl.ds(pl.multiple_of(j * cols, 128), cols)]


def _sibling_copies(p_ref, got_ref, send_sems, recv_sems, cols):
    x, y, c = _place()
    return [pltpu.make_async_remote_copy(
        src_ref=_owner_block(p_ref, 2 * k + (1 - c), cols), dst_ref=got_ref.at[k], send_sem=send_sems.at[k],
        recv_sem=recv_sems.at[k], device_id=(x, y, 1 - c), device_id_type=MESH) for k in range(N_CHIP)]


class _SideJob(NamedTuple):
    c_idx: jax.Array
    arrays: list
    in_tiles: list
    out_shapes: list
    out_tiles: list
    fn: Callable


def _matmul(name, grid, dims, operands, in_specs, out_shapes, out_specs, epilogue, swap=None, side=None, fill=None):
    n_in = len(operands)
    n_out = len(out_shapes)
    nk = grid[2]
    n_host = 0 if swap is None else 1
    n_si = 0 if side is None else len(side.arrays)
    n_so = 0 if side is None else len(side.out_shapes)
    n_fill = 0 if fill is None else 1
    assert nk == 1 or (n_out == 1 and epilogue is _ident and out_shapes[0].dtype == F32)
    assert fill is None or side is None

    def body(*refs):
        if side is not None:
            refs = refs[1:]
        a_ref, b_ref = refs[0], refs[1]
        extra = refs[2:n_in]
        first_out = n_in + n_si + n_host + n_fill
        outs = refs[first_out:first_out + n_out]
        if side is not None:
            step = (pl.program_id(0) * grid[1] + pl.program_id(1)) * grid[2] + pl.program_id(2)
            side.fn(step, refs[n_in:n_in + n_si], refs[first_out + n_out:first_out + n_out + n_so])
        if swap is not None:
            ids = [pl.program_id(d) for d in range(3)]
            copies = _sibling_copies(refs[n_in + n_si], refs[first_out + n_out + n_so], refs[-2], refs[-1], swap[1])

            @pl.when(functools.reduce(jnp.logical_and, [i == 0 for i in ids]))
            def _():
                for cp in copies:
                    cp.start()

        part = lax.dot_general(a_ref[...], b_ref[...], dims, preferred_element_type=F32)

        def finish(acc):
            res = epilogue(acc, *[e[...] for e in extra])
            for o, r in zip(outs, res):
                o[...] = r.astype(o.dtype)

        if nk == 1:
            finish(part)
        else:
            k = pl.program_id(2)

            @pl.when(k == 0)
            def _():
                outs[0][...] = part

            @pl.when(k > 0)
            def _():
                outs[0][...] += part

        if swap is not None:
            @pl.when(functools.reduce(jnp.logical_and, [i == n - 1 for i, n in zip(ids, grid)]))
            def _():
                for cp in copies:
                    cp.wait()

    scratch = []
    semantics = ("parallel", "parallel", "arbitrary")
    operands, in_specs = list(operands), list(in_specs)
    out_shapes, out_specs = list(out_shapes), list(out_specs)
    if side is not None:
        def with_c(spec):
            return pl.BlockSpec(spec.block_shape, lambda n, m, k, c_ref, f=spec.index_map: f(n, m, k))

        def side_tile(shape, index):
            return pl.BlockSpec(shape, lambda n, m, k, c_ref: index((n * grid[1] + m) * grid[2] + k, c_ref[0]))

        in_specs = [with_c(s) for s in in_specs] + [side_tile(*t) for t in side.in_tiles]
        out_specs = [with_c(s) for s in out_specs] + [side_tile(*t) for t in side.out_tiles]
        operands += side.arrays
        out_shapes += side.out_shapes
        semantics = ("arbitrary", "arbitrary", "arbitrary")
    if swap is not None:
        parts, cols = swap
        got_shape = parts.shape[1:] if cols is None else (parts.shape[0], cols)
        hbm = pl.BlockSpec(memory_space=HBM)
        operands, in_specs = operands + [parts], in_specs + [hbm]
        out_shapes = out_shapes + [jax.ShapeDtypeStruct((N_CHIP,) + got_shape, parts.dtype)]
        out_specs = out_specs + [hbm]
        scratch += [pltpu.SemaphoreType.DMA((N_CHIP,)), pltpu.SemaphoreType.DMA((N_CHIP,))]
        semantics = ("arbitrary", "arbitrary", "arbitrary")
    aliases = {}
    if fill is not None:
        aliases = {len(operands): 0}
        operands, in_specs = operands + [fill], in_specs + [pl.BlockSpec(memory_space=pl.ANY)]
    if side is None:
        return pl.pallas_call(
            body, name=name, grid=grid, in_specs=in_specs, out_specs=out_specs, out_shape=out_shapes,
            scratch_shapes=scratch, input_output_aliases=aliases, compiler_params=_params(semantics),
        )(*operands)
    grid_spec = pltpu.PrefetchScalarGridSpec(
        num_scalar_prefetch=1, grid=grid, in_specs=in_specs, out_specs=out_specs, scratch_shapes=scratch)
    return pl.pallas_call(
        body, name=name, grid_spec=grid_spec, out_shape=out_shapes, compiler_params=_params(semantics),
    )(side.c_idx, *operands)


def _add_job(parts, got, c_idx, steps):
    _, R, C = got.shape
    rows = N_CHIP * R
    tr = rows // steps
    assert rows % steps == 0 and R % tr == 0 and tr % 16 == 0, (rows, steps)
    per = R // tr
    if parts.ndim == 3:
        flat = parts.reshape(N_DEV * R, C)
        parts_index = lambda s, c: ((2 * (s // per) + c) * per + s % per, 0)
    else:
        flat = parts
        parts_index = lambda s, c: (s % per, 2 * (s // per) + c)
    by_step = lambda s, c: (s, 0)

    def add(step, ins, outs):
        outs[0][...] = (ins[0][...].astype(F32) + ins[1][...].astype(F32)).astype(outs[0].dtype)

    return _SideJob(
        c_idx, [flat, got.reshape(rows, C)], [((tr, C), parts_index), ((tr, C), by_step)],
        [jax.ShapeDtypeStruct((rows, C), parts.dtype)], [((tr, C), by_step)], add)


def _ident(acc):
    return (acc,)


ROW_TILE = 256


def _rms_fwd(x, g, name):
    T, D = x.shape
    tr = _tile(T, ROW_TILE)

    def body(x_ref, g_ref, o_ref):
        xv = x_ref[...]
        inv = lax.rsqrt(jnp.mean(xv * xv, axis=-1, keepdims=True) + EPS)
        o_ref[...] = (xv * inv * g_ref[...]).astype(o_ref.dtype)

    return pl.pallas_call(
        body, name=name, grid=(T // tr,),
        in_specs=[pl.BlockSpec((tr, D), lambda i: (i, 0)), pl.BlockSpec((1, D), lambda i: (0, 0))],
        out_specs=pl.BlockSpec((tr, D), lambda i: (i, 0)),
        out_shape=jax.ShapeDtypeStruct((T, D), BF16),
        compiler_params=_params(("parallel",)),
    )(x, g)


def _rms_bwd_rows(i, ins, outs):
    dy_ref, x_ref, g_ref, dres_ref = ins
    dx_ref, gg_ref = outs[0], outs[-1]
    xv = x_ref[...]
    dyv = dy_ref[...]
    inv = lax.rsqrt(jnp.mean(xv * xv, axis=-1, keepdims=True) + EPS)
    gd = dyv * g_ref[...]
    dot = jnp.mean(gd * xv, axis=-1, keepdims=True)
    dx = dres_ref[...] + (inv * gd - xv * (inv * inv * inv * dot))
    dx_ref[...] = dx
    if len(outs) == 3:
        outs[1][...] = dx.astype(BF16)
    part = jnp.sum(dyv * xv * inv, axis=0, keepdims=True)

    @pl.when(i == 0)
    def _():
        gg_ref[...] = part

    @pl.when(i > 0)
    def _():
        gg_ref[...] += part


def _rms_bwd_job(dy, x, g, dres, c_idx, steps):
    T, D = x.shape
    tr = T // steps
    assert T % steps == 0 and tr % 16 == 0, (T, steps)
    rows = ((tr, D), lambda s, c: (s, 0))
    vec = ((1, D), lambda s, c: (0, 0))
    return _SideJob(
        c_idx, [dy, x, g, dres], [rows, rows, vec, rows],
        [jax.ShapeDtypeStruct((T, D), F32), jax.ShapeDtypeStruct((T, D), BF16), jax.ShapeDtypeStruct((1, D), F32)],
        [rows, rows, vec], _rms_bwd_rows)


def _rms_bwd(dy, x, g, dres, name, want_bf16):
    T, D = x.shape
    tr = _tile(T, ROW_TILE)

    def body(dy_ref, x_ref, g_ref, dres_ref, *outs):
        _rms_bwd_rows(pl.program_id(0), (dy_ref, x_ref, g_ref, dres_ref), outs)

    row = pl.BlockSpec((tr, D), lambda i: (i, 0))
    vec = pl.BlockSpec((1, D), lambda i: (0, 0))
    out_shape = [jax.ShapeDtypeStruct((T, D), F32)]
    out_specs = [row]
    if want_bf16:
        out_shape.append(jax.ShapeDtypeStruct((T, D), BF16))
        out_specs.append(row)
    out_shape.append(jax.ShapeDtypeStruct((1, D), F32))
    out_specs.append(vec)
    return pl.pallas_call(
        body, name=name, grid=(T // tr,), in_specs=[row, row, vec, row],
        out_specs=out_specs, out_shape=out_shape, compiler_params=_params(("arbitrary",)),
    )(dy, x, g, dres)


def _loss_head(h1, mlp, target, g):
    T, D = h1.shape
    tr = _tile(T, ROW_TILE)

    def body(h1_ref, mlp_ref, t_ref, g_ref, dh_ref, dhb_ref, gg_ref, loss_ref):
        i = pl.program_id(0)
        hv = h1_ref[...] + mlp_ref[...]
        gv = g_ref[...]
        inv = lax.rsqrt(jnp.mean(hv * hv, axis=-1, keepdims=True) + EPS)
        diff = hv * inv * gv - t_ref[...]
        lpart = 0.5 * jnp.sum(jnp.mean(diff * diff, axis=-1, keepdims=True), axis=0, keepdims=True)
        dout = diff * (1.0 / D)
        gd = dout * gv
        dot = jnp.mean(gd * hv, axis=-1, keepdims=True)
        dh = inv * gd - hv * (inv * inv * inv * dot)
        dh_ref[...] = dh
        dhb_ref[...] = dh.astype(BF16)
        part = jnp.sum(dout * hv * inv, axis=0, keepdims=True)
        lrow = jnp.broadcast_to(lpart, (1, 128))

        @pl.when(i == 0)
        def _():
            gg_ref[...] = part
            loss_ref[...] = lrow

        @pl.when(i > 0)
        def _():
            gg_ref[...] += part
            loss_ref[...] += lrow

    row = pl.BlockSpec((tr, D), lambda i: (i, 0))
    vec = pl.BlockSpec((1, D), lambda i: (0, 0))
    return pl.pallas_call(
        body, name="loss_head", grid=(T // tr,), in_specs=[row, row, row, vec],
        out_specs=[row, row, vec, pl.BlockSpec((1, 128), lambda i: (0, 0))],
        out_shape=[jax.ShapeDtypeStruct((T, D), F32), jax.ShapeDtypeStruct((T, D), BF16),
                   jax.ShapeDtypeStruct((1, D), F32), jax.ShapeDtypeStruct((1, 128), F32)],
        compiler_params=_params(("arbitrary",)),
    )(h1, mlp, target, g)


HALO = 8


def _gelu_parts(x):
    th = jnp.tanh(GELU_K0 * (x + GELU_K1 * (x * x * x)))
    return x * (0.5 * (1.0 + th)), th


def _gelu_grad(x, th):
    return 0.5 * (1.0 + th) + (0.5 * GELU_K0) * x * (1.0 - th * th) * (1.0 + (3.0 * GELU_K1) * (x * x))


def _conv_fwd(b_ref, c_ref, h_ref, ch_ref, hh_ref, w_ref, first):
    tt, wc = c_ref.shape
    c = c_ref[...]
    h = h_ref[...]
    hc = c * h
    prev1 = jnp.where(first, 0.0, ch_ref[HALO - 1:HALO, :] * hh_ref[HALO - 1:HALO, :])
    prev2 = jnp.where(first, 0.0, ch_ref[HALO - 2:HALO - 1, :] * hh_ref[HALO - 2:HALO - 1, :])
    row = lax.broadcasted_iota(jnp.int32, (tt, wc), 0)
    m1 = jnp.where(row == 0, prev1, pltpu.roll(hc, 1, 0))
    m2 = jnp.where(row == 0, prev2, jnp.where(row == 1, prev1, pltpu.roll(hc, 2, 0)))
    conv = w_ref[0:1, :] * m2 + w_ref[1:2, :] * m1 + w_ref[2:3, :] * hc
    return c, h, hc, m1, m2, conv, b_ref[...] * conv


def _tril():
    r = lax.broadcasted_iota(jnp.int32, (HEAD, HEAD), 0)
    s = lax.broadcasted_iota(jnp.int32, (HEAD, HEAD), 1)
    return r >= s


def _spatial_fwd(gvb, sw_ref, sbt_ref, s_scr):
    n_head = sw_ref.shape[0]
    tri = _tril()
    for hd in range(n_head):
        sl = slice(hd * HEAD, (hd + 1) * HEAD)
        wm = jnp.where(tri, sw_ref[hd], 0.0).astype(BF16)
        s_scr[:, sl] = jnp.dot(wm, gvb[:, sl], preferred_element_type=F32) + sbt_ref[:, hd:hd + 1]


def _mixer_specs(wc, row_of):
    def grp(g):
        return pl.BlockSpec((HEAD, wc), lambda *ids: (row_of(*ids), g))

    def halo(g):
        return pl.BlockSpec((HALO, wc), lambda *ids: (jnp.maximum(row_of(*ids) * (HEAD // HALO) - 1, 0), g))

    return grp, halo


def _mixer_fwd(proj, conv_w, sw, sbt, g_a, g_b, n_seq, seq):
    T, w5 = proj.shape
    wc = w5 // 5
    n_head = wc // HEAD
    nt = seq // HEAD

    def row_of(s, i):
        return s * nt + i

    grp, halo = _mixer_specs(wc, row_of)

    def body(b_ref, c_ref, h_ref, u_ref, v_ref, ch_ref, hh_ref, w_ref, sw_ref, sbt_ref, ga_ref, gb_ref,
             y_ref, s_scr):
        i = pl.program_id(1)
        ya = _conv_fwd(b_ref, c_ref, h_ref, ch_ref, hh_ref, w_ref, i == 0)[-1]
        inv = lax.rsqrt(jnp.mean(ya * ya, axis=-1, keepdims=True) + EPS)
        y_ref[:, :wc] = (ya * inv * ga_ref[...]).astype(BF16)
        gu, _ = _gelu_parts(u_ref[...])
        gv, _ = _gelu_parts(v_ref[...])
        _spatial_fwd(gv.astype(BF16), sw_ref, sbt_ref, s_scr)
        yb = gu * s_scr[...]
        inv = lax.rsqrt(jnp.mean(yb * yb, axis=-1, keepdims=True) + EPS)
        y_ref[:, wc:] = (yb * inv * gb_ref[...]).astype(BF16)

    const2 = lambda shape: pl.BlockSpec(shape, lambda s, i: (0, 0))
    return pl.pallas_call(
        body, name="mixer_fwd", grid=(n_seq, nt),
        in_specs=[grp(0), grp(1), grp(2), grp(3), grp(4), halo(1), halo(2),
                  const2((CONV_K, wc)), pl.BlockSpec((n_head, HEAD, HEAD), lambda s, i: (0, 0, 0)),
                  const2((HEAD, n_head)), const2((1, wc)), const2((1, wc))],
        out_specs=pl.BlockSpec((HEAD, 2 * wc), lambda s, i: (s * nt + i, 0)),
        out_shape=jax.ShapeDtypeStruct((T, 2 * wc), BF16),
        scratch_shapes=[pltpu.VMEM((HEAD, wc), F32)],
        compiler_params=_params(("parallel", "parallel")),
    )(proj, proj, proj, proj, proj, proj, proj, conv_w, sw, sbt, g_a, g_b)


def _mixer_bwd(proj, dy, conv_w, sw, sbt, g_a, g_b, n_seq, seq):
    T, w5 = proj.shape
    wc = w5 // 5
    n_head = wc // HEAD
    nt = seq // HEAD
    tt = HEAD

    def row_of(s, ir):
        return s * nt + (nt - 1 - ir)

    grp, halo = _mixer_specs(wc, row_of)

    def body(b_ref, c_ref, h_ref, u_ref, v_ref, ch_ref, hh_ref, dya_ref, dyb_ref, w_ref, sw_ref, sbt_ref,
             ga_ref, gb_ref, dp_ref, gw_ref, gga_ref, ggb_ref, gsw_ref, gsb_ref,
             carry_scr, s_scr, t_scr, dsum_scr):
        s_id = pl.program_id(0)
        ir = pl.program_id(1)
        first_tile = jnp.logical_and(s_id == 0, ir == 0)
        last_tile = jnp.logical_and(s_id == n_seq - 1, ir == nt - 1)

        def conv_part():
            @pl.when(ir == 0)
            def _():
                carry_scr[...] = jnp.zeros_like(carry_scr)

            c, h, hc, m1, m2, conv, ya = _conv_fwd(b_ref, c_ref, h_ref, ch_ref, hh_ref, w_ref, ir == nt - 1)
            inv = lax.rsqrt(jnp.mean(ya * ya, axis=-1, keepdims=True) + EPS)
            dyn = dya_ref[...]
            gd = dyn * ga_ref[...]
            dot = jnp.mean(gd * ya, axis=-1, keepdims=True)
            dya = inv * gd - ya * (inv * inv * inv * dot)
            gg = jnp.sum(dyn * ya * inv, axis=0, keepdims=True)
            dconv = dya * b_ref[...]
            nxt0 = carry_scr[0:1, :]
            nxt1 = carry_scr[1:2, :]
            row = lax.broadcasted_iota(jnp.int32, (tt, wc), 0)
            p1 = jnp.where(row == tt - 1, nxt0, pltpu.roll(dconv, tt - 1, 0))
            p2 = jnp.where(row == tt - 2, nxt0, jnp.where(row == tt - 1, nxt1, pltpu.roll(dconv, tt - 2, 0)))
            dhc = w_ref[2:3, :] * dconv + w_ref[1:2, :] * p1 + w_ref[0:1, :] * p2
            carry_scr[...] = dconv[0:HALO, :]
            dp_ref[:, 0:wc] = (dya * conv).astype(BF16)
            dp_ref[:, wc:2 * wc] = (dhc * h).astype(BF16)
            dp_ref[:, 2 * wc:3 * wc] = (dhc * c).astype(BF16)
            gw0 = jnp.sum(dconv * m2, axis=0, keepdims=True)
            gw1 = jnp.sum(dconv * m1, axis=0, keepdims=True)
            gw2 = jnp.sum(dconv * hc, axis=0, keepdims=True)

            @pl.when(first_tile)
            def _():
                gw_ref[0:1, :] = gw0
                gw_ref[1:2, :] = gw1
                gw_ref[2:3, :] = gw2
                gga_ref[...] = gg

            @pl.when(jnp.logical_not(first_tile))
            def _():
                gw_ref[0:1, :] += gw0
                gw_ref[1:2, :] += gw1
                gw_ref[2:3, :] += gw2
                gga_ref[...] += gg

        def gate_part():
            u = u_ref[...]
            v = v_ref[...]
            gu, thu = _gelu_parts(u)
            gv, thv = _gelu_parts(v)
            gvb = gv.astype(BF16)
            _spatial_fwd(gvb, sw_ref, sbt_ref, s_scr)
            sv = s_scr[...]
            yb = gu * sv
            inv = lax.rsqrt(jnp.mean(yb * yb, axis=-1, keepdims=True) + EPS)
            dyn = dyb_ref[...]
            gd = dyn * gb_ref[...]
            dot = jnp.mean(gd * yb, axis=-1, keepdims=True)
            dyb = inv * gd - yb * (inv * inv * inv * dot)
            gg = jnp.sum(dyn * yb * inv, axis=0, keepdims=True)
            ds = dyb * gu
            dsb = ds.astype(BF16)
            tri = _tril()

            @pl.when(first_tile)
            def _():
                ggb_ref[...] = gg
                dsum_scr[...] = ds
                gsw_ref[...] = jnp.zeros_like(gsw_ref)

            @pl.when(jnp.logical_not(first_tile))
            def _():
                ggb_ref[...] += gg
                dsum_scr[...] += ds

            for hd in range(n_head):
                sl = slice(hd * HEAD, (hd + 1) * HEAD)
                wm = jnp.where(tri, sw_ref[hd], 0.0).astype(BF16)
                t_scr[:, sl] = lax.dot_general(wm, dsb[:, sl], TN, preferred_element_type=F32)
                gsw_ref[hd] += lax.dot_general(dsb[:, sl], gvb[:, sl], NT, preferred_element_type=F32)
            dp_ref[:, 3 * wc:4 * wc] = (dyb * sv * _gelu_grad(u, thu)).astype(BF16)
            dp_ref[:, 4 * wc:5 * wc] = (t_scr[...] * _gelu_grad(v, thv)).astype(BF16)

            @pl.when(last_tile)
            def _():
                for hd in range(n_head):
                    sl = slice(hd * HEAD, (hd + 1) * HEAD)
                    gsw_ref[hd] = jnp.where(tri, gsw_ref[hd], 0.0)
                    gsb_ref[:, hd:hd + 1] = jnp.sum(dsum_scr[:, sl], axis=1, keepdims=True)

        conv_part()
        gate_part()

    const2 = lambda shape: pl.BlockSpec(shape, lambda s, i: (0, 0))
    const3 = pl.BlockSpec((n_head, HEAD, HEAD), lambda s, i: (0, 0, 0))
    dy_spec = lambda col: pl.BlockSpec((tt, wc), lambda s, ir: (row_of(s, ir), col))
    return pl.pallas_call(
        body, name="mixer_bwd", grid=(n_seq, nt),
        in_specs=[grp(0), grp(1), grp(2), grp(3), grp(4), halo(1), halo(2), dy_spec(0), dy_spec(1),
                  const2((CONV_K, wc)), const3, const2((HEAD, n_head)), const2((1, wc)), const2((1, wc))],
        out_specs=[pl.BlockSpec((tt, 5 * wc), lambda s, ir: (row_of(s, ir), 0)),
                   const2((CONV_K, wc)), const2((1, wc)), const2((1, wc)), const3, const2((HEAD, n_head))],
        out_shape=[jax.ShapeDtypeStruct((T, 5 * wc), BF16), jax.ShapeDtypeStruct((CONV_K, wc), F32),
                   jax.ShapeDtypeStruct((1, wc), F32), jax.ShapeDtypeStruct((1, wc), F32),
                   jax.ShapeDtypeStruct((n_head, HEAD, HEAD), F32), jax.ShapeDtypeStruct((HEAD, n_head), F32)],
        scratch_shapes=[pltpu.VMEM((HALO, wc), F32), pltpu.VMEM((tt, wc), F32), pltpu.VMEM((tt, wc), F32),
                        pltpu.VMEM((tt, wc), F32)],
        compiler_params=_params(("arbitrary", "arbitrary")),
    )(proj, proj, proj, proj, proj, proj, proj, dy, dy, conv_w, sw, sbt, g_a, g_b)


def _place():
    return lax.axis_index("x"), lax.axis_index("y"), lax.axis_index("c")


def _other_chips(x, y):
    return [(1 - x, y), (x, 1 - y), (1 - x, 1 - y)]


def _all_gather_async(blk, name, collective_id, side_by_side=False):
    cols = blk.shape[1] if side_by_side else None
    out_shape = (blk.shape[0], N_DEV * cols) if side_by_side else (N_DEV,) + blk.shape

    def body(x_ref, out_ref, send_sems, recv_sems, local_sem):
        x, y, c = _place()
        me, sibling = (x, y, c), (x, y, 1 - c)
        x_nbr, y_nbr, diagonal = (1 - x, y, c), (x, 1 - y, c), (1 - x, 1 - y, c)
        near = (c * x + (1 - c) * (1 - x), c * (1 - y) + (1 - c) * y, c)
        far = ((1 - c) * x + c * (1 - x), (1 - c) * (1 - y) + c * y, c)
        _handshake([sibling, x_nbr, y_nbr])

        def slot(px, py, pc):
            return _owner_block(out_ref, 4 * px + 2 * py + pc, cols)

        def copy(k, block, to, src=None):
            return pltpu.make_async_remote_copy(
                src_ref=slot(*block) if src is None else src, dst_ref=slot(*block),
                send_sem=send_sems.at[k], recv_sem=recv_sems.at[k], device_id=to, device_id_type=MESH)

        mine = pltpu.make_async_copy(x_ref, slot(*me), local_sem)
        mine.start()
        sent = [copy(0, me, sibling, src=x_ref), copy(1, me, x_nbr, src=x_ref), copy(2, me, y_nbr, src=x_ref)]
        for cp in sent:
            cp.start()
        copy(1 + c, near, me).wait_recv()
        sent += [copy(3, near, far), copy(4, near, sibling)]
        sent[-2].start()
        sent[-1].start()
        copy(2 - c, far, me).wait_recv()
        sent.append(copy(5, far, sibling))
        sent[-1].start()
        copy(3, diagonal, me).wait_recv()
        sent.append(copy(6, diagonal, sibling))
        sent[-1].start()
        for k in (0, 4, 5, 6):
            copy(k, sibling, me).wait_recv()
        for cp in sent:
            cp.wait_send()
        mine.wait()

    return _sequencer_call(
        body, name, collective_id, jax.ShapeDtypeStruct(out_shape, blk.dtype),
        [pltpu.SemaphoreType.DMA((7,)), pltpu.SemaphoreType.DMA((7,)), pltpu.SemaphoreType.DMA], blk)


def _sequencer_call(body, name, collective_id, out_type, scratch_types, operand):
    return pl.kernel(
        body, name=name, out_type=out_type, mesh=plsc.ScalarSubcoreMesh(axis_name="seq_core", num_cores=1),
        scratch_types=scratch_types, compiler_params=pltpu.CompilerParams(collective_id=collective_id),
    )(operand)


def _handshake(peers):
    barrier = pltpu.get_barrier_semaphore()
    for peer in peers:
        pl.semaphore_signal(barrier, inc=1, device_id=peer, device_id_type=MESH)
    pl.semaphore_wait(barrier, len(peers))


def _add_sibling(parts, got, c_idx, name):
    _, R, C = got.shape
    tr, tc = _tile(R, 1024), _tile(C, 2048)

    def body(c_ref, p_ref, g_ref, o_ref):
        o_ref[...] = (p_ref[...].astype(F32) + g_ref[...].astype(F32)).astype(o_ref.dtype)

    if parts.ndim == 3:
        parts_spec = pl.BlockSpec((None, tr, tc), lambda k, i, j, c_ref: (2 * k + c_ref[0], i, j))
    else:
        parts_spec = pl.BlockSpec((tr, tc), lambda k, i, j, c_ref: (i, (2 * k + c_ref[0]) * (C // tc) + j))
    grid_spec = pltpu.PrefetchScalarGridSpec(
        num_scalar_prefetch=1, grid=(N_CHIP, R // tr, C // tc),
        in_specs=[parts_spec, pl.BlockSpec((None, tr, tc), lambda k, i, j, c_ref: (k, i, j))],
        out_specs=pl.BlockSpec((None, tr, tc), lambda k, i, j, c_ref: (k, i, j)))
    return pl.pallas_call(
        body, name=name, grid_spec=grid_spec, out_shape=jax.ShapeDtypeStruct((N_CHIP, R, C), parts.dtype),
        compiler_params=_params(("parallel", "parallel", "parallel")),
    )(c_idx, parts, got)


def _scatter_to_chips(sums, name, collective_id):
    _, R, C = sums.shape

    def body(q_ref, got_ref, send_sems, recv_sems, local_sem):
        x, y, c = _place()
        _handshake([(*chip, c) for chip in _other_chips(x, y)])
        my_chip = 2 * x + y
        mine = pltpu.make_async_copy(q_ref.at[my_chip], got_ref.at[my_chip], local_sem)
        mine.start()
        copies = [pltpu.make_async_remote_copy(
            src_ref=q_ref.at[2 * px + py], dst_ref=got_ref.at[my_chip], send_sem=send_sems.at[j],
            recv_sem=recv_sems.at[j], device_id=(px, py, c), device_id_type=MESH)
            for j, (px, py) in enumerate(_other_chips(x, y))]
        for cp in copies:
            cp.start()
        for cp in copies:
            cp.wait()
        mine.wait()

    return _sequencer_call(
        body, name, collective_id, jax.ShapeDtypeStruct((N_CHIP, R, C), sums.dtype),
        [pltpu.SemaphoreType.DMA((3,)), pltpu.SemaphoreType.DMA((3,)), pltpu.SemaphoreType.DMA], sums)


def _adamw_math(w, g, m, v):
    m = ADAM_B1 * m + (1.0 - ADAM_B1) * g
    v = ADAM_B2 * v + (1.0 - ADAM_B2) * (g * g)
    m_hat = m / (1.0 - ADAM_B1 ** ADAM_STEP)
    v_hat = v / (1.0 - ADAM_B2 ** ADAM_STEP)
    delta = -ADAM_LR * (m_hat / (jnp.sqrt(v_hat) + ADAM_EPS) + ADAM_WD * w)
    return delta, m, v


def _sum_adamw_tiles(p, w, m, v):
    g = p[0].astype(F32)
    for k in range(1, p.shape[0]):
        g = g + p[k].astype(F32)
    delta, mn, vn = _adamw_math(w, g, m, v)
    return g, delta, mn, vn


def _sum_adamw(parts, w, m, v, name):
    n_parts, R, C = parts.shape
    tr, tc = _tile(R, 512), _tile(C, 1024)

    def body(p_ref, w_ref, m_ref, v_ref, g_out, d_out, m_out, v_out):
        res = _sum_adamw_tiles(p_ref[...], w_ref[...], m_ref[...], v_ref[...])
        for o, r in zip((g_out, d_out, m_out, v_out), res):
            o[...] = r

    blk = pl.BlockSpec((tr, tc), lambda i, j: (i, j))
    shp = jax.ShapeDtypeStruct((R, C), F32)
    return pl.pallas_call(
        body, name=name, grid=(R // tr, C // tc),
        in_specs=[pl.BlockSpec((n_parts, tr, tc), lambda i, j: (0, i, j)), blk, blk, blk],
        out_specs=[blk, blk, blk, blk], out_shape=[shp, shp, shp, shp],
        compiler_params=_params(("parallel", "parallel")),
    )(parts, w, m, v)


def _after(value, dep):
    return lax.optimization_barrier((value, dep))[0]


def _rows128(a):
    return a.reshape(-1, 128)


def kernel(x, mix_norm_g, w_in, conv_w, spatial_w, spatial_b, conv_out_norm_g, gmlp_out_norm_g, w_out, mlp_norm_g, w_up, w_down, final_norm_g, loss_target, m_mix_norm_g, m_w_in, m_conv_w, m_spatial_w, m_spatial_b, m_conv_out_norm_g, m_gmlp_out_norm_g, m_w_out, m_mlp_norm_g, m_w_up, m_w_down, m_final_norm_g, v_mix_norm_g, v_w_in, v_conv_w, v_spatial_w, v_spatial_b, v_conv_out_norm_g, v_gmlp_out_norm_g, v_w_out, v_mlp_norm_g, v_w_up, v_w_down, v_final_norm_g):
    n_seq, seq, D = x.shape
    T = n_seq * seq
    n_in = w_in.shape[2]
    n_out = w_out.shape[1]
    n_up = w_up.shape[2]
    wc = conv_w.shape[2] * N_DEV
    n_head = wc // HEAD
    FF = n_up * N_DEV
    assert N_DEV * n_in == 5 * wc and seq % HEAD == 0 and D == 2 * wc

    c_idx = lax.axis_index("c").astype(jnp.int32).reshape(1)
    my_dev = 4 * lax.axis_index("x") + 2 * lax.axis_index("y") + lax.axis_index("c")

    xf = x.reshape(T, D)
    tgt = loss_target.reshape(T, D)

    cast = lambda w: w[0].astype(BF16)
    win_g = _all_gather_async(cast(w_in), "ag_w_in", 0, side_by_side=True)
    cw_pad = jnp.pad(conv_w[0], ((0, HALO - CONV_K), (0, 0)))
    cw_g = _all_gather_async(cw_pad, "ag_conv_w", 9)
    conv_full = jnp.transpose(cw_g[:, :CONV_K, :], (1, 0, 2)).reshape(CONV_K, wc)
    wout_g = _all_gather_async(cast(w_out), "ag_w_out", 1).reshape(D, D)
    wup_g = _all_gather_async(cast(w_up), "ag_w_up", 2, side_by_side=True)
    wdown_g = _all_gather_async(cast(w_down), "ag_w_down", 3).reshape(FF, D)

    sw = spatial_w[0]
    sbt = spatial_b[0].T
    g_mix, g_a, g_b, g_mlp = mix_norm_g, conv_out_norm_g, gmlp_out_norm_g, mlp_norm_g
    g_fin = final_norm_g.reshape(1, D)

    IN = N_DEV * n_in
    bp = _tile(T, 1024)
    bm = _tile(T, 1024)
    bn = _tile(D, 1024)
    bu = _tile(FF, 1024)
    bw = _tile(D, 512)
    bg = _tile(D, 1024)
    k_ff = _tile(FF, 4096)
    k_in = _tile(IN, 5120)

    def tile(shape, index):
        return pl.BlockSpec(shape, index)

    by_m0 = lambda n, m, k: (m, 0)
    by_0n = lambda n, m, k: (0, n)
    by_n0 = lambda n, m, k: (n, 0)
    by_mn = lambda n, m, k: (m, n)
    by_mk = lambda n, m, k: (m, k)
    by_kn = lambda n, m, k: (k, n)
    by_nk = lambda n, m, k: (n, k)
    by_0m = lambda n, m, k: (0, m)
    f32_td = [jax.ShapeDtypeStruct((T, D), F32)]

    xn = _rms_fwd(xf, g_mix, "norm_mix")
    proj = _matmul(
        "proj", (N_DEV, T // bp, 1), NN, [xn, win_g], [tile((bp, D), by_m0), tile((D, n_in), by_0n)],
        [jax.ShapeDtypeStruct((T, IN), F32)], [tile((bp, n_in), by_mn)], _ident)[0]
    y = _mixer_fwd(proj, conv_full, sw, sbt, g_a, g_b, n_seq, seq)
    h1 = _matmul(
        "out_proj", (D // bn, T // bp, 1), NN, [y, wout_g, xf],
        [tile((bp, D), by_m0), tile((D, bn), by_0n), tile((bp, bn), by_mn)],
        f32_td, [tile((bp, bn), by_mn)], lambda acc, res: (res + acc,))[0]
    xn2 = _rms_fwd(h1, g_mlp, "norm_mlp")

    def up_epilogue(acc):
        r = jnp.maximum(acc, 0.0)
        return acc, r * r

    up, act = _matmul(
        "up_proj", (FF // bu, T // bm, 1), NN, [xn2, wup_g], [tile((bm, D), by_m0), tile((D, bu), by_0n)],
        [jax.ShapeDtypeStruct((T, FF), BF16)] * 2, [tile((bm, bu), by_mn)] * 2, up_epilogue)
    mlp = _matmul(
        "down_proj", (D // bn, T // bm, FF // k_ff), NN, [act, wdown_g],
        [tile((bm, k_ff), by_mk), tile((k_ff, bn), by_kn)], f32_td, [tile((bm, bn), by_mn)], _ident)[0]

    dh2, dh2b, gg_fin, loss_row = _loss_head(h1, mlp, tgt, g_fin)

    gp_down = _matmul(
        "gw_down", (D // bn, FF // bg, 1), TN, [act, dh2b], [tile((T, bg), by_0m), tile((T, bn), by_0n)],
        [jax.ShapeDtypeStruct((FF, D), BF16)], [tile((bg, bn), by_mn)], _ident)[0].reshape(N_DEV, n_up, D)
    dup, got_down = _matmul(
        "d_act", (FF // bu, T // bm, 1), NT, [dh2b, wdown_g, up],
        [tile((bm, D), by_m0), tile((bu, D), by_n0), tile((bm, bu), by_mn)],
        [jax.ShapeDtypeStruct((T, FF), BF16)], [tile((bm, bu), by_mn)],
        lambda acc, u: (acc * (2.0 * jnp.maximum(u.astype(F32), 0.0)),), swap=(gp_down, None))
    dxn2, sums_down = _matmul(
        "d_xn2", (D // bn, T // bm, FF // k_ff), NT, [dup, wup_g],
        [tile((bm, k_ff), by_mk), tile((bn, k_ff), by_nk)], f32_td, [tile((bm, bn), by_mn)], _ident,
        side=_add_job(gp_down, got_down, c_idx, (D // bn) * (T // bm) * (FF // k_ff)))
    sums_down = sums_down.reshape(N_CHIP, n_up, D)
    four_down = _scatter_to_chips(sums_down, "rs_chips_w_down", 4)
    gp_up, dh1, dh1b, gg_mlp = _matmul(
        "gw_up", (FF // bu, D // bg, 1), TN, [xn2, _after(dup, sums_down)],
        [tile((T, bg), by_0m), tile((T, bu), by_0n)],
        [jax.ShapeDtypeStruct((D, FF), BF16)], [tile((bg, bu), by_mn)], _ident,
        side=_rms_bwd_job(dxn2, h1, g_mlp, dh2, c_idx, (FF // bu) * (D // bg)))

    dy, got_up = _matmul(
        "d_y", (D // bn, T // bm, 1), NT, [dh1b, wout_g], [tile((bm, D), by_m0), tile((bn, D), by_n0)],
        f32_td, [tile((bm, bn), by_mn)], _ident, swap=(gp_up, n_up))
    gp_out, sums_up = _matmul(
        "gw_out", (D // bn, D // bw, 1), TN, [y, _after(dh1b, dy)], [tile((T, bw), by_0m), tile((T, bn), by_0n)],
        [jax.ShapeDtypeStruct((D, D), BF16)], [tile((bw, bn), by_mn)], _ident,
        side=_add_job(gp_up, got_up, c_idx, (D // bn) * (D // bw)))
    gp_out = gp_out.reshape(N_DEV, n_out, D)
    sums_up = sums_up.reshape(N_CHIP, D, n_up)
    four_up = _scatter_to_chips(sums_up, "rs_chips_w_up", 5)
    dproj, gl_conv, gl_a, gl_b, gl_sw, gl_sbt = _mixer_bwd(
        proj, _after(dy, sums_up), conv_full, sw, sbt, g_a, g_b, n_seq, seq)
    gp_in, got_out = _matmul(
        "gw_in", (N_DEV, D // bg, 1), TN, [xn, dproj], [tile((T, bg), by_0m), tile((T, n_in), by_0n)],
        [jax.ShapeDtypeStruct((D, IN), BF16)], [tile((bg, n_in), by_mn)], _ident, swap=(gp_out, None))
    sums_out = _add_sibling(gp_out, got_out, c_idx, "rs_add_w_out")
    four_out = _scatter_to_chips(sums_out, "rs_chips_w_out", 6)
    bh = _tile(T // 2, bm)
    hm = (T // 2) // bh
    dproj = _after(dproj, sums_out)
    dxn, got_in = _matmul(
        "d_xn_top", (D // bn, hm, IN // k_in), NT, [dproj, win_g],
        [tile((bh, k_in), by_mk), tile((bn, k_in), by_nk)], f32_td, [tile((bh, bn), by_mn)], _ident,
        swap=(gp_in, n_in))
    sums_in = _add_sibling(gp_in, got_in, c_idx, "rs_add_w_in")
    four_in = _scatter_to_chips(sums_in, "rs_chips_w_in", 7)
    dxn = _matmul(
        "d_xn_bottom", (D // bn, hm, IN // k_in), NT, [_after(dproj, sums_in), win_g],
        [tile((bh, k_in), lambda n, m, k: (m + hm, k)), tile((bn, k_in), by_nk)], f32_td,
        [tile((bh, bn), lambda n, m, k: (m + hm, n))], _ident, fill=dxn)[0]
    grad_x, gg_mix = _rms_bwd(_after(dxn, sums_in), xf, g_mix, dh1, "norm_mix_bwd", False)

    outs = {}
    done = grad_x
    for tag, four, w, m, v in (("w_down", four_down, w_down, m_w_down, v_w_down),
                               ("w_up", four_up, w_up, m_w_up, v_w_up),
                               ("w_out", four_out, w_out, m_w_out, v_w_out),
                               ("w_in", four_in, w_in, m_w_in, v_w_in)):
        res = _sum_adamw(_after(four, done), w[0], m[0], v[0], "adamw_" + tag)
        done = res[0]
        outs[tag] = [a[None] for a in res]

    small = [("mix_norm_g", gg_mix, mix_norm_g, m_mix_norm_g, v_mix_norm_g),
             ("conv_w", gl_conv, None, None, None),
             ("spatial_w", gl_sw, spatial_w, m_spatial_w, v_spatial_w),
             ("spatial_b", gl_sbt.T, spatial_b, m_spatial_b, v_spatial_b),
             ("conv_out_norm_g", gl_a, conv_out_norm_g, m_conv_out_norm_g, v_conv_out_norm_g),
             ("gmlp_out_norm_g", gl_b, gmlp_out_norm_g, m_gmlp_out_norm_g, v_gmlp_out_norm_g),
             ("mlp_norm_g", gg_mlp, mlp_norm_g, m_mlp_norm_g, v_mlp_norm_g),
             ("final_norm_g", gg_fin, final_norm_g, m_final_norm_g, v_final_norm_g),
             ("loss", jnp.broadcast_to(loss_row, (64, 128)), None, None, None)]
    packed_g = jnp.concatenate([_rows128(g) for _, g, _, _, _ in small], axis=0)
    pack = lambda idx: jnp.concatenate(
        [jnp.zeros((item[1].size // 128, 128), F32) if item[2] is None else _rows128(item[idx])
         for item in small], axis=0)
    all_g = _all_gather_async(packed_g, "ag_small_grads", 8)
    sg, sd, sm, sv = _sum_adamw(_after(all_g, done), pack(2), pack(3), pack(4), "adamw_small")
    row = 0
    for name, g, w, _, _ in small:
        n_rows = g.size // 128
        if w is not None:
            outs[name] = [a[row:row + n_rows].reshape(w.shape) for a in (sg, sd, sm, sv)]
        elif name == "conv_w":
            conv_grad_full = sg[row:row + n_rows].reshape(CONV_K, wc)
        else:
            loss = sg[row, 0]
        row += n_rows
    cpd = wc // N_DEV
    conv_grad = lax.dynamic_slice(conv_grad_full, (0, my_dev * cpd), (CONV_K, cpd))
    pad8 = lambda a: jnp.pad(a, ((0, HALO - CONV_K), (0, 0)))
    outs["conv_w"] = [a[:CONV_K][None] for a in _sum_adamw(
        pad8(conv_grad)[None], pad8(conv_w[0]), pad8(m_conv_w[0]), pad8(v_conv_w[0]), "adamw_conv_w")]

    order = ["mix_norm_g", "w_in", "conv_w", "spatial_w", "spatial_b", "conv_out_norm_g", "gmlp_out_norm_g",
             "w_out", "mlp_norm_g", "w_up", "w_down", "final_norm_g"]
    result = [loss, grad_x.reshape(n_seq, seq, D)]
    for k in range(4):
        result += [outs[n][k] for n in order]
    return tuple(result)
```

```python
import functools
import math
from typing import Callable, NamedTuple

import jax
import jax.numpy as jnp
from jax import lax
from jax.experimental import pallas as pl
from jax.experimental.pallas import tpu as pltpu
from jax.experimental.pallas import tpu_sc as plsc

F32 = jnp.float32
BF16 = jnp.bfloat16
MESH = pl.DeviceIdType.MESH
HBM = pltpu.HBM

EPS = 1e-5
HEAD = 128
CONV_K = 3
N_DEV = 8
N_CHIP = 4
VMEM_LIMIT_BYTES = 62 * 1024 * 1024

ADAM_LR = 0.001
ADAM_B1 = 0.9
ADAM_B2 = 0.999
ADAM_EPS = 1e-08
ADAM_WD = 0.01
ADAM_STEP = 10

GELU_K0 = math.sqrt(2.0 / math.pi)
GELU_K1 = 0.044715

NN = (((1,), (0,)), ((), ()))
NT = (((1,), (1,)), ((), ()))
TN = (((0,), (0,)), ((), ()))


def _params(semantics):
    return pltpu.CompilerParams(dimension_semantics=semantics, vmem_limit_bytes=VMEM_LIMIT_BYTES)


def _tile(dim, want):
    if dim <= want:
        return dim
    for t in range(want - want % 8, 0, -8):
        if dim % t == 0:
            return t
    raise ValueError((dim, want))


def _owner_block(ref, j, cols):
    if cols is None:
        return ref.at[j]
    return ref.at[:, pl.ds(pl.multiple_of(j * cols, 128), cols)]


def _sibling_copies(p_ref, got_ref, send_sems, recv_sems, cols):
    x, y, c = _place()
    return [pltpu.make_async_remote_copy(
        src_ref=_owner_block(p_ref, 2 * k + (1 - c), cols), dst_ref=got_ref.at[k], send_sem=send_sems.at[k],
        recv_sem=recv_sems.at[k], device_id=(x, y, 1 - c), device_id_type=MESH) for k in range(N_CHIP)]


class _SideJob(NamedTuple):
    c_idx: jax.Array
    arrays: list
    in_tiles: list
    out_shapes: list
    out_tiles: list
    fn: Callable


def _matmul(name, grid, dims, operands, in_specs, out_shapes, out_specs, epilogue, swap=None, side=None, fill=None):
    n_in = len(operands)
    n_out = len(out_shapes)
    nk = grid[2]
    n_host = 0 if swap is None else 1
    n_si = 0 if side is None else len(side.arrays)
    n_so = 0 if side is None else len(side.out_shapes)
    n_fill = 0 if fill is None else 1
    assert nk == 1 or (n_out == 1 and epilogue is _ident and out_shapes[0].dtype == F32)
    assert fill is None or side is None

    def body(*refs):
        if side is not None:
            refs = refs[1:]
        a_ref, b_ref = refs[0], refs[1]
        extra = refs[2:n_in]
        first_out = n_in + n_si + n_host + n_fill
        outs = refs[first_out:first_out + n_out]
        if side is not None:
            step = (pl.program_id(0) * grid[1] + pl.program_id(1)) * grid[2] + pl.program_id(2)
            side.fn(step, refs[n_in:n_in + n_si], refs[first_out + n_out:first_out + n_out + n_so])
        if swap is not None:
            ids = [pl.program_id(d) for d in range(3)]
            copies = _sibling_copies(refs[n_in + n_si], refs[first_out + n_out + n_so], refs[-2], refs[-1], swap[1])

            @pl.when(functools.reduce(jnp.logical_and, [i == 0 for i in ids]))
            def _():
                for cp in copies:
                    cp.start()

        part = lax.dot_general(a_ref[...], b_ref[...], dims, preferred_element_type=F32)

        def finish(acc):
            res = epilogue(acc, *[e[...] for e in extra])
            for o, r in zip(outs, res):
                o[...] = r.astype(o.dtype)

        if nk == 1:
            finish(part)
        else:
            k = pl.program_id(2)

            @pl.when(k == 0)
            def _():
                outs[0][...] = part

            @pl.when(k > 0)
            def _():
                outs[0][...] += part

        if swap is not None:
            @pl.when(functools.reduce(jnp.logical_and, [i == n - 1 for i, n in zip(ids, grid)]))
            def _():
                for cp in copies:
                    cp.wait()

    scratch = []
    semantics = ("parallel", "parallel", "arbitrary")
    operands, in_specs = list(operands), list(in_specs)
    out_shapes, out_specs = list(out_shapes), list(out_specs)
    if side is not None:
        def with_c(spec):
            return pl.BlockSpec(spec.block_shape, lambda n, m, k, c_ref, f=spec.index_map: f(n, m, k))

        def side_tile(shape, index):
            return pl.BlockSpec(shape, lambda n, m, k, c_ref: index((n * grid[1] + m) * grid[2] + k, c_ref[0]))

        in_specs = [with_c(s) for s in in_specs] + [side_tile(*t) for t in side.in_tiles]
        out_specs = [with_c(s) for s in out_specs] + [side_tile(*t) for t in side.out_tiles]
        operands += side.arrays
        out_shapes += side.out_shapes
        semantics = ("arbitrary", "arbitrary", "arbitrary")
    if swap is not None:
        parts, cols = swap
        got_shape = parts.shape[1:] if cols is None else (parts.shape[0], cols)
        hbm = pl.BlockSpec(memory_space=HBM)
        operands, in_specs = operands + [parts], in_specs + [hbm]
        out_shapes = out_shapes + [jax.ShapeDtypeStruct((N_CHIP,) + got_shape, parts.dtype)]
        out_specs = out_specs + [hbm]
        scratch += [pltpu.SemaphoreType.DMA((N_CHIP,)), pltpu.SemaphoreType.DMA((N_CHIP,))]
        semantics = ("arbitrary", "arbitrary", "arbitrary")
    aliases = {}
    if fill is not None:
        aliases = {len(operands): 0}
        operands, in_specs = operands + [fill], in_specs + [pl.BlockSpec(memory_space=pl.ANY)]
    if side is None:
        return pl.pallas_call(
            body, name=name, grid=grid, in_specs=in_specs, out_specs=out_specs, out_shape=out_shapes,
            scratch_shapes=scratch, input_output_aliases=aliases, compiler_params=_params(semantics),
        )(*operands)
    grid_spec = pltpu.PrefetchScalarGridSpec(
        num_scalar_prefetch=1, grid=grid, in_specs=in_specs, out_specs=out_specs, scratch_shapes=scratch)
    return pl.pallas_call(
        body, name=name, grid_spec=grid_spec, out_shape=out_shapes, compiler_params=_params(semantics),
    )(side.c_idx, *operands)


def _add_job(parts, got, c_idx, steps):
    _, R, C = got.shape
    rows = N_CHIP * R
    tr = rows // steps
    assert rows % steps == 0 and R % tr == 0 and tr % 16 == 0, (rows, steps)
    per = R // tr
    if parts.ndim == 3:
        flat = parts.reshape(N_DEV * R, C)
        parts_index = lambda s, c: ((2 * (s // per) + c) * per + s % per, 0)
    else:
        flat = parts
        parts_index = lambda s, c: (s % per, 2 * (s // per) + c)
    by_step = lambda s, c: (s, 0)

    def add(step, ins, outs):
        outs[0][...] = (ins[0][...].astype(F32) + ins[1][...].astype(F32)).astype(outs[0].dtype)

    return _SideJob(
        c_idx, [flat, got.reshape(rows, C)], [((tr, C), parts_index), ((tr, C), by_step)],
        [jax.ShapeDtypeStruct((rows, C), parts.dtype)], [((tr, C), by_step)], add)


def _ident(acc):
    return (acc,)


ROW_TILE = 256


def _rms_fwd(x, g, name):
    T, D = x.shape
    tr = _tile(T, ROW_TILE)

    def body(x_ref, g_ref, o_ref):
        xv = x_ref[...]
        inv = lax.rsqrt(jnp.mean(xv * xv, axis=-1, keepdims=True) + EPS)
        o_ref[...] = (xv * inv * g_ref[...]).astype(o_ref.dtype)

    return pl.pallas_call(
        body, name=name, grid=(T // tr,),
        in_specs=[pl.BlockSpec((tr, D), lambda i: (i, 0)), pl.BlockSpec((1, D), lambda i: (0, 0))],
        out_specs=pl.BlockSpec((tr, D), lambda i: (i, 0)),
        out_shape=jax.ShapeDtypeStruct((T, D), BF16),
        compiler_params=_params(("parallel",)),
    )(x, g)


def _rms_bwd_rows(i, ins, outs):
    dy_ref, x_ref, g_ref, dres_ref = ins
    dx_ref, gg_ref = outs[0], outs[-1]
    xv = x_ref[...]
    dyv = dy_ref[...]
    inv = lax.rsqrt(jnp.mean(xv * xv, axis=-1, keepdims=True) + EPS)
    gd = dyv * g_ref[...]
    dot = jnp.mean(gd * xv, axis=-1, keepdims=True)
    dx = dres_ref[...] + (inv * gd - xv * (inv * inv * inv * dot))
    dx_ref[...] = dx
    if len(outs) == 3:
        outs[1][...] = dx.astype(BF16)
    part = jnp.sum(dyv * xv * inv, axis=0, keepdims=True)

    @pl.when(i == 0)
    def _():
        gg_ref[...] = part

    @pl.when(i > 0)
    def _():
        gg_ref[...] += part


def _rms_bwd_job(dy, x, g, dres, c_idx, steps):
    T, D = x.shape
    tr = T // steps
    assert T % steps == 0 and tr % 16 == 0, (T, steps)
    rows = ((tr, D), lambda s, c: (s, 0))
    vec = ((1, D), lambda s, c: (0, 0))
    return _SideJob(
        c_idx, [dy, x, g, dres], [rows, rows, vec, rows],
        [jax.ShapeDtypeStruct((T, D), F32), jax.ShapeDtypeStruct((T, D), BF16), jax.ShapeDtypeStruct((1, D), F32)],
        [rows, rows, vec], _rms_bwd_rows)


def _rms_bwd(dy, x, g, dres, name, want_bf16):
    T, D = x.shape
    tr = _tile(T, ROW_TILE)

    def body(dy_ref, x_ref, g_ref, dres_ref, *outs):
        _rms_bwd_rows(pl.program_id(0), (dy_ref, x_ref, g_ref, dres_ref), outs)

    row = pl.BlockSpec((tr, D), lambda i: (i, 0))
    vec = pl.BlockSpec((1, D), lambda i: (0, 0))
    out_shape = [jax.ShapeDtypeStruct((T, D), F32)]
    out_specs = [row]
    if want_bf16:
        out_shape.append(jax.ShapeDtypeStruct((T, D), BF16))
        out_specs.append(row)
    out_shape.append(jax.ShapeDtypeStruct((1, D), F32))
    out_specs.append(vec)
    return pl.pallas_call(
        body, name=name, grid=(T // tr,), in_specs=[row, row, vec, row],
        out_specs=out_specs, out_shape=out_shape, compiler_params=_params(("arbitrary",)),
    )(dy, x, g, dres)


def _loss_head(h1, mlp, target, g):
    T, D = h1.shape
    tr = _tile(T, ROW_TILE)

    def body(h1_ref, mlp_ref, t_ref, g_ref, dh_ref, dhb_ref, gg_ref, loss_ref):
        i = pl.program_id(0)
        hv = h1_ref[...] + mlp_ref[...]
        gv = g_ref[...]
        inv = lax.rsqrt(jnp.mean(hv * hv, axis=-1, keepdims=True) + EPS)
        diff = hv * inv * gv - t_ref[...]
        lpart = 0.5 * jnp.sum(jnp.mean(diff * diff, axis=-1, keepdims=True), axis=0, keepdims=True)
        dout = diff * (1.0 / D)
        gd = dout * gv
        dot = jnp.mean(gd * hv, axis=-1, keepdims=True)
        dh = inv * gd - hv * (inv * inv * inv * dot)
        dh_ref[...] = dh
        dhb_ref[...] = dh.astype(BF16)
        part = jnp.sum(dout * hv * inv, axis=0, keepdims=True)
        lrow = jnp.broadcast_to(lpart, (1, 128))

        @pl.when(i == 0)
        def _():
            gg_ref[...] = part
            loss_ref[...] = lrow

        @pl.when(i > 0)
        def _():
            gg_ref[...] += part
            loss_ref[...] += lrow

    row = pl.BlockSpec((tr, D), lambda i: (i, 0))
    vec = pl.BlockSpec((1, D), lambda i: (0, 0))
    return pl.pallas_call(
        body, name="loss_head", grid=(T // tr,), in_specs=[row, row, row, vec],
        out_specs=[row, row, vec, pl.BlockSpec((1, 128), lambda i: (0, 0))],
        out_shape=[jax.ShapeDtypeStruct((T, D), F32), jax.ShapeDtypeStruct((T, D), BF16),
                   jax.ShapeDtypeStruct((1, D), F32), jax.ShapeDtypeStruct((1, 128), F32)],
        compiler_params=_params(("arbitrary",)),
    )(h1, mlp, target, g)


HALO = 8


def _gelu_parts(x):
    th = jnp.tanh(GELU_K0 * (x + GELU_K1 * (x * x * x)))
    return x * (0.5 * (1.0 + th)), th


def _gelu_grad(x, th):
    return 0.5 * (1.0 + th) + (0.5 * GELU_K0) * x * (1.0 - th * th) * (1.0 + (3.0 * GELU_K1) * (x * x))


def _conv_fwd(b_ref, c_ref, h_ref, ch_ref, hh_ref, w_ref, first):
    tt, wc = c_ref.shape
    c = c_ref[...]
    h = h_ref[...]
    hc = c * h
    prev1 = jnp.where(first, 0.0, ch_ref[HALO - 1:HALO, :] * hh_ref[HALO - 1:HALO, :])
    prev2 = jnp.where(first, 0.0, ch_ref[HALO - 2:HALO - 1, :] * hh_ref[HALO - 2:HALO - 1, :])
    row = lax.broadcasted_iota(jnp.int32, (tt, wc), 0)
    m1 = jnp.where(row == 0, prev1, pltpu.roll(hc, 1, 0))
    m2 = jnp.where(row == 0, prev2, jnp.where(row == 1, prev1, pltpu.roll(hc, 2, 0)))
    conv = w_ref[0:1, :] * m2 + w_ref[1:2, :] * m1 + w_ref[2:3, :] * hc
    return c, h, hc, m1, m2, conv, b_ref[...] * conv


def _tril():
    r = lax.broadcasted_iota(jnp.int32, (HEAD, HEAD), 0)
    s = lax.broadcasted_iota(jnp.int32, (HEAD, HEAD), 1)
    return r >= s


def _spatial_fwd(gvb, sw_ref, sbt_ref, s_scr):
    n_head = sw_ref.shape[0]
    tri = _tril()
    for hd in range(n_head):
        sl = slice(hd * HEAD, (hd + 1) * HEAD)
        wm = jnp.where(tri, sw_ref[hd], 0.0).astype(BF16)
        s_scr[:, sl] = jnp.dot(wm, gvb[:, sl], preferred_element_type=F32) + sbt_ref[:, hd:hd + 1]


def _mixer_specs(wc, row_of):
    def grp(g):
        return pl.BlockSpec((HEAD, wc), lambda *ids: (row_of(*ids), g))

    def halo(g):
        return pl.BlockSpec((HALO, wc), lambda *ids: (jnp.maximum(row_of(*ids) * (HEAD // HALO) - 1, 0), g))

    return grp, halo


def _mixer_fwd(proj, conv_w, sw, sbt, g_a, g_b, n_seq, seq):
    T, w5 = proj.shape
    wc = w5 // 5
    n_head = wc // HEAD
    nt = seq // HEAD

    def row_of(s, i):
        return s * nt + i

    grp, halo = _mixer_specs(wc, row_of)

    def body(b_ref, c_ref, h_ref, u_ref, v_ref, ch_ref, hh_ref, w_ref, sw_ref, sbt_ref, ga_ref, gb_ref,
             y_ref, s_scr):
        i = pl.program_id(1)
        ya = _conv_fwd(b_ref, c_ref, h_ref, ch_ref, hh_ref, w_ref, i == 0)[-1]
        inv = lax.rsqrt(jnp.mean(ya * ya, axis=-1, keepdims=True) + EPS)
        y_ref[:, :wc] = (ya * inv * ga_ref[...]).astype(BF16)
        gu, _ = _gelu_parts(u_ref[...])
        gv, _ = _gelu_parts(v_ref[...])
        _spatial_fwd(gv.astype(BF16), sw_ref, sbt_ref, s_scr)
        yb = gu * s_scr[...]
        inv = lax.rsqrt(jnp.mean(yb * yb, axis=-1, keepdims=True) + EPS)
        y_ref[:, wc:] = (yb * inv * gb_ref[...]).astype(BF16)

    const2 = lambda shape: pl.BlockSpec(shape, lambda s, i: (0, 0))
    return pl.pallas_call(
        body, name="mixer_fwd", grid=(n_seq, nt),
        in_specs=[grp(0), grp(1), grp(2), grp(3), grp(4), halo(1), halo(2),
                  const2((CONV_K, wc)), pl.BlockSpec((n_head, HEAD, HEAD), lambda s, i: (0, 0, 0)),
                  const2((HEAD, n_head)), const2((1, wc)), const2((1, wc))],
        out_specs=pl.BlockSpec((HEAD, 2 * wc), lambda s, i: (s * nt + i, 0)),
        out_shape=jax.ShapeDtypeStruct((T, 2 * wc), BF16),
        scratch_shapes=[pltpu.VMEM((HEAD, wc), F32)],
        compiler_params=_params(("parallel", "parallel")),
    )(proj, proj, proj, proj, proj, proj, proj, conv_w, sw, sbt, g_a, g_b)


def _mixer_bwd(proj, dy, conv_w, sw, sbt, g_a, g_b, n_seq, seq):
    T, w5 = proj.shape
    wc = w5 // 5
    n_head = wc // HEAD
    nt = seq // HEAD
    tt = HEAD

    def row_of(s, ir):
        return s * nt + (nt - 1 - ir)

    grp, halo = _mixer_specs(wc, row_of)

    def body(b_ref, c_ref, h_ref, u_ref, v_ref, ch_ref, hh_ref, dya_ref, dyb_ref, w_ref, sw_ref, sbt_ref,
             ga_ref, gb_ref, dp_ref, gw_ref, gga_ref, ggb_ref, gsw_ref, gsb_ref,
             carry_scr, s_scr, t_scr, dsum_scr):
        s_id = pl.program_id(0)
        ir = pl.program_id(1)
        first_tile = jnp.logical_and(s_id == 0, ir == 0)
        last_tile = jnp.logical_and(s_id == n_seq - 1, ir == nt - 1)

        def conv_part():
            @pl.when(ir == 0)
            def _():
                carry_scr[...] = jnp.zeros_like(carry_scr)

            c, h, hc, m1, m2, conv, ya = _conv_fwd(b_ref, c_ref, h_ref, ch_ref, hh_ref, w_ref, ir == nt - 1)
            inv = lax.rsqrt(jnp.mean(ya * ya, axis=-1, keepdims=True) + EPS)
            dyn = dya_ref[...]
            gd = dyn * ga_ref[...]
            dot = jnp.mean(gd * ya, axis=-1, keepdims=True)
            dya = inv * gd - ya * (inv * inv * inv * dot)
            gg = jnp.sum(dyn * ya * inv, axis=0, keepdims=True)
            dconv = dya * b_ref[...]
            nxt0 = carry_scr[0:1, :]
            nxt1 = carry_scr[1:2, :]
            row = lax.broadcasted_iota(jnp.int32, (tt, wc), 0)
            p1 = jnp.where(row == tt - 1, nxt0, pltpu.roll(dconv, tt - 1, 0))
            p2 = jnp.where(row == tt - 2, nxt0, jnp.where(row == tt - 1, nxt1, pltpu.roll(dconv, tt - 2, 0)))
            dhc = w_ref[2:3, :] * dconv + w_ref[1:2, :] * p1 + w_ref[0:1, :] * p2
            carry_scr[...] = dconv[0:HALO, :]
            dp_ref[:, 0:wc] = (dya * conv).astype(BF16)
            dp_ref[:, wc:2 * wc] = (dhc * h).astype(BF16)
            dp_ref[:, 2 * wc:3 * wc] = (dhc * c).astype(BF16)
            gw0 = jnp.sum(dconv * m2, axis=0, keepdims=True)
            gw1 = jnp.sum(dconv * m1, axis=0, keepdims=True)
            gw2 = jnp.sum(dconv * hc, axis=0, keepdims=True)

            @pl.when(first_tile)
            def _():
                gw_ref[0:1, :] = gw0
                gw_ref[1:2, :] = gw1
                gw_ref[2:3, :] = gw2
                gga_ref[...] = gg

            @pl.when(jnp.logical_not(first_tile))
            def _():
                gw_ref[0:1, :] += gw0
                gw_ref[1:2, :] += gw1
                gw_ref[2:3, :] += gw2
                gga_ref[...] += gg

        def gate_part():
            u = u_ref[...]
            v = v_ref[...]
            gu, thu = _gelu_parts(u)
            gv, thv = _gelu_parts(v)
            gvb = gv.astype(BF16)
            _spatial_fwd(gvb, sw_ref, sbt_ref, s_scr)
            sv = s_scr[...]
            yb = gu * sv
            inv = lax.rsqrt(jnp.mean(yb * yb, axis=-1, keepdims=True) + EPS)
            dyn = dyb_ref[...]
            gd = dyn * gb_ref[...]
            dot = jnp.mean(gd * yb, axis=-1, keepdims=True)
            dyb = inv * gd - yb * (inv * inv * inv * dot)
            gg = jnp.sum(dyn * yb * inv, axis=0, keepdims=True)
            ds = dyb * gu
            dsb = ds.astype(BF16)
            tri = _tril()

            @pl.when(first_tile)
            def _():
                ggb_ref[...] = gg
                dsum_scr[...] = ds
                gsw_ref[...] = jnp.zeros_like(gsw_ref)

            @pl.when(jnp.logical_not(first_tile))
            def _():
                ggb_ref[...] += gg
                dsum_scr[...] += ds

            for hd in range(n_head):
                sl = slice(hd * HEAD, (hd + 1) * HEAD)
                wm = jnp.where(tri, sw_ref[hd], 0.0).astype(BF16)
                t_scr[:, sl] = lax.dot_general(wm, dsb[:, sl], TN, preferred_element_type=F32)
                gsw_ref[hd] += lax.dot_general(dsb[:, sl], gvb[:, sl], NT, preferred_element_type=F32)
            dp_ref[:, 3 * wc:4 * wc] = (dyb * sv * _gelu_grad(u, thu)).astype(BF16)
            dp_ref[:, 4 * wc:5 * wc] = (t_scr[...] * _gelu_grad(v, thv)).astype(BF16)

            @pl.when(last_tile)
            def _():
                for hd in range(n_head):
                    sl = slice(hd * HEAD, (hd + 1) * HEAD)
                    gsw_ref[hd] = jnp.where(tri, gsw_ref[hd], 0.0)
                    gsb_ref[:, hd:hd + 1] = jnp.sum(dsum_scr[:, sl], axis=1, keepdims=True)

        conv_part()
        gate_part()

    const2 = lambda shape: pl.BlockSpec(shape, lambda s, i: (0, 0))
    const3 = pl.BlockSpec((n_head, HEAD, HEAD), lambda s, i: (0, 0, 0))
    dy_spec = lambda col: pl.BlockSpec((tt, wc), lambda s, ir: (row_of(s, ir), col))
    return pl.pallas_call(
        body, name="mixer_bwd", grid=(n_seq, nt),
        in_specs=[grp(0), grp(1), grp(2), grp(3), grp(4), halo(1), halo(2), dy_spec(0), dy_spec(1),
                  const2((CONV_K, wc)), const3, const2((HEAD, n_head)), const2((1, wc)), const2((1, wc))],
        out_specs=[pl.BlockSpec((tt, 5 * wc), lambda s, ir: (row_of(s, ir), 0)),
                   const2((CONV_K, wc)), const2((1, wc)), const2((1, wc)), const3, const2((HEAD, n_head))],
        out_shape=[jax.ShapeDtypeStruct((T, 5 * wc), BF16), jax.ShapeDtypeStruct((CONV_K, wc), F32),
                   jax.ShapeDtypeStruct((1, wc), F32), jax.ShapeDtypeStruct((1, wc), F32),
                   jax.ShapeDtypeStruct((n_head, HEAD, HEAD), F32), jax.ShapeDtypeStruct((HEAD, n_head), F32)],
        scratch_shapes=[pltpu.VMEM((HALO, wc), F32), pltpu.VMEM((tt, wc), F32), pltpu.VMEM((tt, wc), F32),
                        pltpu.VMEM((tt, wc), F32)],
        compiler_params=_params(("arbitrary", "arbitrary")),
    )(proj, proj, proj, proj, proj, proj, proj, dy, dy, conv_w, sw, sbt, g_a, g_b)


def _place():
    return lax.axis_index("x"), lax.axis_index("y"), lax.axis_index("c")


def _other_chips(x, y):
    return [(1 - x, y), (x, 1 - y), (1 - x, 1 - y)]


def _all_gather_async(blk, name, collective_id, side_by_side=False):
    cols = blk.shape[1] if side_by_side else None
    out_shape = (blk.shape[0], N_DEV * cols) if side_by_side else (N_DEV,) + blk.shape

    def body(x_ref, out_ref, send_sems, recv_sems, local_sem):
        x, y, c = _place()
        me, sibling = (x, y, c), (x, y, 1 - c)
        x_nbr, y_nbr, diagonal = (1 - x, y, c), (x, 1 - y, c), (1 - x, 1 - y, c)
        near = (c * x + (1 - c) * (1 - x), c * (1 - y) + (1 - c) * y, c)
        far = ((1 - c) * x + c * (1 - x), (1 - c) * (1 - y) + c * y, c)
        _handshake([sibling, x_nbr, y_nbr])

        def slot(px, py, pc, part=None):
            ref = _owner_block(out_ref, 4 * px + 2 * py + pc, cols)
            return ref if part is None else ref.at[pl.ds(part * part_rows, part_rows)]

        def copy(k, block, to, src=None, part=None):
            return pltpu.make_async_remote_copy(
                src_ref=slot(*block, part) if src is None else src, dst_ref=slot(*block, part),
                send_sem=send_sems.at[k], recv_sem=recv_sems.at[k], device_id=to, device_id_type=MESH)

        mine = pltpu.make_async_copy(x_ref, slot(*me), local_sem)
        mine.start()
        sent = [copy(0, me, sibling, src=x_ref), copy(1, me, x_nbr, src=x_ref), copy(2, me, y_nbr, src=x_ref)]
        for cp in sent:
            cp.start()
        copy(1 + c, near, me).wait_recv()
        sent += [copy(HAND_ON[p], near, far, part=pieces[p]) for p in range(parts)] + [copy(4, near, sibling)]
        for cp in sent[3:]:
            cp.start()
        copy(2 - c, far, me).wait_recv()
        sent.append(copy(5, far, sibling))
        sent[-1].start()
        for p in range(parts):
            copy(HAND_ON[p], diagonal, me, part=pieces[p]).wait_recv()
            sent.append(copy(PASS_ON[p], diagonal, sibling, part=pieces[p]))
            sent[-1].start()
        for k in (0, 4, 5):
            copy(k, sibling, me).wait_recv()
        for p in range(parts):
            copy(PASS_ON[p], sibling, me, part=pieces[p]).wait_recv()
        for cp in sent:
            cp.wait_send()
        mine.wait()

    HAND_ON, PASS_ON = (3, 7, 8, 9), (6, 10, 11, 12)
    parts = 4 if blk.shape[0] % 64 == 0 else 1
    part_rows = blk.shape[0] // parts
    pieces = range(parts) if parts > 1 else [None]
    n_sems = 7 + 2 * (parts - 1)
    return _sequencer_call(
        body, name, collective_id, jax.ShapeDtypeStruct(out_shape, blk.dtype),
        [pltpu.SemaphoreType.DMA((n_sems,)), pltpu.SemaphoreType.DMA((n_sems,)), pltpu.SemaphoreType.DMA], blk)


def _sequencer_call(body, name, collective_id, out_type, scratch_types, operand):
    return pl.kernel(
        body, name=name, out_type=out_type, mesh=plsc.ScalarSubcoreMesh(axis_name="seq_core", num_cores=1),
        scratch_types=scratch_types, compiler_params=pltpu.CompilerParams(collective_id=collective_id),
    )(operand)


def _handshake(peers):
    barrier = pltpu.get_barrier_semaphore()
    for peer in peers:
        pl.semaphore_signal(barrier, inc=1, device_id=peer, device_id_type=MESH)
    pl.semaphore_wait(barrier, len(peers))


def _add_sibling(parts, got, c_idx, name):
    _, R, C = got.shape
    tr, tc = _tile(R, 1024), _tile(C, 2048)

    def body(c_ref, p_ref, g_ref, o_ref):
        o_ref[...] = (p_ref[...].astype(F32) + g_ref[...].astype(F32)).astype(o_ref.dtype)

    if parts.ndim == 3:
        parts_spec = pl.BlockSpec((None, tr, tc), lambda k, i, j, c_ref: (2 * k + c_ref[0], i, j))
    else:
        parts_spec = pl.BlockSpec((tr, tc), lambda k, i, j, c_ref: (i, (2 * k + c_ref[0]) * (C // tc) + j))
    grid_spec = pltpu.PrefetchScalarGridSpec(
        num_scalar_prefetch=1, grid=(N_CHIP, R // tr, C // tc),
        in_specs=[parts_spec, pl.BlockSpec((None, tr, tc), lambda k, i, j, c_ref: (k, i, j))],
        out_specs=pl.BlockSpec((None, tr, tc), lambda k, i, j, c_ref: (k, i, j)))
    return pl.pallas_call(
        body, name=name, grid_spec=grid_spec, out_shape=jax.ShapeDtypeStruct((N_CHIP, R, C), parts.dtype),
        compiler_params=_params(("parallel", "parallel", "parallel")),
    )(c_idx, parts, got)


def _scatter_to_chips(sums, name, collective_id):
    _, R, C = sums.shape

    def body(q_ref, got_ref, send_sems, recv_sems, local_sem):
        x, y, c = _place()
        _handshake([(*chip, c) for chip in _other_chips(x, y)])
        my_chip = 2 * x + y
        mine = pltpu.make_async_copy(q_ref.at[my_chip], got_ref.at[my_chip], local_sem)
        mine.start()
        copies = [pltpu.make_async_remote_copy(
            src_ref=q_ref.at[2 * px + py], dst_ref=got_ref.at[my_chip], send_sem=send_sems.at[j],
            recv_sem=recv_sems.at[j], device_id=(px, py, c), device_id_type=MESH)
            for j, (px, py) in enumerate(_other_chips(x, y))]
        for cp in copies:
            cp.start()
        for cp in copies:
            cp.wait()
        mine.wait()

    return _sequencer_call(
        body, name, collective_id, jax.ShapeDtypeStruct((N_CHIP, R, C), sums.dtype),
        [pltpu.SemaphoreType.DMA((3,)), pltpu.SemaphoreType.DMA((3,)), pltpu.SemaphoreType.DMA], sums)


def _adamw_math(w, g, m, v):
    m = ADAM_B1 * m + (1.0 - ADAM_B1) * g
    v = ADAM_B2 * v + (1.0 - ADAM_B2) * (g * g)
    m_hat = m / (1.0 - ADAM_B1 ** ADAM_STEP)
    v_hat = v / (1.0 - ADAM_B2 ** ADAM_STEP)
    delta = -ADAM_LR * (m_hat / (jnp.sqrt(v_hat) + ADAM_EPS) + ADAM_WD * w)
    return delta, m, v


def _sum_adamw_tiles(p, w, m, v):
    g = p[0].astype(F32)
    for k in range(1, p.shape[0]):
        g = g + p[k].astype(F32)
    delta, mn, vn = _adamw_math(w, g, m, v)
    return g, delta, mn, vn


def _sum_adamw(parts, w, m, v, name):
    n_parts, R, C = parts.shape
    tr, tc = _tile(R, 512), _tile(C, 1024)

    def body(p_ref, w_ref, m_ref, v_ref, g_out, d_out, m_out, v_out):
        res = _sum_adamw_tiles(p_ref[...], w_ref[...], m_ref[...], v_ref[...])
        for o, r in zip((g_out, d_out, m_out, v_out), res):
            o[...] = r

    blk = pl.BlockSpec((tr, tc), lambda i, j: (i, j))
    shp = jax.ShapeDtypeStruct((R, C), F32)
    return pl.pallas_call(
        body, name=name, grid=(R // tr, C // tc),
        in_specs=[pl.BlockSpec((n_parts, tr, tc), lambda i, j: (0, i, j)), blk, blk, blk],
        out_specs=[blk, blk, blk, blk], out_shape=[shp, shp, shp, shp],
        compiler_params=_params(("parallel", "parallel")),
    )(parts, w, m, v)


def _after(value, dep):
    return lax.optimization_barrier((value, dep))[0]


def _rows128(a):
    return a.reshape(-1, 128)


def kernel(x, mix_norm_g, w_in, conv_w, spatial_w, spatial_b, conv_out_norm_g, gmlp_out_norm_g, w_out, mlp_norm_g, w_up, w_down, final_norm_g, loss_target, m_mix_norm_g, m_w_in, m_conv_w, m_spatial_w, m_spatial_b, m_conv_out_norm_g, m_gmlp_out_norm_g, m_w_out, m_mlp_norm_g, m_w_up, m_w_down, m_final_norm_g, v_mix_norm_g, v_w_in, v_conv_w, v_spatial_w, v_spatial_b, v_conv_out_norm_g, v_gmlp_out_norm_g, v_w_out, v_mlp_norm_g, v_w_up, v_w_down, v_final_norm_g):
    n_seq, seq, D = x.shape
    T = n_seq * seq
    n_in = w_in.shape[2]
    n_out = w_out.shape[1]
    n_up = w_up.shape[2]
    wc = conv_w.shape[2] * N_DEV
    n_head = wc // HEAD
    FF = n_up * N_DEV
    assert N_DEV * n_in == 5 * wc and seq % HEAD == 0 and D == 2 * wc

    c_idx = lax.axis_index("c").astype(jnp.int32).reshape(1)
    my_dev = 4 * lax.axis_index("x") + 2 * lax.axis_index("y") + lax.axis_index("c")

    xf = x.reshape(T, D)
    tgt = loss_target.reshape(T, D)

    cast = lambda w: w[0].astype(BF16)
    win_g = _all_gather_async(cast(w_in), "ag_w_in", 0, side_by_side=True)
    cw_pad = jnp.pad(conv_w[0], ((0, HALO - CONV_K), (0, 0)))
    cw_g = _all_gather_async(cw_pad, "ag_conv_w", 9)
    conv_full = jnp.transpose(cw_g[:, :CONV_K, :], (1, 0, 2)).reshape(CONV_K, wc)
    wout_g = _all_gather_async(cast(w_out), "ag_w_out", 1).reshape(D, D)
    wup_g = _all_gather_async(cast(w_up), "ag_w_up", 2, side_by_side=True)
    wdown_g = _all_gather_async(cast(w_down), "ag_w_down", 3).reshape(FF, D)

    sw = spatial_w[0]
    sbt = spatial_b[0].T
    g_mix, g_a, g_b, g_mlp = mix_norm_g, conv_out_norm_g, gmlp_out_norm_g, mlp_norm_g
    g_fin = final_norm_g.reshape(1, D)

    IN = N_DEV * n_in
    bp = _tile(T, 1024)
    bm = _tile(T, 1024)
    bn = _tile(D, 1024)
    bu = _tile(FF, 1024)
    bw = _tile(D, 512)
    bg = _tile(D, 1024)
    k_ff = _tile(FF, 4096)
    k_in = _tile(IN, 5120)

    def tile(shape, index):
        return pl.BlockSpec(shape, index)

    by_m0 = lambda n, m, k: (m, 0)
    by_0n = lambda n, m, k: (0, n)
    by_n0 = lambda n, m, k: (n, 0)
    by_mn = lambda n, m, k: (m, n)
    by_mk = lambda n, m, k: (m, k)
    by_kn = lambda n, m, k: (k, n)
    by_nk = lambda n, m, k: (n, k)
    by_0m = lambda n, m, k: (0, m)
    f32_td = [jax.ShapeDtypeStruct((T, D), F32)]

    xn = _rms_fwd(xf, g_mix, "norm_mix")
    proj = _matmul(
        "proj", (N_DEV, T // bp, 1), NN, [xn, win_g], [tile((bp, D), by_m0), tile((D, n_in), by_0n)],
        [jax.ShapeDtypeStruct((T, IN), F32)], [tile((bp, n_in), by_mn)], _ident)[0]
    y = _mixer_fwd(proj, conv_full, sw, sbt, g_a, g_b, n_seq, seq)
    h1 = _matmul(
        "out_proj", (D // bn, T // bp, 1), NN, [y, wout_g, xf],
        [tile((bp, D), by_m0), tile((D, bn), by_0n), tile((bp, bn), by_mn)],
        f32_td, [tile((bp, bn), by_mn)], lambda acc, res: (res + acc,))[0]
    xn2 = _rms_fwd(h1, g_mlp, "norm_mlp")

    def up_epilogue(acc):
        r = jnp.maximum(acc, 0.0)
        return acc, r * r

    up, act = _matmul(
        "up_proj", (FF // bu, T // bm, 1), NN, [xn2, wup_g], [tile((bm, D), by_m0), tile((D, bu), by_0n)],
        [jax.ShapeDtypeStruct((T, FF), BF16)] * 2, [tile((bm, bu), by_mn)] * 2, up_epilogue)
    mlp = _matmul(
        "down_proj", (D // bn, T // bm, FF // k_ff), NN, [act, wdown_g],
        [tile((bm, k_ff), by_mk), tile((k_ff, bn), by_kn)], f32_td, [tile((bm, bn), by_mn)], _ident)[0]

    dh2, dh2b, gg_fin, loss_row = _loss_head(h1, mlp, tgt, g_fin)

    gp_down = _matmul(
        "gw_down", (D // bn, FF // bg, 1), TN, [act, dh2b], [tile((T, bg), by_0m), tile((T, bn), by_0n)],
        [jax.ShapeDtypeStruct((FF, D), BF16)], [tile((bg, bn), by_mn)], _ident)[0].reshape(N_DEV, n_up, D)
    dup, got_down = _matmul(
        "d_act", (FF // bu, T // bm, 1), NT, [dh2b, wdown_g, up],
        [tile((bm, D), by_m0), tile((bu, D), by_n0), tile((bm, bu), by_mn)],
        [jax.ShapeDtypeStruct((T, FF), BF16)], [tile((bm, bu), by_mn)],
        lambda acc, u: (acc * (2.0 * jnp.maximum(u.astype(F32), 0.0)),), swap=(gp_down, None))
    dxn2, sums_down = _matmul(
        "d_xn2", (D // bn, T // bm, FF // k_ff), NT, [dup, wup_g],
        [tile((bm, k_ff), by_mk), tile((bn, k_ff), by_nk)], f32_td, [tile((bm, bn), by_mn)], _ident,
        side=_add_job(gp_down, got_down, c_idx, (D // bn) * (T // bm) * (FF // k_ff)))
    sums_down = sums_down.reshape(N_CHIP, n_up, D)
    four_down = _scatter_to_chips(sums_down, "rs_chips_w_down", 4)
    gp_up, dh1, dh1b, gg_mlp = _matmul(
        "gw_up", (FF // bu, D // bg, 1), TN, [xn2, _after(dup, sums_down)],
        [tile((T, bg), by_0m), tile((T, bu), by_0n)],
        [jax.ShapeDtypeStruct((D, FF), BF16)], [tile((bg, bu), by_mn)], _ident,
        side=_rms_bwd_job(dxn2, h1, g_mlp, dh2, c_idx, (FF // bu) * (D // bg)))

    dy, got_up = _matmul(
        "d_y", (D // bn, T // bm, 1), NT, [dh1b, wout_g], [tile((bm, D), by_m0), tile((bn, D), by_n0)],
        f32_td, [tile((bm, bn), by_mn)], _ident, swap=(gp_up, n_up))
    gp_out, sums_up = _matmul(
        "gw_out", (D // bn, D // bw, 1), TN, [y, _after(dh1b, dy)], [tile((T, bw), by_0m), tile((T, bn), by_0n)],
        [jax.ShapeDtypeStruct((D, D), BF16)], [tile((bw, bn), by_mn)], _ident,
        side=_add_job(gp_up, got_up, c_idx, (D // bn) * (D // bw)))
    gp_out = gp_out.reshape(N_DEV, n_out, D)
    sums_up = sums_up.reshape(N_CHIP, D, n_up)
    four_up = _scatter_to_chips(sums_up, "rs_chips_w_up", 5)
    dproj, gl_conv, gl_a, gl_b, gl_sw, gl_sbt = _mixer_bwd(
        proj, _after(dy, sums_up), conv_full, sw, sbt, g_a, g_b, n_seq, seq)
    gp_in, got_out = _matmul(
        "gw_in", (N_DEV, D // bg, 1), TN, [xn, dproj], [tile((T, bg), by_0m), tile((T, n_in), by_0n)],
        [jax.ShapeDtypeStruct((D, IN), BF16)], [tile((bg, n_in), by_mn)], _ident, swap=(gp_out, None))
    sums_out = _add_sibling(gp_out, got_out, c_idx, "rs_add_w_out")
    four_out = _scatter_to_chips(sums_out, "rs_chips_w_out", 6)
    bh = _tile(T // 2, bm)
    hm = (T // 2) // bh
    dproj = _after(dproj, sums_out)
    dxn, got_in = _matmul(
        "d_xn_top", (D // bn, hm, IN // k_in), NT, [dproj, win_g],
        [tile((bh, k_in), by_mk), tile((bn, k_in), by_nk)], f32_td, [tile((bh, bn), by_mn)], _ident,
        swap=(gp_in, n_in))
    sums_in = _add_sibling(gp_in, got_in, c_idx, "rs_add_w_in")
    four_in = _scatter_to_chips(sums_in, "rs_chips_w_in", 7)
    dxn = _matmul(
        "d_xn_bottom", (D // bn, hm, IN // k_in), NT, [_after(dproj, sums_in), win_g],
        [tile((bh, k_in), lambda n, m, k: (m + hm, k)), tile((bn, k_in), by_nk)], f32_td,
        [tile((bh, bn), lambda n, m, k: (m + hm, n))], _ident, fill=dxn)[0]
    grad_x, gg_mix = _rms_bwd(_after(dxn, sums_in), xf, g_mix, dh1, "norm_mix_bwd", False)

    outs = {}
    done = grad_x
    for tag, four, w, m, v in (("w_down", four_down, w_down, m_w_down, v_w_down),
                               ("w_up", four_up, w_up, m_w_up, v_w_up),
                               ("w_out", four_out, w_out, m_w_out, v_w_out),
                               ("w_in", four_in, w_in, m_w_in, v_w_in)):
        res = _sum_adamw(_after(four, done), w[0], m[0], v[0], "adamw_" + tag)
        done = res[0]
        outs[tag] = [a[None] for a in res]

    small = [("mix_norm_g", gg_mix, mix_norm_g, m_mix_norm_g, v_mix_norm_g),
             ("conv_w", gl_conv, None, None, None),
             ("spatial_w", gl_sw, spatial_w, m_spatial_w, v_spatial_w),
             ("spatial_b", gl_sbt.T, spatial_b, m_spatial_b, v_spatial_b),
             ("conv_out_norm_g", gl_a, conv_out_norm_g, m_conv_out_norm_g, v_conv_out_norm_g),
             ("gmlp_out_norm_g", gl_b, gmlp_out_norm_g, m_gmlp_out_norm_g, v_gmlp_out_norm_g),
             ("mlp_norm_g", gg_mlp, mlp_norm_g, m_mlp_norm_g, v_mlp_norm_g),
             ("final_norm_g", gg_fin, final_norm_g, m_final_norm_g, v_final_norm_g),
             ("loss", jnp.broadcast_to(loss_row, (64, 128)), None, None, None)]
    packed_g = jnp.concatenate([_rows128(g) for _, g, _, _, _ in small], axis=0)
    pack = lambda idx: jnp.concatenate(
        [jnp.zeros((item[1].size // 128, 128), F32) if item[2] is None else _rows128(item[idx])
         for item in small], axis=0)
    all_g = _all_gather_async(packed_g, "ag_small_grads", 8)
    sg, sd, sm, sv = _sum_adamw(_after(all_g, done), pack(2), pack(3), pack(4), "adamw_small")
    row = 0
    for name, g, w, _, _ in small:
        n_rows = g.size // 128
        if w is not None:
            outs[name] = [a[row:row + n_rows].reshape(w.shape) for a in (sg, sd, sm, sv)]
        elif name == "conv_w":
            conv_grad_full = sg[row:row + n_rows].reshape(CONV_K, wc)
        else:
            loss = sg[row, 0]
        row += n_rows
    cpd = wc // N_DEV
    conv_grad = lax.dynamic_slice(conv_grad_full, (0, my_dev * cpd), (CONV_K, cpd))
    pad8 = lambda a: jnp.pad(a, ((0, HALO - CONV_K), (0, 0)))
    outs["conv_w"] = [a[:CONV_K][None] for a in _sum_adamw(
        pad8(conv_grad)[None], pad8(conv_w[0]), pad8(m_conv_w[0]), pad8(v_conv_w[0]), "adamw_conv_w")]

    order = ["mix_norm_g", "w_in", "conv_w", "spatial_w", "spatial_b", "conv_out_norm_g", "gmlp_out_norm_g",
             "w_out", "mlp_norm_g", "w_up", "w_down", "final_norm_g"]
    result = [loss, grad_x.reshape(n_seq, seq, D)]
    for k in range(4):
        result += [outs[n][k] for n in order]
    return tuple(result)
```

```python
import functools
import math
from typing import Callable, NamedTuple

import jax
import jax.numpy as jnp
from jax import lax
from jax.experimental import pallas as pl
from jax.experimental.pallas import tpu as pltpu
from jax.experimental.pallas import tpu_sc as plsc

F32 = jnp.float32
BF16 = jnp.bfloat16
MESH = pl.DeviceIdType.MESH
HBM = pltpu.HBM

EPS = 1e-5
HEAD = 128
CONV_K = 3
N_DEV = 8
N_CHIP = 4
VMEM_LIMIT_BYTES = 62 * 1024 * 1024
STREAM_IN = pl.Buffered(3)

ADAM_LR = 0.001
ADAM_B1 = 0.9
ADAM_B2 = 0.999
ADAM_EPS = 1e-08
ADAM_WD = 0.01
ADAM_STEP = 10

GELU_K0 = math.sqrt(2.0 / math.pi)
GELU_K1 = 0.044715

NN = (((1,), (0,)), ((), ()))
NT = (((1,), (1,)), ((), ()))
TN = (((0,), (0,)), ((), ()))


def _params(semantics):
    return pltpu.CompilerParams(dimension_semantics=semantics, vmem_limit_bytes=VMEM_LIMIT_BYTES)


def _tile(dim, want):
    if dim <= want:
        return dim
    for t in range(want - want % 8, 0, -8):
        if dim % t == 0:
            return t
    raise ValueError((dim, want))


def _owner_block(ref, j, cols):
    if cols is None:
        return ref.at[j]
    return ref.at[:, pl.ds(pl.multiple_of(j * cols, 128), cols)]


def _sibling_copies(p_ref, got_ref, send_sems, recv_sems, cols):
    x, y, c = _place()
    return [pltpu.make_async_remote_copy(
        src_ref=_owner_block(p_ref, 2 * k + (1 - c), cols), dst_ref=got_ref.at[k], send_sem=send_sems.at[k],
        recv_sem=recv_sems.at[k], device_id=(x, y, 1 - c), device_id_type=MESH) for k in range(N_CHIP)]


class _SideJob(NamedTuple):
    c_idx: jax.Array
    arrays: list
    in_tiles: list
    out_shapes: list
    out_tiles: list
    fn: Callable


def _matmul(name, grid, dims, operands, in_specs, out_shapes, out_specs, epilogue, swap=None, side=None, fill=None):
    n_in = len(operands)
    n_out = len(out_shapes)
    nk = grid[2]
    n_host = 0 if swap is None else 1
    n_si = 0 if side is None else len(side.arrays)
    n_so = 0 if side is None else len(side.out_shapes)
    n_fill = 0 if fill is None else 1
    assert nk == 1 or (n_out == 1 and epilogue is _ident and out_shapes[0].dtype == F32)
    assert fill is None or side is None

    def body(*refs):
        if side is not None:
            refs = refs[1:]
        a_ref, b_ref = refs[0], refs[1]
        extra = refs[2:n_in]
        first_out = n_in + n_si + n_host + n_fill
        outs = refs[first_out:first_out + n_out]
        if side is not None:
            step = (pl.program_id(0) * grid[1] + pl.program_id(1)) * grid[2] + pl.program_id(2)
            side.fn(step, refs[n_in:n_in + n_si], refs[first_out + n_out:first_out + n_out + n_so])
        if swap is not None:
            ids = [pl.program_id(d) for d in range(3)]
            copies = _sibling_copies(refs[n_in + n_si], refs[first_out + n_out + n_so], refs[-2], refs[-1], swap[1])

            @pl.when(functools.reduce(jnp.logical_and, [i == 0 for i in ids]))
            def _():
                for cp in copies:
                    cp.start()

        part = lax.dot_general(a_ref[...], b_ref[...], dims, preferred_element_type=F32)

        def finish(acc):
            res = epilogue(acc, *[e[...] for e in extra])
            for o, r in zip(outs, res):
                o[...] = r.astype(o.dtype)

        if nk == 1:
            finish(part)
        else:
            k = pl.program_id(2)

            @pl.when(k == 0)
            def _():
                outs[0][...] = part

            @pl.when(k > 0)
            def _():
                outs[0][...] += part

        if swap is not None:
            @pl.when(functools.reduce(jnp.logical_and, [i == n - 1 for i, n in zip(ids, grid)]))
            def _():
                for cp in copies:
                    cp.wait()

    scratch = []
    semantics = ("parallel", "parallel", "arbitrary")
    operands, in_specs = list(operands), list(in_specs)
    out_shapes, out_specs = list(out_shapes), list(out_specs)
    if side is not None:
        def with_c(spec):
            return pl.BlockSpec(spec.block_shape, lambda n, m, k, c_ref, f=spec.index_map: f(n, m, k))

        def side_tile(shape, index):
            return pl.BlockSpec(shape, lambda n, m, k, c_ref: index((n * grid[1] + m) * grid[2] + k, c_ref[0]))

        in_specs = [with_c(s) for s in in_specs] + [side_tile(*t) for t in side.in_tiles]
        out_specs = [with_c(s) for s in out_specs] + [side_tile(*t) for t in side.out_tiles]
        operands += side.arrays
        out_shapes += side.out_shapes
        semantics = ("arbitrary", "arbitrary", "arbitrary")
    if swap is not None:
        parts, cols = swap
        got_shape = parts.shape[1:] if cols is None else (parts.shape[0], cols)
        hbm = pl.BlockSpec(memory_space=HBM)
        operands, in_specs = operands + [parts], in_specs + [hbm]
        out_shapes = out_shapes + [jax.ShapeDtypeStruct((N_CHIP,) + got_shape, parts.dtype)]
        out_specs = out_specs + [hbm]
        scratch += [pltpu.SemaphoreType.DMA((N_CHIP,)), pltpu.SemaphoreType.DMA((N_CHIP,))]
        semantics = ("arbitrary", "arbitrary", "arbitrary")
    aliases = {}
    if fill is not None:
        aliases = {len(operands): 0}
        operands, in_specs = operands + [fill], in_specs + [pl.BlockSpec(memory_space=pl.ANY)]
    if side is None:
        return pl.pallas_call(
            body, name=name, grid=grid, in_specs=in_specs, out_specs=out_specs, out_shape=out_shapes,
            scratch_shapes=scratch, input_output_aliases=aliases, compiler_params=_params(semantics),
        )(*operands)
    grid_spec = pltpu.PrefetchScalarGridSpec(
        num_scalar_prefetch=1, grid=grid, in_specs=in_specs, out_specs=out_specs, scratch_shapes=scratch)
    return pl.pallas_call(
        body, name=name, grid_spec=grid_spec, out_shape=out_shapes, compiler_params=_params(semantics),
    )(side.c_idx, *operands)


def _add_job(parts, got, c_idx, steps):
    _, R, C = got.shape
    rows = N_CHIP * R
    tr = rows // steps
    assert rows % steps == 0 and R % tr == 0 and tr % 16 == 0, (rows, steps)
    per = R // tr
    if parts.ndim == 3:
        flat = parts.reshape(N_DEV * R, C)
        parts_index = lambda s, c: ((2 * (s // per) + c) * per + s % per, 0)
    else:
        flat = parts
        parts_index = lambda s, c: (s % per, 2 * (s // per) + c)
    by_step = lambda s, c: (s, 0)

    def add(step, ins, outs):
        outs[0][...] = (ins[0][...].astype(F32) + ins[1][...].astype(F32)).astype(outs[0].dtype)

    return _SideJob(
        c_idx, [flat, got.reshape(rows, C)], [((tr, C), parts_index), ((tr, C), by_step)],
        [jax.ShapeDtypeStruct((rows, C), parts.dtype)], [((tr, C), by_step)], add)


def _ident(acc):
    return (acc,)


ROW_TILE = 256


def _rms_fwd(x, g, name):
    T, D = x.shape
    tr = _tile(T, ROW_TILE)

    def body(x_ref, g_ref, o_ref):
        xv = x_ref[...]
        inv = lax.rsqrt(jnp.mean(xv * xv, axis=-1, keepdims=True) + EPS)
        o_ref[...] = (xv * inv * g_ref[...]).astype(o_ref.dtype)

    return pl.pallas_call(
        body, name=name, grid=(T // tr,),
        in_specs=[pl.BlockSpec((tr, D), lambda i: (i, 0)), pl.BlockSpec((1, D), lambda i: (0, 0))],
        out_specs=pl.BlockSpec((tr, D), lambda i: (i, 0)),
        out_shape=jax.ShapeDtypeStruct((T, D), BF16),
        compiler_params=_params(("parallel",)),
    )(x, g)


def _rms_bwd_rows(i, ins, outs):
    dy_ref, x_ref, g_ref, dres_ref = ins
    dx_ref, gg_ref = outs[0], outs[-1]
    xv = x_ref[...]
    dyv = dy_ref[...]
    inv = lax.rsqrt(jnp.mean(xv * xv, axis=-1, keepdims=True) + EPS)
    gd = dyv * g_ref[...]
    dot = jnp.mean(gd * xv, axis=-1, keepdims=True)
    dx = dres_ref[...] + (inv * gd - xv * (inv * inv * inv * dot))
    dx_ref[...] = dx
    if len(outs) == 3:
        outs[1][...] = dx.astype(BF16)
    part = jnp.sum(dyv * xv * inv, axis=0, keepdims=True)

    @pl.when(i == 0)
    def _():
        gg_ref[...] = part

    @pl.when(i > 0)
    def _():
        gg_ref[...] += part


def _rms_bwd_job(dy, x, g, dres, c_idx, steps):
    T, D = x.shape
    tr = T // steps
    assert T % steps == 0 and tr % 16 == 0, (T, steps)
    rows = ((tr, D), lambda s, c: (s, 0))
    vec = ((1, D), lambda s, c: (0, 0))
    return _SideJob(
        c_idx, [dy, x, g, dres], [rows, rows, vec, rows],
        [jax.ShapeDtypeStruct((T, D), F32), jax.ShapeDtypeStruct((T, D), BF16), jax.ShapeDtypeStruct((1, D), F32)],
        [rows, rows, vec], _rms_bwd_rows)


def _rms_bwd(dy, x, g, dres, name, want_bf16):
    T, D = x.shape
    tr = _tile(T, ROW_TILE)

    def body(dy_ref, x_ref, g_ref, dres_ref, *outs):
        _rms_bwd_rows(pl.program_id(0), (dy_ref, x_ref, g_ref, dres_ref), outs)

    row = pl.BlockSpec((tr, D), lambda i: (i, 0))
    vec = pl.BlockSpec((1, D), lambda i: (0, 0))
    out_shape = [jax.ShapeDtypeStruct((T, D), F32)]
    out_specs = [row]
    if want_bf16:
        out_shape.append(jax.ShapeDtypeStruct((T, D), BF16))
        out_specs.append(row)
    out_shape.append(jax.ShapeDtypeStruct((1, D), F32))
    out_specs.append(vec)
    return pl.pallas_call(
        body, name=name, grid=(T // tr,), in_specs=[row, row, vec, row],
        out_specs=out_specs, out_shape=out_shape, compiler_params=_params(("arbitrary",)),
    )(dy, x, g, dres)


def _loss_head(h1, mlp, target, g):
    T, D = h1.shape
    tr = _tile(T, ROW_TILE)

    def body(h1_ref, mlp_ref, t_ref, g_ref, dh_ref, dhb_ref, gg_ref, loss_ref):
        i = pl.program_id(0)
        hv = h1_ref[...] + mlp_ref[...]
        gv = g_ref[...]
        inv = lax.rsqrt(jnp.mean(hv * hv, axis=-1, keepdims=True) + EPS)
        diff = hv * inv * gv - t_ref[...]
        lpart = 0.5 * jnp.sum(jnp.mean(diff * diff, axis=-1, keepdims=True), axis=0, keepdims=True)
        dout = diff * (1.0 / D)
        gd = dout * gv
        dot = jnp.mean(gd * hv, axis=-1, keepdims=True)
        dh = inv * gd - hv * (inv * inv * inv * dot)
        dh_ref[...] = dh
        dhb_ref[...] = dh.astype(BF16)
        part = jnp.sum(dout * hv * inv, axis=0, keepdims=True)
        lrow = jnp.broadcast_to(lpart, (1, 128))

        @pl.when(i == 0)
        def _():
            gg_ref[...] = part
            loss_ref[...] = lrow

        @pl.when(i > 0)
        def _():
            gg_ref[...] += part
            loss_ref[...] += lrow

    row = pl.BlockSpec((tr, D), lambda i: (i, 0))
    vec = pl.BlockSpec((1, D), lambda i: (0, 0))
    return pl.pallas_call(
        body, name="loss_head", grid=(T // tr,), in_specs=[row, row, row, vec],
        out_specs=[row, row, vec, pl.BlockSpec((1, 128), lambda i: (0, 0))],
        out_shape=[jax.ShapeDtypeStruct((T, D), F32), jax.ShapeDtypeStruct((T, D), BF16),
                   jax.ShapeDtypeStruct((1, D), F32), jax.ShapeDtypeStruct((1, 128), F32)],
        compiler_params=_params(("arbitrary",)),
    )(h1, mlp, target, g)


HALO = 8


def _gelu_parts(x):
    th = jnp.tanh(GELU_K0 * (x + GELU_K1 * (x * x * x)))
    return x * (0.5 * (1.0 + th)), th


def _gelu_grad(x, th):
    return 0.5 * (1.0 + th) + (0.5 * GELU_K0) * x * (1.0 - th * th) * (1.0 + (3.0 * GELU_K1) * (x * x))


def _conv_fwd(b_ref, c_ref, h_ref, ch_ref, hh_ref, w_ref, first):
    tt, wc = c_ref.shape
    c = c_ref[...]
    h = h_ref[...]
    hc = c * h
    prev1 = jnp.where(first, 0.0, ch_ref[HALO - 1:HALO, :] * hh_ref[HALO - 1:HALO, :])
    prev2 = jnp.where(first, 0.0, ch_ref[HALO - 2:HALO - 1, :] * hh_ref[HALO - 2:HALO - 1, :])
    row = lax.broadcasted_iota(jnp.int32, (tt, wc), 0)
    m1 = jnp.where(row == 0, prev1, pltpu.roll(hc, 1, 0))
    m2 = jnp.where(row == 0, prev2, jnp.where(row == 1, prev1, pltpu.roll(hc, 2, 0)))
    conv = w_ref[0:1, :] * m2 + w_ref[1:2, :] * m1 + w_ref[2:3, :] * hc
    return c, h, hc, m1, m2, conv, b_ref[...] * conv


def _tril():
    r = lax.broadcasted_iota(jnp.int32, (HEAD, HEAD), 0)
    s = lax.broadcasted_iota(jnp.int32, (HEAD, HEAD), 1)
    return r >= s


def _spatial_fwd(gvb, sw_ref, sbt_ref, s_scr):
    n_head = sw_ref.shape[0]
    tri = _tril()
    for hd in range(n_head):
        sl = slice(hd * HEAD, (hd + 1) * HEAD)
        wm = jnp.where(tri, sw_ref[hd], 0.0).astype(BF16)
        s_scr[:, sl] = jnp.dot(wm, gvb[:, sl], preferred_element_type=F32) + sbt_ref[:, hd:hd + 1]


def _mixer_specs(wc, row_of):
    def grp(g):
        return pl.BlockSpec((HEAD, wc), lambda *ids: (row_of(*ids), g))

    def halo(g):
        return pl.BlockSpec((HALO, wc), lambda *ids: (jnp.maximum(row_of(*ids) * (HEAD // HALO) - 1, 0), g))

    return grp, halo


def _mixer_fwd(proj, conv_w, sw, sbt, g_a, g_b, n_seq, seq):
    T, w5 = proj.shape
    wc = w5 // 5
    n_head = wc // HEAD
    nt = seq // HEAD

    def row_of(s, i):
        return s * nt + i

    grp, halo = _mixer_specs(wc, row_of)

    def body(b_ref, c_ref, h_ref, u_ref, v_ref, ch_ref, hh_ref, w_ref, sw_ref, sbt_ref, ga_ref, gb_ref,
             y_ref, s_scr):
        i = pl.program_id(1)
        ya = _conv_fwd(b_ref, c_ref, h_ref, ch_ref, hh_ref, w_ref, i == 0)[-1]
        inv = lax.rsqrt(jnp.mean(ya * ya, axis=-1, keepdims=True) + EPS)
        y_ref[:, :wc] = (ya * inv * ga_ref[...]).astype(BF16)
        gu, _ = _gelu_parts(u_ref[...])
        gv, _ = _gelu_parts(v_ref[...])
        _spatial_fwd(gv.astype(BF16), sw_ref, sbt_ref, s_scr)
        yb = gu * s_scr[...]
        inv = lax.rsqrt(jnp.mean(yb * yb, axis=-1, keepdims=True) + EPS)
        y_ref[:, wc:] = (yb * inv * gb_ref[...]).astype(BF16)

    const2 = lambda shape: pl.BlockSpec(shape, lambda s, i: (0, 0))
    return pl.pallas_call(
        body, name="mixer_fwd", grid=(n_seq, nt),
        in_specs=[grp(0), grp(1), grp(2), grp(3), grp(4), halo(1), halo(2),
                  const2((CONV_K, wc)), pl.BlockSpec((n_head, HEAD, HEAD), lambda s, i: (0, 0, 0)),
                  const2((HEAD, n_head)), const2((1, wc)), const2((1, wc))],
        out_specs=pl.BlockSpec((HEAD, 2 * wc), lambda s, i: (s * nt + i, 0)),
        out_shape=jax.ShapeDtypeStruct((T, 2 * wc), BF16),
        scratch_shapes=[pltpu.VMEM((HEAD, wc), F32)],
        compiler_params=_params(("parallel", "parallel")),
    )(proj, proj, proj, proj, proj, proj, proj, conv_w, sw, sbt, g_a, g_b)


def _mixer_bwd(proj, dy, conv_w, sw, sbt, g_a, g_b, n_seq, seq):
    T, w5 = proj.shape
    wc = w5 // 5
    n_head = wc // HEAD
    nt = seq // HEAD
    tt = HEAD

    def row_of(s, ir):
        return s * nt + (nt - 1 - ir)

    grp, halo = _mixer_specs(wc, row_of)

    def body(b_ref, c_ref, h_ref, u_ref, v_ref, ch_ref, hh_ref, dya_ref, dyb_ref, w_ref, sw_ref, sbt_ref,
             ga_ref, gb_ref, dp_ref, gw_ref, gga_ref, ggb_ref, gsw_ref, gsb_ref,
             carry_scr, s_scr, t_scr, dsum_scr):
        s_id = pl.program_id(0)
        ir = pl.program_id(1)
        first_tile = jnp.logical_and(s_id == 0, ir == 0)
        last_tile = jnp.logical_and(s_id == n_seq - 1, ir == nt - 1)

        def conv_part():
            @pl.when(ir == 0)
            def _():
                carry_scr[...] = jnp.zeros_like(carry_scr)

            c, h, hc, m1, m2, conv, ya = _conv_fwd(b_ref, c_ref, h_ref, ch_ref, hh_ref, w_ref, ir == nt - 1)
            inv = lax.rsqrt(jnp.mean(ya * ya, axis=-1, keepdims=True) + EPS)
            dyn = dya_ref[...]
            gd = dyn * ga_ref[...]
            dot = jnp.mean(gd * ya, axis=-1, keepdims=True)
            dya = inv * gd - ya * (inv * inv * inv * dot)
            gg = jnp.sum(dyn * ya * inv, axis=0, keepdims=True)
            dconv = dya * b_ref[...]
            nxt0 = carry_scr[0:1, :]
            nxt1 = carry_scr[1:2, :]
            row = lax.broadcasted_iota(jnp.int32, (tt, wc), 0)
            p1 = jnp.where(row == tt - 1, nxt0, pltpu.roll(dconv, tt - 1, 0))
            p2 = jnp.where(row == tt - 2, nxt0, jnp.where(row == tt - 1, nxt1, pltpu.roll(dconv, tt - 2, 0)))
            dhc = w_ref[2:3, :] * dconv + w_ref[1:2, :] * p1 + w_ref[0:1, :] * p2
            carry_scr[...] = dconv[0:HALO, :]
            dp_ref[:, 0:wc] = (dya * conv).astype(BF16)
            dp_ref[:, wc:2 * wc] = (dhc * h).astype(BF16)
            dp_ref[:, 2 * wc:3 * wc] = (dhc * c).astype(BF16)
            gw0 = jnp.sum(dconv * m2, axis=0, keepdims=True)
            gw1 = jnp.sum(dconv * m1, axis=0, keepdims=True)
            gw2 = jnp.sum(dconv * hc, axis=0, keepdims=True)

            @pl.when(first_tile)
            def _():
                gw_ref[0:1, :] = gw0
                gw_ref[1:2, :] = gw1
                gw_ref[2:3, :] = gw2
                gga_ref[...] = gg

            @pl.when(jnp.logical_not(first_tile))
            def _():
                gw_ref[0:1, :] += gw0
                gw_ref[1:2, :] += gw1
                gw_ref[2:3, :] += gw2
                gga_ref[...] += gg

        def gate_part():
            u = u_ref[...]
            v = v_ref[...]
            gu, thu = _gelu_parts(u)
            gv, thv = _gelu_parts(v)
            gvb = gv.astype(BF16)
            _spatial_fwd(gvb, sw_ref, sbt_ref, s_scr)
            sv = s_scr[...]
            yb = gu * sv
            inv = lax.rsqrt(jnp.mean(yb * yb, axis=-1, keepdims=True) + EPS)
            dyn = dyb_ref[...]
            gd = dyn * gb_ref[...]
            dot = jnp.mean(gd * yb, axis=-1, keepdims=True)
            dyb = inv * gd - yb * (inv * inv * inv * dot)
            gg = jnp.sum(dyn * yb * inv, axis=0, keepdims=True)
            ds = dyb * gu
            dsb = ds.astype(BF16)
            tri = _tril()

            @pl.when(first_tile)
            def _():
                ggb_ref[...] = gg
                dsum_scr[...] = ds
                gsw_ref[...] = jnp.zeros_like(gsw_ref)

            @pl.when(jnp.logical_not(first_tile))
            def _():
                ggb_ref[...] += gg
                dsum_scr[...] += ds

            for hd in range(n_head):
                sl = slice(hd * HEAD, (hd + 1) * HEAD)
                wm = jnp.where(tri, sw_ref[hd], 0.0).astype(BF16)
                t_scr[:, sl] = lax.dot_general(wm, dsb[:, sl], TN, preferred_element_type=F32)
                gsw_ref[hd] += lax.dot_general(dsb[:, sl], gvb[:, sl], NT, preferred_element_type=F32)
            dp_ref[:, 3 * wc:4 * wc] = (dyb * sv * _gelu_grad(u, thu)).astype(BF16)
            dp_ref[:, 4 * wc:5 * wc] = (t_scr[...] * _gelu_grad(v, thv)).astype(BF16)

            @pl.when(last_tile)
            def _():
                for hd in range(n_head):
                    sl = slice(hd * HEAD, (hd + 1) * HEAD)
                    gsw_ref[hd] = jnp.where(tri, gsw_ref[hd], 0.0)
                    gsb_ref[:, hd:hd + 1] = jnp.sum(dsum_scr[:, sl], axis=1, keepdims=True)

        conv_part()
        gate_part()

    const2 = lambda shape: pl.BlockSpec(shape, lambda s, i: (0, 0))
    const3 = pl.BlockSpec((n_head, HEAD, HEAD), lambda s, i: (0, 0, 0))
    dy_spec = lambda col: pl.BlockSpec((tt, wc), lambda s, ir: (row_of(s, ir), col))
    return pl.pallas_call(
        body, name="mixer_bwd", grid=(n_seq, nt),
        in_specs=[grp(0), grp(1), grp(2), grp(3), grp(4), halo(1), halo(2), dy_spec(0), dy_spec(1),
                  const2((CONV_K, wc)), const3, const2((HEAD, n_head)), const2((1, wc)), const2((1, wc))],
        out_specs=[pl.BlockSpec((tt, 5 * wc), lambda s, ir: (row_of(s, ir), 0)),
                   const2((CONV_K, wc)), const2((1, wc)), const2((1, wc)), const3, const2((HEAD, n_head))],
        out_shape=[jax.ShapeDtypeStruct((T, 5 * wc), BF16), jax.ShapeDtypeStruct((CONV_K, wc), F32),
                   jax.ShapeDtypeStruct((1, wc), F32), jax.ShapeDtypeStruct((1, wc), F32),
                   jax.ShapeDtypeStruct((n_head, HEAD, HEAD), F32), jax.ShapeDtypeStruct((HEAD, n_head), F32)],
        scratch_shapes=[pltpu.VMEM((HALO, wc), F32), pltpu.VMEM((tt, wc), F32), pltpu.VMEM((tt, wc), F32),
                        pltpu.VMEM((tt, wc), F32)],
        compiler_params=_params(("arbitrary", "arbitrary")),
    )(proj, proj, proj, proj, proj, proj, proj, dy, dy, conv_w, sw, sbt, g_a, g_b)


def _place():
    return lax.axis_index("x"), lax.axis_index("y"), lax.axis_index("c")


def _other_chips(x, y):
    return [(1 - x, y), (x, 1 - y), (1 - x, 1 - y)]


def _all_gather_async(blk, name, collective_id, side_by_side=False):
    cols = blk.shape[1] if side_by_side else None
    out_shape = (blk.shape[0], N_DEV * cols) if side_by_side else (N_DEV,) + blk.shape

    def body(x_ref, out_ref, send_sems, recv_sems, local_sem):
        x, y, c = _place()
        me, sibling = (x, y, c), (x, y, 1 - c)
        x_nbr, y_nbr, diagonal = (1 - x, y, c), (x, 1 - y, c), (1 - x, 1 - y, c)
        near = (c * x + (1 - c) * (1 - x), c * (1 - y) + (1 - c) * y, c)
        far = ((1 - c) * x + c * (1 - x), (1 - c) * (1 - y) + c * y, c)
        _handshake([sibling, x_nbr, y_nbr])

        def slot(px, py, pc, part=None):
            ref = _owner_block(out_ref, 4 * px + 2 * py + pc, cols)
            return ref if part is None else ref.at[pl.ds(part * part_rows, part_rows)]

        def copy(k, block, to, src=None, part=None):
            return pltpu.make_async_remote_copy(
                src_ref=slot(*block, part) if src is None else src, dst_ref=slot(*block, part),
                send_sem=send_sems.at[k], recv_sem=recv_sems.at[k], device_id=to, device_id_type=MESH)

        mine = pltpu.make_async_copy(x_ref, slot(*me), local_sem)
        mine.start()
        sent = [copy(0, me, sibling, src=x_ref), copy(1, me, x_nbr, src=x_ref), copy(2, me, y_nbr, src=x_ref)]
        for cp in sent:
            cp.start()
        copy(1 + c, near, me).wait_recv()
        sent += [copy(HAND_ON[p], near, far, part=pieces[p]) for p in range(parts)] + [copy(4, near, sibling)]
        for cp in sent[3:]:
            cp.start()
        copy(2 - c, far, me).wait_recv()
        sent.append(copy(5, far, sibling))
        sent[-1].start()
        for p in range(parts):
            copy(HAND_ON[p], diagonal, me, part=pieces[p]).wait_recv()
            sent.append(copy(PASS_ON[p], diagonal, sibling, part=pieces[p]))
            sent[-1].start()
        for k in (0, 4, 5):
            copy(k, sibling, me).wait_recv()
        for p in range(parts):
            copy(PASS_ON[p], sibling, me, part=pieces[p]).wait_recv()
        for cp in sent:
            cp.wait_send()
        mine.wait()

    HAND_ON, PASS_ON = (3, 7, 8, 9), (6, 10, 11, 12)
    parts = 4 if blk.shape[0] % 64 == 0 else 1
    part_rows = blk.shape[0] // parts
    pieces = range(parts) if parts > 1 else [None]
    n_sems = 7 + 2 * (parts - 1)
    return _sequencer_call(
        body, name, collective_id, jax.ShapeDtypeStruct(out_shape, blk.dtype),
        [pltpu.SemaphoreType.DMA((n_sems,)), pltpu.SemaphoreType.DMA((n_sems,)), pltpu.SemaphoreType.DMA], blk)


def _sequencer_call(body, name, collective_id, out_type, scratch_types, operand):
    return pl.kernel(
        body, name=name, out_type=out_type, mesh=plsc.ScalarSubcoreMesh(axis_name="seq_core", num_cores=1),
        scratch_types=scratch_types, compiler_params=pltpu.CompilerParams(collective_id=collective_id),
    )(operand)


def _handshake(peers):
    barrier = pltpu.get_barrier_semaphore()
    for peer in peers:
        pl.semaphore_signal(barrier, inc=1, device_id=peer, device_id_type=MESH)
    pl.semaphore_wait(barrier, len(peers))


def _add_sibling(parts, got, c_idx, name):
    _, R, C = got.shape
    tr, tc = _tile(R, 1024), _tile(C, 2048)

    def body(c_ref, p_ref, g_ref, o_ref):
        o_ref[...] = (p_ref[...].astype(F32) + g_ref[...].astype(F32)).astype(o_ref.dtype)

    if parts.ndim == 3:
        parts_spec = pl.BlockSpec((None, tr, tc), lambda k, i, j, c_ref: (2 * k + c_ref[0], i, j))
    else:
        parts_spec = pl.BlockSpec((tr, tc), lambda k, i, j, c_ref: (i, (2 * k + c_ref[0]) * (C // tc) + j))
    grid_spec = pltpu.PrefetchScalarGridSpec(
        num_scalar_prefetch=1, grid=(N_CHIP, R // tr, C // tc),
        in_specs=[parts_spec, pl.BlockSpec((None, tr, tc), lambda k, i, j, c_ref: (k, i, j))],
        out_specs=pl.BlockSpec((None, tr, tc), lambda k, i, j, c_ref: (k, i, j)))
    return pl.pallas_call(
        body, name=name, grid_spec=grid_spec, out_shape=jax.ShapeDtypeStruct((N_CHIP, R, C), parts.dtype),
        compiler_params=_params(("parallel", "parallel", "parallel")),
    )(c_idx, parts, got)


def _scatter_to_chips(sums, name, collective_id):
    _, R, C = sums.shape

    def body(q_ref, got_ref, send_sems, recv_sems, local_sem):
        x, y, c = _place()
        _handshake([(*chip, c) for chip in _other_chips(x, y)])
        my_chip = 2 * x + y
        mine = pltpu.make_async_copy(q_ref.at[my_chip], got_ref.at[my_chip], local_sem)
        mine.start()
        copies = [pltpu.make_async_remote_copy(
            src_ref=q_ref.at[2 * px + py], dst_ref=got_ref.at[my_chip], send_sem=send_sems.at[j],
            recv_sem=recv_sems.at[j], device_id=(px, py, c), device_id_type=MESH)
            for j, (px, py) in enumerate(_other_chips(x, y))]
        for cp in copies:
            cp.start()
        for cp in copies:
            cp.wait()
        mine.wait()

    return _sequencer_call(
        body, name, collective_id, jax.ShapeDtypeStruct((N_CHIP, R, C), sums.dtype),
        [pltpu.SemaphoreType.DMA((3,)), pltpu.SemaphoreType.DMA((3,)), pltpu.SemaphoreType.DMA], sums)


def _adamw_math(w, g, m, v):
    m = ADAM_B1 * m + (1.0 - ADAM_B1) * g
    v = ADAM_B2 * v + (1.0 - ADAM_B2) * (g * g)
    m_hat = m / (1.0 - ADAM_B1 ** ADAM_STEP)
    v_hat = v / (1.0 - ADAM_B2 ** ADAM_STEP)
    delta = -ADAM_LR * (m_hat / (jnp.sqrt(v_hat) + ADAM_EPS) + ADAM_WD * w)
    return delta, m, v


def _sum_adamw_tiles(p, w, m, v):
    g = p[0].astype(F32)
    for k in range(1, p.shape[0]):
        g = g + p[k].astype(F32)
    delta, mn, vn = _adamw_math(w, g, m, v)
    return g, delta, mn, vn


def _sum_adamw(parts, w, m, v, name):
    n_parts, R, C = parts.shape
    tr, tc = _tile(R, 512), _tile(C, 1024)

    def body(p_ref, w_ref, m_ref, v_ref, g_out, d_out, m_out, v_out):
        res = _sum_adamw_tiles(p_ref[...], w_ref[...], m_ref[...], v_ref[...])
        for o, r in zip((g_out, d_out, m_out, v_out), res):
            o[...] = r

    blk = pl.BlockSpec((tr, tc), lambda i, j: (i, j))
    blk_in = pl.BlockSpec((tr, tc), lambda i, j: (i, j), pipeline_mode=STREAM_IN)
    parts_in = pl.BlockSpec((n_parts, tr, tc), lambda i, j: (0, i, j), pipeline_mode=STREAM_IN)
    shp = jax.ShapeDtypeStruct((R, C), F32)

    def cell(*refs):
        pltpu.emit_pipeline(
            body, grid=(R // tr, C // tc), in_specs=[parts_in, blk_in, blk_in, blk_in],
            out_specs=[blk, blk, blk, blk])(*refs)

    anywhere = pl.BlockSpec(memory_space=pl.ANY)
    return pl.pallas_call(
        cell, name=name, in_specs=[anywhere] * 4, out_specs=[anywhere] * 4, out_shape=[shp, shp, shp, shp],
        compiler_params=pltpu.CompilerParams(vmem_limit_bytes=VMEM_LIMIT_BYTES),
    )(parts, w, m, v)


def _after(value, dep):
    return lax.optimization_barrier((value, dep))[0]


def _rows128(a):
    return a.reshape(-1, 128)


def kernel(x, mix_norm_g, w_in, conv_w, spatial_w, spatial_b, conv_out_norm_g, gmlp_out_norm_g, w_out, mlp_norm_g, w_up, w_down, final_norm_g, loss_target, m_mix_norm_g, m_w_in, m_conv_w, m_spatial_w, m_spatial_b, m_conv_out_norm_g, m_gmlp_out_norm_g, m_w_out, m_mlp_norm_g, m_w_up, m_w_down, m_final_norm_g, v_mix_norm_g, v_w_in, v_conv_w, v_spatial_w, v_spatial_b, v_conv_out_norm_g, v_gmlp_out_norm_g, v_w_out, v_mlp_norm_g, v_w_up, v_w_down, v_final_norm_g):
    n_seq, seq, D = x.shape
    T = n_seq * seq
    n_in = w_in.shape[2]
    n_out = w_out.shape[1]
    n_up = w_up.shape[2]
    wc = conv_w.shape[2] * N_DEV
    n_head = wc // HEAD
    FF = n_up * N_DEV
    assert N_DEV * n_in == 5 * wc and seq % HEAD == 0 and D == 2 * wc

    c_idx = lax.axis_index("c").astype(jnp.int32).reshape(1)
    my_dev = 4 * lax.axis_index("x") + 2 * lax.axis_index("y") + lax.axis_index("c")

    xf = x.reshape(T, D)
    tgt = loss_target.reshape(T, D)

    cast = lambda w: w[0].astype(BF16)
    win_g = _all_gather_async(cast(w_in), "ag_w_in", 0, side_by_side=True)
    cw_pad = jnp.pad(conv_w[0], ((0, HALO - CONV_K), (0, 0)))
    cw_g = _all_gather_async(cw_pad, "ag_conv_w", 9)
    conv_full = jnp.transpose(cw_g[:, :CONV_K, :], (1, 0, 2)).reshape(CONV_K, wc)
    wout_g = _all_gather_async(cast(w_out), "ag_w_out", 1).reshape(D, D)
    wup_g = _all_gather_async(cast(w_up), "ag_w_up", 2, side_by_side=True)
    wdown_g = _all_gather_async(cast(w_down), "ag_w_down", 3).reshape(FF, D)

    sw = spatial_w[0]
    sbt = spatial_b[0].T
    g_mix, g_a, g_b, g_mlp = mix_norm_g, conv_out_norm_g, gmlp_out_norm_g, mlp_norm_g
    g_fin = final_norm_g.reshape(1, D)

    IN = N_DEV * n_in
    bp = _tile(T, 1024)
    bm = _tile(T, 1024)
    bn = _tile(D, 1024)
    bu = _tile(FF, 1024)
    bw = _tile(D, 512)
    bg = _tile(D, 1024)
    k_ff = _tile(FF, 4096)
    k_in = _tile(IN, 5120)

    def tile(shape, index):
        return pl.BlockSpec(shape, index)

    by_m0 = lambda n, m, k: (m, 0)
    by_0n = lambda n, m, k: (0, n)
    by_n0 = lambda n, m, k: (n, 0)
    by_mn = lambda n, m, k: (m, n)
    by_mk = lambda n, m, k: (m, k)
    by_kn = lambda n, m, k: (k, n)
    by_nk = lambda n, m, k: (n, k)
    by_0m = lambda n, m, k: (0, m)
    f32_td = [jax.ShapeDtypeStruct((T, D), F32)]

    xn = _rms_fwd(xf, g_mix, "norm_mix")
    proj = _matmul(
        "proj", (N_DEV, T // bp, 1), NN, [xn, win_g], [tile((bp, D), by_m0), tile((D, n_in), by_0n)],
        [jax.ShapeDtypeStruct((T, IN), F32)], [tile((bp, n_in), by_mn)], _ident)[0]
    y = _mixer_fwd(proj, conv_full, sw, sbt, g_a, g_b, n_seq, seq)
    h1 = _matmul(
        "out_proj", (D // bn, T // bp, 1), NN, [y, wout_g, xf],
        [tile((bp, D), by_m0), tile((D, bn), by_0n), tile((bp, bn), by_mn)],
        f32_td, [tile((bp, bn), by_mn)], lambda acc, res: (res + acc,))[0]
    xn2 = _rms_fwd(h1, g_mlp, "norm_mlp")

    def up_epilogue(acc):
        r = jnp.maximum(acc, 0.0)
        return acc, r * r

    up, act = _matmul(
        "up_proj", (FF // bu, T // bm, 1), NN, [xn2, wup_g], [tile((bm, D), by_m0), tile((D, bu), by_0n)],
        [jax.ShapeDtypeStruct((T, FF), BF16)] * 2, [tile((bm, bu), by_mn)] * 2, up_epilogue)
    mlp = _matmul(
        "down_proj", (D // bn, T // bm, FF // k_ff), NN, [act, wdown_g],
        [tile((bm, k_ff), by_mk), tile((k_ff, bn), by_kn)], f32_td, [tile((bm, bn), by_mn)], _ident)[0]

    dh2, dh2b, gg_fin, loss_row = _loss_head(h1, mlp, tgt, g_fin)

    gp_down = _matmul(
        "gw_down", (D // bn, FF // bg, 1), TN, [act, dh2b], [tile((T, bg), by_0m), tile((T, bn), by_0n)],
        [jax.ShapeDtypeStruct((FF, D), BF16)], [tile((bg, bn), by_mn)], _ident)[0].reshape(N_DEV, n_up, D)
    dup, got_down = _matmul(
        "d_act", (FF // bu, T // bm, 1), NT, [dh2b, wdown_g, up],
        [tile((bm, D), by_m0), tile((bu, D), by_n0), tile((bm, bu), by_mn)],
        [jax.ShapeDtypeStruct((T, FF), BF16)], [tile((bm, bu), by_mn)],
        lambda acc, u: (acc * (2.0 * jnp.maximum(u.astype(F32), 0.0)),), swap=(gp_down, None))
    dxn2, sums_down = _matmul(
        "d_xn2", (D // bn, T // bm, FF // k_ff), NT, [dup, wup_g],
        [tile((bm, k_ff), by_mk), tile((bn, k_ff), by_nk)], f32_td, [tile((bm, bn), by_mn)], _ident,
        side=_add_job(gp_down, got_down, c_idx, (D // bn) * (T // bm) * (FF // k_ff)))
    sums_down = sums_down.reshape(N_CHIP, n_up, D)
    four_down = _scatter_to_chips(sums_down, "rs_chips_w_down", 4)
    gp_up, dh1, dh1b, gg_mlp = _matmul(
        "gw_up", (FF // bu, D // bg, 1), TN, [xn2, _after(dup, sums_down)],
        [tile((T, bg), by_0m), tile((T, bu), by_0n)],
        [jax.ShapeDtypeStruct((D, FF), BF16)], [tile((bg, bu), by_mn)], _ident,
        side=_rms_bwd_job(dxn2, h1, g_mlp, dh2, c_idx, (FF // bu) * (D // bg)))

    dy, got_up = _matmul(
        "d_y", (D // bn, T // bm, 1), NT, [dh1b, wout_g], [tile((bm, D), by_m0), tile((bn, D), by_n0)],
        f32_td, [tile((bm, bn), by_mn)], _ident, swap=(gp_up, n_up))
    gp_out, sums_up = _matmul(
        "gw_out", (D // bn, D // bw, 1), TN, [y, _after(dh1b, dy)], [tile((T, bw), by_0m), tile((T, bn), by_0n)],
        [jax.ShapeDtypeStruct((D, D), BF16)], [tile((bw, bn), by_mn)], _ident,
        side=_add_job(gp_up, got_up, c_idx, (D // bn) * (D // bw)))
    gp_out = gp_out.reshape(N_DEV, n_out, D)
    sums_up = sums_up.reshape(N_CHIP, D, n_up)
    four_up = _scatter_to_chips(sums_up, "rs_chips_w_up", 5)
    dproj, gl_conv, gl_a, gl_b, gl_sw, gl_sbt = _mixer_bwd(
        proj, _after(dy, sums_up), conv_full, sw, sbt, g_a, g_b, n_seq, seq)
    gp_in, got_out = _matmul(
        "gw_in", (N_DEV, D // bg, 1), TN, [xn, dproj], [tile((T, bg), by_0m), tile((T, n_in), by_0n)],
        [jax.ShapeDtypeStruct((D, IN), BF16)], [tile((bg, n_in), by_mn)], _ident, swap=(gp_out, None))
    sums_out = _add_sibling(gp_out, got_out, c_idx, "rs_add_w_out")
    four_out = _scatter_to_chips(sums_out, "rs_chips_w_out", 6)
    bh = _tile(T // 2, bm)
    hm = (T // 2) // bh
    dproj = _after(dproj, sums_out)
    dxn, got_in = _matmul(
        "d_xn_top", (D // bn, hm, IN // k_in), NT, [dproj, win_g],
        [tile((bh, k_in), by_mk), tile((bn, k_in), by_nk)], f32_td, [tile((bh, bn), by_mn)], _ident,
        swap=(gp_in, n_in))
    sums_in = _add_sibling(gp_in, got_in, c_idx, "rs_add_w_in")
    four_in = _scatter_to_chips(sums_in, "rs_chips_w_in", 7)
    dxn = _matmul(
        "d_xn_bottom", (D // bn, hm, IN // k_in), NT, [_after(dproj, sums_in), win_g],
        [tile((bh, k_in), lambda n, m, k: (m + hm, k)), tile((bn, k_in), by_nk)], f32_td,
        [tile((bh, bn), lambda n, m, k: (m + hm, n))], _ident, fill=dxn)[0]
    grad_x, gg_mix = _rms_bwd(_after(dxn, sums_in), xf, g_mix, dh1, "norm_mix_bwd", False)

    outs = {}
    done = grad_x
    for tag, four, w, m, v in (("w_down", four_down, w_down, m_w_down, v_w_down),
                               ("w_up", four_up, w_up, m_w_up, v_w_up),
                               ("w_out", four_out, w_out, m_w_out, v_w_out),
                               ("w_in", four_in, w_in, m_w_in, v_w_in)):
        res = _sum_adamw(_after(four, done), w[0], m[0], v[0], "adamw_" + tag)
        done = res[0]
        outs[tag] = [a[None] for a in res]

    small = [("mix_norm_g", gg_mix, mix_norm_g, m_mix_norm_g, v_mix_norm_g),
             ("conv_w", gl_conv, None, None, None),
             ("spatial_w", gl_sw, spatial_w, m_spatial_w, v_spatial_w),
             ("spatial_b", gl_sbt.T, spatial_b, m_spatial_b, v_spatial_b),
             ("conv_out_norm_g", gl_a, conv_out_norm_g, m_conv_out_norm_g, v_conv_out_norm_g),
             ("gmlp_out_norm_g", gl_b, gmlp_out_norm_g, m_gmlp_out_norm_g, v_gmlp_out_norm_g),
             ("mlp_norm_g", gg_mlp, mlp_norm_g, m_mlp_norm_g, v_mlp_norm_g),
             ("final_norm_g", gg_fin, final_norm_g, m_final_norm_g, v_final_norm_g),
             ("loss", jnp.broadcast_to(loss_row, (64, 128)), None, None, None)]
    packed_g = jnp.concatenate([_rows128(g) for _, g, _, _, _ in small], axis=0)
    pack = lambda idx: jnp.concatenate(
        [jnp.zeros((item[1].size // 128, 128), F32) if item[2] is None else _rows128(item[idx])
         for item in small], axis=0)
    all_g = _all_gather_async(packed_g, "ag_small_grads", 8)
    sg, sd, sm, sv = _sum_adamw(_after(all_g, done), pack(2), pack(3), pack(4), "adamw_small")
    row = 0
    for name, g, w, _, _ in small:
        n_rows = g.size // 128
        if w is not None:
            outs[name] = [a[row:row + n_rows].reshape(w.shape) for a in (sg, sd, sm, sv)]
        elif name == "conv_w":
            conv_grad_full = sg[row:row + n_rows].reshape(CONV_K, wc)
        else:
            loss = sg[row, 0]
        row += n_rows
    cpd = wc // N_DEV
    conv_grad = lax.dynamic_slice(conv_grad_full, (0, my_dev * cpd), (CONV_K, cpd))
    pad8 = lambda a: jnp.pad(a, ((0, HALO - CONV_K), (0, 0)))
    outs["conv_w"] = [a[:CONV_K][None] for a in _sum_adamw(
        pad8(conv_grad)[None], pad8(conv_w[0]), pad8(m_conv_w[0]), pad8(v_conv_w[0]), "adamw_conv_w")]

    order = ["mix_norm_g", "w_in", "conv_w", "spatial_w", "spatial_b", "conv_out_norm_g", "gmlp_out_norm_g",
             "w_out", "mlp_norm_g", "w_up", "w_down", "final_norm_g"]
    result = [loss, grad_x.reshape(n_seq, seq, D)]
    for k in range(4):
        result += [outs[n][k] for n in order]
    return tuple(result)
```
